```python
import math
import jax, jax.numpy as jnp
from jax import lax
import numpy as np

D_MODEL = 2048
BATCH = 32
SEQ = 256
DEPTH = 2
DEC_BATCH = 8
DEC_SEQ = 4096
PAST_LEN = 256

GRID_W = 64
ROPE_BASE = 10000.0
EPS = 1e-6
Q_BLOCK = 128
N_BRANCH = 3
N_ADA = 6
FOURIER_GROUPS = 4
FOURIER_CH = 128
FOURIER_W = FOURIER_GROUPS * FOURIER_CH
DIFF_HEADS = 4
DIFF_HD = 64
DIFF_QK_W = DIFF_HEADS * 2 * DIFF_HD
DIFF_V_W = DIFF_HEADS * 2 * DIFF_HD
MLA_HEADS = 8
MLA_NOPE = 64
MLA_ROPE = 32
MLA_QK_HD = MLA_NOPE + MLA_ROPE
MLA_V = 64
MLA_Q_LORA = 384
MLA_KV_LORA = 256
MLA_W = MLA_HEADS * MLA_V
N_EXPERTS = 16
EXPERT_FF = 1024
EC_FACTOR = 2
IN_COLS = FOURIER_W + 2 * DIFF_QK_W + DIFF_V_W + MLA_Q_LORA + MLA_KV_LORA + MLA_ROPE + N_BRANCH * D_MODEL

kernel_name = 'hybrid_fourier_diffattn_mla_ecmoe_dit_step'


def in_split_points():
    sizes = [FOURIER_W, DIFF_QK_W, DIFF_QK_W, DIFF_V_W, MLA_Q_LORA, MLA_KV_LORA, MLA_ROPE]
    pts, acc = [], 0
    for s in sizes:
        acc += s
        pts.append(acc)
    return pts


def rmsnorm(x, w):
    xf = x.astype(jnp.float32)
    y = xf * lax.rsqrt(jnp.mean(xf * xf, axis=-1, keepdims=True) + EPS)
    return (y * w.astype(jnp.float32)).astype(x.dtype)


def _rotate(x, ang):
    d2 = x.shape[-1] // 2
    shape = (x.shape[1],) + (1,) * (x.ndim - 3) + (d2,)
    cos = jnp.cos(ang).reshape(shape)
    sin = jnp.sin(ang).reshape(shape)
    x1, x2 = x[..., :d2], x[..., d2:]
    return jnp.concatenate([x1 * cos - x2 * sin, x1 * sin + x2 * cos], axis=-1)


def rope_2d(x):
    L, R = x.shape[1], x.shape[-1]
    rows = L // GRID_W
    row = jnp.repeat(jnp.arange(rows, dtype=jnp.float32), GRID_W)
    col = jnp.tile(jnp.arange(GRID_W, dtype=jnp.float32), rows)
    half = R // 2
    nf = half // 2
    inv = ROPE_BASE ** (-jnp.arange(nf, dtype=jnp.float32) / nf)
    xf = x.astype(jnp.float32)
    out = jnp.concatenate([_rotate(xf[..., :half], row[:, None] * inv),
                           _rotate(xf[..., half:], col[:, None] * inv)], axis=-1)
    return out.astype(x.dtype)


def sweep_query_blocks(fn, q):
    B, Lq = q.shape[0], q.shape[1]
    nb = Lq // Q_BLOCK
    qb = jnp.moveaxis(q.reshape((B, nb, Q_BLOCK) + q.shape[2:]), 1, 0)
    ob = lax.map(fn, qb)
    return jnp.moveaxis(ob, 0, 1).reshape((B, Lq) + ob.shape[3:])


def diff_attention(q, k, v, lam):
    scale = DIFF_HD ** -0.5
    def block(qb):
        s = jnp.einsum('bqhid,bkhid->bhiqk', qb, k, preferred_element_type=jnp.float32) * scale
        p = jax.nn.softmax(s, axis=-1)
        a = p[:, :, 0] - lam * p[:, :, 1]
        return jnp.einsum('bhqk,bkhe->bqhe', a.astype(v.dtype), v)
    return sweep_query_blocks(block, q)


def softmax_attention(q, k, v, scale):
    def block(qb):
        s = jnp.einsum('bqhd,bkhd->bhqk', qb, k, preferred_element_type=jnp.float32) * scale
        p = jax.nn.softmax(s, axis=-1)
        return jnp.einsum('bhqk,bkhe->bqhe', p.astype(v.dtype), v)
    return sweep_query_blocks(block, q)


def fourier_mix(u):
    B, L, _ = u.shape
    ug = u.reshape(B, L, FOURIER_GROUPS, FOURIER_CH).astype(jnp.float32)
    f = jnp.fft.fft2(ug, axes=(1, 3), norm='ortho')
    return jnp.real(f).astype(u.dtype).reshape(B, L, FOURIER_W)


def mla_keys(c_kv, k_rope, w_kvb, knorm_w):
    B, L, _ = c_kv.shape
    kv = (c_kv @ w_kvb).reshape(B, L, MLA_HEADS, MLA_NOPE + MLA_V)
    k_shared = jnp.broadcast_to(k_rope[:, :, None, :], (B, L, MLA_HEADS, MLA_ROPE))
    k = rmsnorm(jnp.concatenate([kv[..., :MLA_NOPE], k_shared], axis=-1), knorm_w)
    return k, kv[..., MLA_NOPE:]


def mixer(h, lp, lam_init, ctx):
    B, L, _ = h.shape
    u_f, q_d, k_d, v_d, q_a, kv_a, k_r, g = jnp.split(h @ lp['w_in'], in_split_points(), axis=-1)
    y_f = fourier_mix(u_f) @ lp['w_br_fourier']
    q_d = rmsnorm(q_d.reshape(B, L, DIFF_HEADS, 2, DIFF_HD), lp['diff_qnorm_w'])
    k_d = rmsnorm(k_d.reshape(B, L, DIFF_HEADS, 2, DIFF_HD), lp['diff_knorm_w'])
    v_d = v_d.reshape(B, L, DIFF_HEADS, 2 * DIFF_HD)
    c_q = rmsnorm(q_a, lp['mla_qa_norm_w'])
    q_m = rmsnorm((c_q @ lp['mla_w_qb']).reshape(B, L, MLA_HEADS, MLA_QK_HD), lp['mla_qnorm_w'])
    c_kv = rmsnorm(kv_a, lp['mla_kva_norm_w'])
    k_m, v_m = mla_keys(c_kv, k_r, lp['mla_w_kvb'], lp['mla_knorm_w'])
    if ctx is None:
        kd_all, vd_all, km_all, vm_all = k_d, v_d, k_m, v_m
        new_ctx = (k_d, v_d, c_kv, k_r)
    else:
        q_d = rope_2d(q_d)
        k_d = rope_2d(k_d)
        q_m = jnp.concatenate([q_m[..., :MLA_NOPE], rope_2d(q_m[..., MLA_NOPE:])], axis=-1)
        k_m = jnp.concatenate([k_m[..., :MLA_NOPE], rope_2d(k_m[..., MLA_NOPE:])], axis=-1)
        kd_c, vd_c, ckv_c, kr_c = ctx
        km_c, vm_c = mla_keys(ckv_c, kr_c, lp['mla_w_kvb'], lp['mla_knorm_w'])
        kd_all = jnp.concatenate([kd_c, k_d], axis=1)
        vd_all = jnp.concatenate([vd_c, v_d], axis=1)
        km_all = jnp.concatenate([km_c, k_m], axis=1)
        vm_all = jnp.concatenate([vm_c, v_m], axis=1)
        new_ctx = None
    lam = (jnp.exp(jnp.sum(lp['diff_lambda_q1'].astype(jnp.float32) * lp['diff_lambda_k1'].astype(jnp.float32)))
           - jnp.exp(jnp.sum(lp['diff_lambda_q2'].astype(jnp.float32) * lp['diff_lambda_k2'].astype(jnp.float32)))
           + lam_init)
    o_d = rmsnorm(diff_attention(q_d, kd_all, vd_all, lam), lp['diff_subln_w']) * (1.0 - lam_init)
    y_d = o_d.reshape(B, L, DIFF_V_W) @ lp['w_br_diff']
    o_m = softmax_attention(q_m, km_all, vm_all, MLA_QK_HD ** -0.5)
    y_m = o_m.reshape(B, L, MLA_W) @ lp['w_br_mla']
    gates = jax.nn.sigmoid(g.astype(jnp.float32)).astype(h.dtype).reshape(B, L, N_BRANCH, D_MODEL)
    merged = gates[:, :, 0] * y_f + gates[:, :, 1] * y_d + gates[:, :, 2] * y_m
    return merged @ lp['w_out'], new_ctx


def expert_choice_ffn(h, w_router, w_gate, w_up, w_down):
    B, N, D = h.shape
    cap = EC_FACTOR * N // N_EXPERTS
    aff = jax.nn.softmax((h @ w_router).astype(jnp.float32), axis=-1)
    gval, idx = lax.top_k(jnp.swapaxes(aff, 1, 2), cap)
    xg = jax.vmap(lambda hb, ib: hb[ib])(h, idx)
    a = jnp.einsum('becd,edf->becf', xg, w_gate)
    u = jnp.einsum('becd,edf->becf', xg, w_up)
    y = jnp.einsum('becf,efd->becd', jax.nn.silu(a) * u, w_down) * gval[..., None].astype(h.dtype)
    return jax.vmap(lambda yb, ib: jnp.zeros((N, D), yb.dtype).at[ib.reshape(-1)].add(yb.reshape(-1, D)))(y, idx)


def trunk_layer(x, cond, lp, lam_init, ctx):
    mods = jnp.split(jax.nn.silu(cond) @ lp['w_ada'] + lp['b_ada'], N_ADA, axis=-1)
    sh1, sc1, g1, sh2, sc2, g2 = [m[:, None, :] for m in mods]
    h = rmsnorm(x, lp['norm1_w']) * (1.0 + sc1) + sh1
    y, new_ctx = mixer(h, lp, lam_init, ctx)
    x = x + g1 * y
    h = rmsnorm(x, lp['norm2_w']) * (1.0 + sc2) + sh2
    x = x + g2 * expert_choice_ffn(h, lp['moe_w_router'], lp['moe_w_gate'], lp['moe_w_up'], lp['moe_w_down'])
    return x, new_ctx


def setup_inputs(seed: int = 0) -> dict:
    key = jax.random.key(seed)
    ks = iter(jax.random.split(key, 40))
    def nrm(shape, scale=1.0):
        return jax.random.normal(next(ks), shape, jnp.float32) * scale
    def gain(shape):
        return 1.0 + 0.02 * nrm(shape)
    D, L = D_MODEL, DEPTH
    return {
        'x_prompt': nrm((BATCH, SEQ, D)),
        'x_sample': nrm((DEC_BATCH, DEC_SEQ, D)),
        'cache_diff_k': nrm((DEC_BATCH, L, PAST_LEN, DIFF_HEADS, 2, DIFF_HD)),
        'cache_diff_v': nrm((DEC_BATCH, L, PAST_LEN, DIFF_HEADS, 2 * DIFF_HD)),
        'cache_mla_ckv': nrm((DEC_BATCH, L, PAST_LEN, MLA_KV_LORA)),
        'cache_mla_krope': nrm((DEC_BATCH, L, PAST_LEN, MLA_ROPE)),
        'c': nrm((DEC_BATCH, D)),
        'c_ctx': nrm((D,)),
        'w_ada': nrm((L, D, N_ADA * D), 0.5 * D ** -0.5),
        'b_ada': nrm((L, N_ADA * D), 0.1),
        'norm1_w': gain((L, D)),
        'norm2_w': gain((L, D)),
        'w_in': nrm((L, D, IN_COLS), D ** -0.5),
        'diff_qnorm_w': gain((L, DIFF_HD)),
        'diff_knorm_w': gain((L, DIFF_HD)),
        'diff_lambda_q1': nrm((L, DIFF_HD), 0.1),
        'diff_lambda_k1': nrm((L, DIFF_HD), 0.1),
        'diff_lambda_q2': nrm((L, DIFF_HD), 0.1),
        'diff_lambda_k2': nrm((L, DIFF_HD), 0.1),
        'diff_subln_w': gain((L, 2 * DIFF_HD)),
        'mla_qa_norm_w': gain((L, MLA_Q_LORA)),
        'mla_w_qb': nrm((L, MLA_Q_LORA, MLA_HEADS * MLA_QK_HD), MLA_Q_LORA ** -0.5),
        'mla_kva_norm_w': gain((L, MLA_KV_LORA)),
        'mla_w_kvb': nrm((L, MLA_KV_LORA, MLA_HEADS * (MLA_NOPE + MLA_V)), MLA_KV_LORA ** -0.5),
        'mla_qnorm_w': gain((L, MLA_QK_HD)),
        'mla_knorm_w': gain((L, MLA_QK_HD)),
        'w_br_fourier': nrm((L, FOURIER_W, D), FOURIER_W ** -0.5),
        'w_br_diff': nrm((L, DIFF_V_W, D), DIFF_V_W ** -0.5),
        'w_br_mla': nrm((L, MLA_W, D), MLA_W ** -0.5),
        'w_out': nrm((L, D, D), D ** -0.5),
        'moe_w_router': nrm((L, D, N_EXPERTS), D ** -0.5),
        'moe_w_gate': nrm((L, N_EXPERTS, D, EXPERT_FF), D ** -0.5),
        'moe_w_up': nrm((L, N_EXPERTS, D, EXPERT_FF), D ** -0.5),
        'moe_w_down': nrm((L, N_EXPERTS, EXPERT_FF, D), EXPERT_FF ** -0.5),
    }


def reference(x_prompt, x_sample, cache_diff_k, cache_diff_v, cache_mla_ckv, cache_mla_krope, c, c_ctx,
              w_ada, b_ada, norm1_w, norm2_w, w_in, diff_qnorm_w, diff_knorm_w,
              diff_lambda_q1, diff_lambda_k1, diff_lambda_q2, diff_lambda_k2, diff_subln_w,
              mla_qa_norm_w, mla_w_qb, mla_kva_norm_w, mla_w_kvb, mla_qnorm_w, mla_knorm_w,
              w_br_fourier, w_br_diff, w_br_mla, w_out,
              moe_w_router, moe_w_gate, moe_w_up, moe_w_down):
    y_prompt = x_prompt
    y_sample = x_sample
    dk, dv, ckv, krp = [], [], [], []
    for l in range(DEPTH):
        lp = dict(w_ada=w_ada[l], b_ada=b_ada[l], norm1_w=norm1_w[l], norm2_w=norm2_w[l], w_in=w_in[l],
                  diff_qnorm_w=diff_qnorm_w[l], diff_knorm_w=diff_knorm_w[l],
                  diff_lambda_q1=diff_lambda_q1[l], diff_lambda_k1=diff_lambda_k1[l],
                  diff_lambda_q2=diff_lambda_q2[l], diff_lambda_k2=diff_lambda_k2[l],
                  diff_subln_w=diff_subln_w[l], mla_qa_norm_w=mla_qa_norm_w[l], mla_w_qb=mla_w_qb[l],
                  mla_kva_norm_w=mla_kva_norm_w[l], mla_w_kvb=mla_w_kvb[l],
                  mla_qnorm_w=mla_qnorm_w[l], mla_knorm_w=mla_knorm_w[l],
                  w_br_fourier=w_br_fourier[l], w_br_diff=w_br_diff[l], w_br_mla=w_br_mla[l], w_out=w_out[l],
                  moe_w_router=moe_w_router[l], moe_w_gate=moe_w_gate[l],
                  moe_w_up=moe_w_up[l], moe_w_down=moe_w_down[l])
        lam_init = 0.8 - 0.6 * math.exp(-0.3 * l)
        y_prompt, (k_c, v_c, ckv_c, kr_c) = trunk_layer(y_prompt, c_ctx[None, :], lp, lam_init, None)
        dk.append(k_c)
        dv.append(v_c)
        ckv.append(ckv_c)
        krp.append(kr_c)
        ctx = (cache_diff_k[:, l], cache_diff_v[:, l], cache_mla_ckv[:, l], cache_mla_krope[:, l])
        y_sample, _ = trunk_layer(y_sample, c, lp, lam_init, ctx)
    new_diff_k = jnp.stack(dk, axis=1)
    new_diff_v = jnp.stack(dv, axis=1)
    new_mla_ckv = jnp.stack(ckv, axis=1)
    new_mla_krope = jnp.stack(krp, axis=1)
    return (y_prompt, y_sample, new_diff_k, new_diff_v, new_mla_ckv, new_mla_krope)
```

```python
import functools
import math

import jax
import jax.numpy as jnp
import numpy as np
from jax import lax
from jax.experimental import pallas as pl
from jax.experimental.pallas import tpu as pltpu

F32, BF16, I32, U32 = jnp.float32, jnp.bfloat16, jnp.int32, jnp.uint32

GRID_W = 64
ROPE_BASE = 10000.0
EPS = 1e-6
N_ADA = 6
FOURIER_GROUPS = 4
FOURIER_CH = 128
FOURIER_W = FOURIER_GROUPS * FOURIER_CH
DIFF_HEADS = 4
DIFF_HD = 64
DIFF_W = DIFF_HEADS * 2 * DIFF_HD
MLA_HEADS = 8
MLA_NOPE = 64
MLA_ROPE = 32
MLA_QK_HD = MLA_NOPE + MLA_ROPE
MLA_V = 64
MLA_Q_LORA = 384
MLA_KV_LORA = 256
N_EXPERTS = 16
EC_FACTOR = 2

LANES = 128
SUBLANES = 8
VMEM_LIMIT_BYTES = 56 * 1024 * 1024
MLA_HEAD_PAD = LANES
MLA_PAD_W = MLA_HEADS * MLA_HEAD_PAD
COND_ROWS = 16


def _cparams(n_axes, **kw):
    return pltpu.CompilerParams(dimension_semantics=("arbitrary",) * n_axes,
                                vmem_limit_bytes=VMEM_LIMIT_BYTES, **kw)


def _dot(a, b):
    return jnp.dot(a, b, preferred_element_type=F32)


def _dot_nt(a, b):
    return lax.dot_general(a, b, (((1,), (1,)), ((), ())), preferred_element_type=F32)


def _pick_tile(n, target):
    t = min(n, target)
    while n % t:
        t //= 2
    return t


def _adaln_body(c_ref, w_ref, b_ref, o_ref):
    c = c_ref[...]
    a = (c * jax.nn.sigmoid(c)).astype(BF16)
    o_ref[...] = _dot(a, w_ref[...].astype(BF16)) + b_ref[...]


def adaln(cond, w_ada, b_ada):
    depth, d, n = w_ada.shape
    tn = _pick_tile(n, 1024)
    return pl.pallas_call(
        _adaln_body,
        grid=(depth, n // tn),
        in_specs=[pl.BlockSpec((COND_ROWS, d), lambda l, j: (0, 0)),
                  pl.BlockSpec((None, d, tn), lambda l, j: (l, 0, j)),
                  pl.BlockSpec((None, 1, tn), lambda l, j: (l, 0, j))],
        out_specs=pl.BlockSpec((None, COND_ROWS, tn), lambda l, j: (l, 0, j)),
        out_shape=jax.ShapeDtypeStruct((depth, COND_ROWS, n), F32),
        compiler_params=_cparams(2),
    )(cond, w_ada, b_ada.reshape(depth, 1, n))


def _norm_mod(x, nw, sc, sh):
    r = lax.rsqrt(jnp.mean(x * x, axis=-1, keepdims=True) + EPS)
    return (x * r) * nw * (1.0 + sc) + sh


def _norm_mod_body(x_ref, nw_ref, sc_ref, sh_ref, h_ref):
    h_ref[...] = _norm_mod(x_ref[...], nw_ref[...], sc_ref[...], sh_ref[...]).astype(BF16)


def _mod_index(nb, tiles_per_batch):
    if nb == 1:
        return lambda i, *_: (0, 0, 0)
    return lambda i, *_: (i // tiles_per_batch, 0, 0)


def norm_mod(x, nw, sc, sh, seq):
    t, d = x.shape
    tm = _pick_tile(seq, 512)
    mod_spec = pl.BlockSpec((None, 1, d), _mod_index(sc.shape[0], seq // tm))
    return pl.pallas_call(
        _norm_mod_body,
        grid=(t // tm,),
        in_specs=[pl.BlockSpec((tm, d), lambda i: (i, 0)),
                  pl.BlockSpec((1, d), lambda i: (0, 0)),
                  mod_spec, mod_spec],
        out_specs=pl.BlockSpec((tm, d), lambda i: (i, 0)),
        out_shape=jax.ShapeDtypeStruct((t, d), BF16),
        compiler_params=_cparams(1),
    )(x, nw, sc, sh)


def _gates_body(h_ref, w_ref, o_ref):
    o_ref[...] = jax.nn.sigmoid(_dot(h_ref[...], w_ref[...])).astype(BF16)


def gates_proj(h, w):
    t, d = h.shape
    n = w.shape[1]
    tm, tn = _pick_tile(t, 1024), _pick_tile(n, 1024)
    return pl.pallas_call(
        _gates_body,
        grid=(t // tm, n // tn),
        in_specs=[pl.BlockSpec((tm, d), lambda i, j: (i, 0)),
                  pl.BlockSpec((d, tn), lambda i, j: (0, j))],
        out_specs=pl.BlockSpec((tm, tn), lambda i, j: (i, j)),
        out_shape=jax.ShapeDtypeStruct((t, n), BF16),
        compiler_params=_cparams(2),
    )(h, w)


def _fourier_ch_body(h_ref, w_ref, cs_ref, o_ref):
    u = _dot(h_ref[...], w_ref[...]).astype(BF16)
    for g in range(FOURIER_GROUPS):
        ab = _dot(u[:, g * FOURIER_CH:(g + 1) * FOURIER_CH], cs_ref[...])
        o_ref[:, g * FOURIER_CH:(g + 1) * FOURIER_CH] = ab[:, :FOURIER_CH].astype(BF16)
        o_ref[:, FOURIER_W + g * FOURIER_CH:FOURIER_W + (g + 1) * FOURIER_CH] = ab[:, FOURIER_CH:].astype(BF16)


def fourier_channel(h, w, cs):
    t, d = h.shape
    tm = _pick_tile(t, 1024)
    return pl.pallas_call(
        _fourier_ch_body,
        grid=(t // tm,),
        in_specs=[pl.BlockSpec((tm, d), lambda i: (i, 0)),
                  pl.BlockSpec((d, FOURIER_W), lambda i: (0, 0)),
                  pl.BlockSpec((FOURIER_CH, 2 * FOURIER_CH), lambda i: (0, 0))],
        out_specs=pl.BlockSpec((tm, 2 * FOURIER_W), lambda i: (i, 0)),
        out_shape=jax.ShapeDtypeStruct((t, 2 * FOURIER_W), BF16),
        compiler_params=_cparams(1),
    )(h, w, cs)


def _fourier_pos_body(c_ref, s_ref, ab_ref, o_ref):
    y = _dot(c_ref[...], ab_ref[:, :FOURIER_W]) - _dot(s_ref[...], ab_ref[:, FOURIER_W:])
    o_ref[...] = y.astype(BF16)


def fourier_position(ab, cpos, spos, seq):
    t = ab.shape[0]
    tr = _pick_tile(seq, 512)
    nr = seq // tr
    return pl.pallas_call(
        _fourier_pos_body,
        grid=(t // seq, nr),
        in_specs=[pl.BlockSpec((tr, seq), lambda b, r: (r, 0)),
                  pl.BlockSpec((tr, seq), lambda b, r: (r, 0)),
                  pl.BlockSpec((seq, 2 * FOURIER_W), lambda b, r: (b, 0))],
        out_specs=pl.BlockSpec((tr, FOURIER_W), lambda b, r: (b * nr + r, 0)),
        out_shape=jax.ShapeDtypeStruct((t, FOURIER_W), BF16),
        compiler_params=_cparams(2),
    )(cpos, spos, ab)


def _rope(x, cos, sin, off):
    w = x.shape[1]
    lane = lax.broadcasted_iota(I32, (1, w), 1)
    first = (lane & off) == 0
    partner = jnp.where(first, pltpu.roll(x, w - off, 1), pltpu.roll(x, off, 1))
    return x * cos + partner * sin


def _half_tile_norm(x):
    outs = []
    lane = lax.broadcasted_iota(I32, (1, LANES), 1)
    low = lane < DIFF_HD
    for j in range(x.shape[1] // LANES):
        seg = x[:, j * LANES:(j + 1) * LANES]
        sq = seg * seg
        s_lo = jnp.sum(jnp.where(low, sq, 0.0), axis=-1, keepdims=True)
        s_hi = jnp.sum(jnp.where(low, 0.0, sq), axis=-1, keepdims=True)
        ms = jnp.where(low, s_lo, s_hi) * (1.0 / DIFF_HD)
        outs.append(seg * lax.rsqrt(ms + EPS))
    return jnp.concatenate(outs, axis=1)


def _tile_norm(x, width):
    outs = []
    for j in range(x.shape[1] // LANES):
        seg = x[:, j * LANES:(j + 1) * LANES]
        ms = jnp.sum(seg * seg, axis=-1, keepdims=True) * (1.0 / width)
        outs.append(seg * lax.rsqrt(ms + EPS))
    return jnp.concatenate(outs, axis=1)


def _diff_qkv_body(*refs, rope):
    if rope:
        h_ref, w_ref, qn_ref, kn_ref, cos_ref, sin_ref, q_out, k_out, v_out = refs
    else:
        h_ref, w_ref, qn_ref, kn_ref, q_out, k_out, v_out, k32_out, v32_out = refs
    z = _dot(h_ref[...], w_ref[...])
    q = _half_tile_norm(z[:, :DIFF_W]) * qn_ref[...]
    k = _half_tile_norm(z[:, DIFF_W:2 * DIFF_W]) * kn_ref[...]
    v = z[:, 2 * DIFF_W:]
    if rope:
        q = _rope(q, cos_ref[...], sin_ref[...], DIFF_HD // 4)
        k = _rope(k, cos_ref[...], sin_ref[...], DIFF_HD // 4)
    else:
        k32_out[...] = k
        v32_out[...] = v
    q_out[...] = (q * DIFF_HD ** -0.5).astype(BF16)
    k_out[...] = k.astype(BF16)
    v_out[...] = v.astype(BF16)


def diff_qkv(h, w, qn, kn, seq, rope_tabs):
    t, d = h.shape
    rope = rope_tabs is not None
    tm = _pick_tile(seq, 512)
    row = lambda i: (i, 0)
    const = lambda i: (0, 0)
    in_specs = [pl.BlockSpec((tm, d), row), pl.BlockSpec((d, 3 * DIFF_W), const),
                pl.BlockSpec((1, DIFF_W), const), pl.BlockSpec((1, DIFF_W), const)]
    args = [h, w, qn, kn]
    out_shape = [jax.ShapeDtypeStruct((t, DIFF_W), BF16)] * 3
    if rope:
        nt = seq // tm
        tab = pl.BlockSpec((tm, DIFF_W), lambda i: (i % nt, 0))
        in_specs += [tab, tab]
        args += list(rope_tabs)
    else:
        out_shape = out_shape + [jax.ShapeDtypeStruct((t, DIFF_W), F32)] * 2
    return pl.pallas_call(
        functools.partial(_diff_qkv_body, rope=rope),
        grid=(t // tm,),
        in_specs=in_specs,
        out_specs=[pl.BlockSpec((tm, DIFF_W), row)] * len(out_shape),
        out_shape=out_shape,
        compiler_params=_cparams(1),
    )(*args)


def _mla_keys(ckv_bf, krb, wk_ref, wv_ref, kn_ref):
    k = _dot(ckv_bf, wk_ref[...]) + jnp.concatenate([krb] * MLA_HEADS, axis=1)
    k = _tile_norm(k, MLA_QK_HD) * kn_ref[...]
    v = _dot(ckv_bf, wv_ref[...])
    return k, v


def _mla_proj_body(*refs, rope):
    (h_ref, w_ref, qan_ref, wqb_ref, qn_ref, kvan_ref, wk_ref, wv_ref, kn_ref), refs = refs[:9], refs[9:]
    if rope:
        cos_ref, sin_ref, q_out, k_out, v_out = refs
    else:
        q_out, k_out, v_out, ckv32_out, kr32_out = refs
    z = _dot(h_ref[...], w_ref[...])
    q_a = z[:, :MLA_Q_LORA]
    kv_a = z[:, MLA_Q_LORA:MLA_Q_LORA + MLA_KV_LORA]
    krb = z[:, MLA_Q_LORA + MLA_KV_LORA:]
    c_q = q_a * lax.rsqrt(jnp.mean(q_a * q_a, axis=-1, keepdims=True) + EPS) * qan_ref[...]
    q = _tile_norm(_dot(c_q.astype(BF16), wqb_ref[...]), MLA_QK_HD) * qn_ref[...]
    c_kv = kv_a * lax.rsqrt(jnp.mean(kv_a * kv_a, axis=-1, keepdims=True) + EPS) * kvan_ref[...]
    k, v = _mla_keys(c_kv.astype(BF16), krb, wk_ref, wv_ref, kn_ref)
    if rope:
        q = _rope(q, cos_ref[...], sin_ref[...], MLA_ROPE // 4)
        k = _rope(k, cos_ref[...], sin_ref[...], MLA_ROPE // 4)
    else:
        ckv32_out[...] = c_kv
        kr32_out[...] = krb
    q_out[...] = (q * MLA_QK_HD ** -0.5).astype(BF16)
    k_out[...] = k.astype(BF16)
    v_out[...] = v.astype(BF16)


def mla_proj(h, w, p, seq, rope_tabs):
    t, d = h.shape
    rope = rope_tabs is not None
    tm = _pick_tile(seq, 512)
    row = lambda i: (i, 0)
    const = lambda i: (0, 0)
    wcols = MLA_Q_LORA + MLA_KV_LORA + LANES
    in_specs = [pl.BlockSpec((tm, d), row), pl.BlockSpec((d, wcols), const),
                pl.BlockSpec((1, MLA_Q_LORA), const), pl.BlockSpec((MLA_Q_LORA, MLA_PAD_W), const),
                pl.BlockSpec((1, MLA_PAD_W), const), pl.BlockSpec((1, MLA_KV_LORA), const),
                pl.BlockSpec((MLA_KV_LORA, MLA_PAD_W), const), pl.BlockSpec((MLA_KV_LORA, MLA_PAD_W), const),
                pl.BlockSpec((1, MLA_PAD_W), const)]
    args = [h, w, p["qa_norm"], p["w_qb"], p["q_norm"], p["kva_norm"], p["w_k"], p["w_v"], p["k_norm"]]
    out_shape = [jax.ShapeDtypeStruct((t, MLA_PAD_W), BF16)] * 3
    out_specs = [pl.BlockSpec((tm, MLA_PAD_W), row)] * 3
    if rope:
        nt = seq // tm
        tab = pl.BlockSpec((tm, MLA_PAD_W), lambda i: (i % nt, 0))
        in_specs += [tab, tab]
        args += list(rope_tabs)
    else:
        out_shape += [jax.ShapeDtypeStruct((t, MLA_KV_LORA), F32), jax.ShapeDtypeStruct((t, LANES), F32)]
        out_specs += [pl.BlockSpec((tm, MLA_KV_LORA), row), pl.BlockSpec((tm, LANES), row)]
    return pl.pallas_call(
        functools.partial(_mla_proj_body, rope=rope),
        grid=(t // tm,),
        in_specs=in_specs,
        out_specs=out_specs,
        out_shape=out_shape,
        compiler_params=_cparams(1),
    )(*args)


def _mla_ctx_body(ckv_ref, krb_ref, wk_ref, wv_ref, kn_ref, k_out, v_out):
    k, v = _mla_keys(ckv_ref[...].astype(BF16), krb_ref[...], wk_ref, wv_ref, kn_ref)
    k_out[...] = k.astype(BF16)
    v_out[...] = v.astype(BF16)


def mla_ctx_keys(ckv, krb, p):
    t = ckv.shape[0]
    tm = _pick_tile(t, 512)
    row = lambda i: (i, 0)
    const = lambda i: (0, 0)
    return pl.pallas_call(
        _mla_ctx_body,
        grid=(t // tm,),
        in_specs=[pl.BlockSpec((tm, MLA_KV_LORA), row), pl.BlockSpec((tm, LANES), row),
                  pl.BlockSpec((MLA_KV_LORA, MLA_PAD_W), const), pl.BlockSpec((MLA_KV_LORA, MLA_PAD_W), const),
                  pl.BlockSpec((1, MLA_PAD_W), const)],
        out_specs=[pl.BlockSpec((tm, MLA_PAD_W), row)] * 2,
        out_shape=[jax.ShapeDtypeStruct((t, MLA_PAD_W), BF16)] * 2,
        compiler_params=_cparams(1),
    )(ckv, krb, p["w_k"], p["w_v"], p["k_norm"])


def _attn_body(*refs, nmaps, has_ctx, subln, tk, lam_init):
    refs = list(refs)
    lamv_ref = refs.pop(0) if nmaps == 2 else None
    q_ref, k_ref, v_ref = refs[:3]
    refs = refs[3:]
    segments = []
    if has_ctx:
        segments.append((refs[0], refs[1]))
        refs = refs[2:]
    segments.append((k_ref, v_ref))
    subln_ref = refs.pop(0) if subln else None
    o_ref = refs[0]

    q = q_ref[...]
    tq = q.shape[0]
    if nmaps == 2:
        lane = lax.broadcasted_iota(I32, (1, LANES), 1)
        zero = jnp.zeros_like(q)
        qs = [jnp.where(lane < DIFF_HD, q, zero), jnp.where(lane < DIFF_HD, zero, q)]
    else:
        qs = [q]
    m = [jnp.full((tq, 1), -jnp.inf, F32) for _ in qs]
    l = [jnp.zeros((tq, 1), F32) for _ in qs]
    acc = [jnp.zeros((tq, LANES), F32) for _ in qs]
    for kr, vr in segments:
        n = kr.shape[0]
        for c0 in range(0, n, tk):
            c1 = min(n, c0 + tk)
            kc = kr[c0:c1, :]
            vc = vr[c0:c1, :]
            for i, qi in enumerate(qs):
                s = _dot_nt(qi, kc)
                mn = jnp.maximum(m[i], jnp.max(s, axis=-1, keepdims=True))
                alpha = jnp.exp(m[i] - mn)
                p = jnp.exp(s - mn)
                l[i] = alpha * l[i] + jnp.sum(p, axis=-1, keepdims=True)
                acc[i] = alpha * acc[i] + _dot(p.astype(BF16), vc)
                m[i] = mn
    o = acc[0] / l[0]
    if nmaps == 2:
        lv = lamv_ref[...]
        lam = (jnp.exp(jnp.sum(lv[0:1] * lv[1:2], axis=-1, keepdims=True))
               - jnp.exp(jnp.sum(lv[2:3] * lv[3:4], axis=-1, keepdims=True)) + lam_init)
        o = o - lam * (acc[1] / l[1])
    if subln:
        o = o * lax.rsqrt(jnp.mean(o * o, axis=-1, keepdims=True) + EPS) * subln_ref[...] * (1.0 - lam_init)
    o_ref[...] = o.astype(BF16)


def attention(q, k, v, ctx, *, nbatch, seq, heads, nmaps, lamv=None, subln=None, lam_init=0.0):
    t, w = q.shape
    tq = _pick_tile(seq, 512)
    nq = seq // tq
    qspec = pl.BlockSpec((tq, LANES), lambda b, h, i: (b * nq + i, h))
    kvspec = pl.BlockSpec((seq, LANES), lambda b, h, i: (b, h))
    in_specs, args = [], []
    if nmaps == 2:
        in_specs.append(pl.BlockSpec((SUBLANES, LANES), lambda b, h, i: (0, 0)))
        args.append(lamv)
    in_specs += [qspec, kvspec, kvspec]
    args += [q, k, v]
    if ctx is not None:
        past = ctx[0].shape[0] // nbatch
        cspec = pl.BlockSpec((past, LANES), lambda b, h, i: (b, h))
        in_specs += [cspec, cspec]
        args += list(ctx)
    if subln is not None:
        in_specs.append(pl.BlockSpec((1, LANES), lambda b, h, i: (0, 0)))
        args.append(subln)
    return pl.pallas_call(
        functools.partial(_attn_body, nmaps=nmaps, has_ctx=ctx is not None, subln=subln is not None,
                          tk=512, lam_init=lam_init),
        grid=(nbatch, heads, nq),
        in_specs=in_specs,
        out_specs=qspec,
        out_shape=jax.ShapeDtypeStruct((t, w), BF16),
        compiler_params=_cparams(3),
    )(*args)


def _merge_body(x_ref, f_ref, od_ref, om_ref, g0_ref, g1_ref, g2_ref, wf_ref, wd_ref, wm_ref, wo_ref,
                gate1_ref, nw_ref, sc_ref, sh_ref, wrh_ref, wrl_ref,
                x1_ref, h2p_ref, aff_ref):
    merged = g0_ref[...].astype(F32) * _dot(f_ref[...], wf_ref[...])
    merged = merged + g1_ref[...].astype(F32) * _dot(od_ref[...], wd_ref[...])
    merged = merged + g2_ref[...].astype(F32) * _dot(om_ref[...], wm_ref[...])
    x1 = x_ref[...] + gate1_ref[...] * _dot(merged.astype(BF16), wo_ref[...])
    x1_ref[...] = x1
    h2 = _norm_mod(x1, nw_ref[...], sc_ref[...], sh_ref[...])
    tm, d = h2.shape
    h_hi = h2.astype(jnp.bfloat16)
    hi32 = h_hi.astype(F32)
    bits = pltpu.bitcast(hi32, U32)
    packed = (bits[:, :d // 2] >> 16) | bits[:, d // 2:]
    nchunk = d // 2 // LANES
    for c in range(nchunk):
        h2p_ref[pl.ds(c, tm, stride=nchunk), :] = packed[:, c * LANES:(c + 1) * LANES]
    h_lo = (h2 - hi32).astype(BF16)
    logits = _dot(h_hi, wrh_ref[...]) + _dot(h_lo, wrh_ref[...]) + _dot(h_hi, wrl_ref[...])
    lane = lax.broadcasted_iota(I32, (1, LANES), 1)
    logits = jnp.where(lane < N_EXPERTS, logits, -1e30)
    e = jnp.exp(logits - jnp.max(logits, axis=-1, keepdims=True))
    aff = e / jnp.sum(e, axis=-1, keepdims=True)
    aff_ref[...] = aff.T[:N_EXPERTS, :]


def merge_out(x, f, od, om, gates, w, gate1, nw, sc, sh, seq):
    t, d = x.shape
    tm = _pick_tile(seq, 256)
    nchunk = d // 2 // LANES
    row = lambda i: (i, 0)
    const = lambda i: (0, 0)
    once = pl.Buffered(1)
    mod_spec = pl.BlockSpec((None, 1, d), _mod_index(sc.shape[0], seq // tm))
    wspec = lambda a: pl.BlockSpec(a.shape, const, pipeline_mode=once)
    in_specs = [pl.BlockSpec((tm, d), row),
                pl.BlockSpec((tm, f.shape[1]), row), pl.BlockSpec((tm, od.shape[1]), row),
                pl.BlockSpec((tm, om.shape[1]), row),
                pl.BlockSpec((tm, d), lambda i: (i, 0)), pl.BlockSpec((tm, d), lambda i: (i, 1)),
                pl.BlockSpec((tm, d), lambda i: (i, 2)),
                wspec(w["br_f"]), wspec(w["br_d"]), wspec(w["br_m"]), wspec(w["out"]),
                mod_spec, pl.BlockSpec((1, d), const), mod_spec, mod_spec,
                wspec(w["router_hi"]), wspec(w["router_lo"])]
    return pl.pallas_call(
        _merge_body,
        grid=(t // tm,),
        in_specs=in_specs,
        out_specs=[pl.BlockSpec((tm, d), row),
                   pl.BlockSpec((tm * nchunk, LANES), row),
                   pl.BlockSpec((N_EXPERTS, tm), lambda i: (0, i))],
        out_shape=[jax.ShapeDtypeStruct((t, d), F32),
                   jax.ShapeDtypeStruct((t * nchunk, LANES), U32),
                   jax.ShapeDtypeStruct((N_EXPERTS, t), F32)],
        compiler_params=_cparams(1),
    )(x, f, od, om, gates, gates, gates, w["br_f"], w["br_d"], w["br_m"], w["out"],
      gate1, nw, sc, sh, w["router_hi"], w["router_lo"])


def _cumsum_lanes(x):
    n = x.shape[1]
    lane = lax.broadcasted_iota(I32, (1, n), 1)
    s = 1
    while s < n:
        x = x + jnp.where(lane >= s, pltpu.roll(x, s, 1), 0)
        s *= 2
    return x


def _route_body(aff_ref, idx_ref, dst_ref, gv_ref, seg_ref, key_scr, dst_scr, *, cap, row_chunk):
    b = pl.program_id(0)
    aff = aff_ref[...]
    ne, n = aff.shape
    bits = pltpu.bitcast(aff, I32)

    def search(i, thr):
        cand = thr | (1 << (30 - i))
        cnt = jnp.sum(jnp.where(bits >= cand, 1.0, 0.0), axis=-1, keepdims=True)
        return jnp.where(cnt >= cap, cand, thr)

    thr = lax.fori_loop(0, 31, search, jnp.zeros((ne, 1), I32))
    gt = bits > thr
    eq = (bits == thr).astype(I32)
    need = cap - jnp.sum(jnp.where(gt, 1.0, 0.0), axis=-1, keepdims=True).astype(I32)
    eq_before = _cumsum_lanes(eq) - eq
    sel = jnp.where(gt | ((eq > 0) & (eq_before < need)), 1, 0)
    cum = _cumsum_lanes(sel)
    key_scr[...] = sel * cum

    before = jnp.zeros((1, n), I32)
    for e in range(ne):
        dst_scr[e:e + 1, :] = before
        before = before + sel[e:e + 1, :]
    k_tok = before
    start = _cumsum_lanes(k_tok) - k_tok + b * (ne * cap)
    dst_scr[...] = dst_scr[...] + start
    seg_ref[0:1, :] = start
    seg_ref[1:2, :] = start + k_tok
    seg_ref[2:SUBLANES, :] = jnp.zeros((SUBLANES - 2, n), I32)

    tok = (lax.broadcasted_iota(I32, (1, n), 1) + b * n).astype(F32)
    lane = lax.broadcasted_iota(I32, (1, LANES), 1)
    idx_ref[...] = jnp.zeros(idx_ref.shape, I32)
    dst_ref[...] = jnp.zeros(dst_ref.shape, I32)
    gv_ref[...] = jnp.zeros(gv_ref.shape, F32)

    def per_expert(e, carry):
        key = key_scr[pl.ds(e, 1), :]
        dst = dst_scr[pl.ds(e, 1), :].astype(F32)
        af = aff_ref[pl.ds(e, 1), :]
        for c0 in range(0, cap, row_chunk):
            slot = lax.broadcasted_iota(I32, (row_chunk, 1), 0) + (c0 + 1)
            hit = key == slot
            i_col = jnp.sum(jnp.where(hit, tok, 0.0), axis=-1, keepdims=True).astype(I32)
            d_col = jnp.sum(jnp.where(hit, dst, 0.0), axis=-1, keepdims=True).astype(I32)
            g_col = jnp.sum(jnp.where(hit, af, 0.0), axis=-1, keepdims=True)
            rows = pl.ds(c0, row_chunk)
            idx_ref[rows, :] = jnp.where(lane == e, i_col, idx_ref[rows, :])
            dst_ref[rows, :] = jnp.where(lane == e, d_col, dst_ref[rows, :])
            gv_ref[rows, :] = jnp.where(lane == e, g_col, gv_ref[rows, :])
        return carry

    lax.fori_loop(0, ne, per_expert, 0)


def route(aff_t, nbatch, seq):
    ne = aff_t.shape[0]
    cap = EC_FACTOR * seq // ne
    row_chunk = min(cap, 32)
    tab = lambda dt: jax.ShapeDtypeStruct((nbatch, cap, LANES), dt)
    tspec = pl.BlockSpec((None, cap, LANES), lambda b: (b, 0, 0))
    return pl.pallas_call(
        functools.partial(_route_body, cap=cap, row_chunk=row_chunk),
        grid=(nbatch,),
        in_specs=[pl.BlockSpec((ne, seq), lambda b: (0, b))],
        out_specs=[tspec, tspec, tspec, pl.BlockSpec((None, SUBLANES, seq), lambda b: (b, 0, 0))],
        out_shape=[tab(I32), tab(I32), tab(F32), jax.ShapeDtypeStruct((nbatch, SUBLANES, seq), I32)],
        scratch_shapes=[pltpu.VMEM((ne, seq), I32), pltpu.VMEM((ne, seq), I32)],
        compiler_params=_cparams(1),
    )(aff_t)


def _unpack_rows(xbuf, rt, nchunk):
    lo, hi = [], []
    for c in range(nchunk):
        wds = xbuf[pl.ds(c, rt, stride=nchunk), :]
        lo.append(pltpu.bitcast(wds << 16, F32).astype(BF16))
        hi.append(pltpu.bitcast(wds & jnp.uint32(0xFFFF0000), F32).astype(BF16))
    return jnp.concatenate(lo + hi, axis=1)


def _moe_body(idx_ref, dst_ref, gv_ref, h2p_ref, wg_ref, wu_ref, wd_ref, g_ref,
              xbuf, ybuf, sem_in, sem_out, *, rt, d):
    nin = d // 2 // LANES
    nout = d // LANES

    def gather(r, carry):
        src = pl.multiple_of(idx_ref[0, r] * nin, nin)
        pltpu.make_async_copy(h2p_ref.at[pl.ds(src, nin), :],
                              xbuf.at[pl.ds(pl.multiple_of(r * nin, nin), nin), :], sem_in).start()
        return carry

    lax.fori_loop(0, rt, gather, 0)

    def gather_wait(r, carry):
        pltpu.make_async_copy(h2p_ref.at[pl.ds(0, nin), :], xbuf.at[pl.ds(0, nin), :], sem_in).wait()
        return carry

    lax.fori_loop(0, rt, gather_wait, 0)

    x = _unpack_rows(xbuf, rt, nin)
    a = _dot(x, wg_ref[...])
    u = _dot(x, wu_ref[...])
    mid = (a * jax.nn.sigmoid(a) * u).astype(BF16)
    y = _dot(mid, wd_ref[...]) * gv_ref[...]
    for c in range(nout):
        ybuf[pl.ds(c, rt, stride=nout), :] = y[:, c * LANES:(c + 1) * LANES]

    def scatter(r, carry):
        dst = pl.multiple_of(dst_ref[0, r] * nout, nout)
        pltpu.make_async_copy(ybuf.at[pl.ds(pl.multiple_of(r * nout, nout), nout), :],
                              g_ref.at[pl.ds(dst, nout), :], sem_out).start()
        return carry

    lax.fori_loop(0, rt, scatter, 0)

    def scatter_wait(r, carry):
        pltpu.make_async_copy(ybuf.at[pl.ds(0, nout), :], g_ref.at[pl.ds(0, nout), :], sem_out).wait()
        return carry

    lax.fori_loop(0, rt, scatter_wait, 0)


def moe_experts(idx, dst, gv, h2p, wg, wu, wd, d):
    ne, _, ff = wg.shape
    rt = idx.shape[2]
    tiles = idx.shape[0] // ne
    npairs = ne * tiles * rt
    nout = d // LANES
    smem = lambda: pl.BlockSpec((None, 1, rt), lambda e, j: (e * tiles + j, 0, 0), memory_space=pltpu.SMEM)
    return pl.pallas_call(
        functools.partial(_moe_body, rt=rt, d=d),
        grid=(ne, tiles),
        in_specs=[smem(), smem(),
                  pl.BlockSpec((rt, 1), lambda e, j: (e * tiles + j, 0)),
                  pl.BlockSpec(memory_space=pl.ANY),
                  pl.BlockSpec((None, d, ff), lambda e, j: (e, 0, 0)),
                  pl.BlockSpec((None, d, ff), lambda e, j: (e, 0, 0)),
                  pl.BlockSpec((None, ff, d), lambda e, j: (e, 0, 0))],
        out_specs=pl.BlockSpec(memory_space=pl.ANY),
        out_shape=jax.ShapeDtypeStruct((npairs * nout, LANES), F32),
        scratch_shapes=[pltpu.VMEM((rt * d // 2 // LANES, LANES), U32),
                        pltpu.VMEM((rt * nout, LANES), F32),
                        pltpu.SemaphoreType.DMA, pltpu.SemaphoreType.DMA],
        compiler_params=_cparams(2, has_side_effects=True),
    )(idx, dst, gv, h2p, wg, wu, wd)


def _combine_body(tb_ref, x_ref, s0_ref, s1_ref, gate_ref, g_ref, o_ref, gbuf, sem, *, win, d, npairs):
    t = pl.program_id(0)
    tm = x_ref.shape[0]
    nout = d // LANES
    first = tb_ref[t]
    last = tb_ref[t + 1]
    nwin = (last - first + win - 1) // win
    s0 = s0_ref[...]
    s1 = s1_ref[...]

    def window(wi, acc):
        w0 = jnp.minimum(first + wi * win, npairs - win)
        cp = pltpu.make_async_copy(g_ref.at[pl.ds(pl.multiple_of(w0 * nout, nout), win * nout), :], gbuf, sem)
        cp.start()
        cp.wait()
        pos = lax.broadcasted_iota(I32, (1, win), 1) + w0
        lo = first + wi * win
        own = (pos >= s0) & (pos < s1) & (pos >= lo)
        sel = jnp.where(own, 1.0, 0.0).astype(BF16)
        rows = jnp.concatenate([gbuf[pl.ds(c, win, stride=nout), :] for c in range(nout)], axis=1)
        r_hi = rows.astype(BF16)
        r_lo = (rows - r_hi.astype(F32)).astype(BF16)
        return acc + _dot(sel, r_hi) + _dot(sel, r_lo)

    acc = lax.fori_loop(0, nwin, window, jnp.zeros((tm, d), F32))
    o_ref[...] = x_ref[...] + gate_ref[...] * acc


def combine(x1, seg0, seg1, tile_bounds, gate2, g, seq):
    t, d = x1.shape
    tm = _pick_tile(seq, 256)
    nout = d // LANES
    npairs = g.shape[0] // nout
    win = min(256, npairs)
    row = lambda i, tb: (i, 0)
    grid_spec = pltpu.PrefetchScalarGridSpec(
        num_scalar_prefetch=1,
        grid=(t // tm,),
        in_specs=[pl.BlockSpec((tm, d), row), pl.BlockSpec((tm, 1), row), pl.BlockSpec((tm, 1), row),
                  pl.BlockSpec((None, 1, d), _mod_index(gate2.shape[0], seq // tm)),
                  pl.BlockSpec(memory_space=pl.ANY)],
        out_specs=pl.BlockSpec((tm, d), row),
        scratch_shapes=[pltpu.VMEM((win * nout, LANES), F32), pltpu.SemaphoreType.DMA],
    )
    return pl.pallas_call(
        functools.partial(_combine_body, win=win, d=d, npairs=npairs),
        grid_spec=grid_spec,
        out_shape=jax.ShapeDtypeStruct((t, d), F32),
        compiler_params=_cparams(1),
    )(tile_bounds, x1, seg0, seg1, gate2, g)


def _dft_channel_table():
    k = np.arange(FOURIER_CH)
    ang = 2.0 * np.pi * ((k[:, None] * k[None, :]) % FOURIER_CH) / FOURIER_CH
    return jnp.asarray(np.concatenate([np.cos(ang), np.sin(ang)], axis=1), BF16)


def _dft_position_tables(seq):
    k = jnp.arange(seq, dtype=I32)
    ang = ((k[:, None] * k[None, :]) % seq).astype(F32) * (2.0 * math.pi / seq)
    scale = 1.0 / math.sqrt(seq * FOURIER_CH)
    return (jnp.cos(ang) * scale).astype(BF16), (jnp.sin(ang) * scale).astype(BF16)


def _rope_pattern(seq, width):
    nf = width // 4
    pos = np.arange(seq)
    inv = ROPE_BASE ** (-np.arange(nf, dtype=np.float64) / nf)
    ar = (pos // GRID_W)[:, None] * inv
    ac = (pos % GRID_W)[:, None] * inv
    cos = np.concatenate([np.cos(ar), np.cos(ar), np.cos(ac), np.cos(ac)], axis=1)
    sin = np.concatenate([-np.sin(ar), np.sin(ar), -np.sin(ac), np.sin(ac)], axis=1)
    return cos, sin


def _rope_tables(seq):
    cd, sd = _rope_pattern(seq, DIFF_HD)
    diff = (np.tile(cd, (1, 2 * DIFF_HEADS)), np.tile(sd, (1, 2 * DIFF_HEADS)))
    cm, sm = _rope_pattern(seq, MLA_ROPE)
    pad_r = MLA_HEAD_PAD - MLA_QK_HD
    cm = np.concatenate([np.ones((seq, MLA_NOPE)), cm, np.ones((seq, pad_r))], axis=1)
    sm = np.concatenate([np.zeros((seq, MLA_NOPE)), sm, np.zeros((seq, pad_r))], axis=1)
    mla = (np.tile(cm, (1, MLA_HEADS)), np.tile(sm, (1, MLA_HEADS)))
    as_f32 = lambda pair: tuple(jnp.asarray(a, F32) for a in pair)
    return as_f32(diff), as_f32(mla)


def _pad_heads(a, width):
    lead = a.shape[:-1]
    a = a.reshape(lead + (MLA_HEADS, width))
    a = jnp.pad(a, [(0, 0)] * len(lead) + [(0, 0), (0, MLA_HEAD_PAD - width)])
    return a.reshape(lead + (MLA_PAD_W,))


def _layer_weights(p, l):
    w_in = p["w_in"][l]
    c0 = FOURIER_W
    c1 = c0 + 3 * DIFF_W
    c2 = c1 + MLA_Q_LORA + MLA_KV_LORA
    c3 = c2 + MLA_ROPE
    kr_cols = jnp.pad(w_in[:, c2:c3], ((0, 0), (MLA_NOPE, LANES - MLA_QK_HD)))
    w_kvb = p["mla_w_kvb"][l].reshape(MLA_KV_LORA, MLA_HEADS, MLA_NOPE + MLA_V)
    router = jnp.pad(p["moe_w_router"][l], ((0, 0), (0, LANES - N_EXPERTS)))
    router_hi = router.astype(BF16)
    tile = lambda v, reps: jnp.tile(v, reps)[None, :].astype(F32)
    pad_gain = lambda v: jnp.tile(jnp.pad(v, (0, MLA_HEAD_PAD - MLA_QK_HD)), MLA_HEADS)[None, :].astype(F32)
    return dict(
        fourier=w_in[:, :c0].astype(BF16),
        diff=w_in[:, c0:c1].astype(BF16),
        mla_in=jnp.concatenate([w_in[:, c1:c2], kr_cols], axis=1).astype(BF16),
        gates=w_in[:, c3:].astype(BF16),
        diff_qn=tile(p["diff_qnorm_w"][l], 2 * DIFF_HEADS),
        diff_kn=tile(p["diff_knorm_w"][l], 2 * DIFF_HEADS),
        subln=p["diff_subln_w"][l][None, :].astype(F32),
        lamv=jnp.pad(jnp.stack([p["diff_lambda_q1"][l], p["diff_lambda_k1"][l],
                                p["diff_lambda_q2"][l], p["diff_lambda_k2"][l]]).astype(F32),
                     ((0, SUBLANES - 4), (0, LANES - DIFF_HD))),
        mla=dict(
            qa_norm=p["mla_qa_norm_w"][l][None, :].astype(F32),
            w_qb=_pad_heads(p["mla_w_qb"][l], MLA_QK_HD).astype(BF16),
            q_norm=pad_gain(p["mla_qnorm_w"][l]),
            kva_norm=p["mla_kva_norm_w"][l][None, :].astype(F32),
            w_k=_pad_heads(w_kvb[:, :, :MLA_NOPE].reshape(MLA_KV_LORA, -1), MLA_NOPE).astype(BF16),
            w_v=_pad_heads(w_kvb[:, :, MLA_NOPE:].reshape(MLA_KV_LORA, -1), MLA_V).astype(BF16),
            k_norm=pad_gain(p["mla_knorm_w"][l]),
        ),
        merge=dict(
            br_f=p["w_br_fourier"][l].astype(BF16),
            br_d=p["w_br_diff"][l].astype(BF16),
            br_m=jnp.pad(p["w_br_mla"][l].reshape(MLA_HEADS, MLA_V, -1),
                         ((0, 0), (0, MLA_HEAD_PAD - MLA_V), (0, 0))).reshape(MLA_PAD_W, -1).astype(BF16),
            out=p["w_out"][l].astype(BF16),
            router_hi=router_hi,
            router_lo=(router - router_hi.astype(F32)).astype(BF16),
        ),
        moe_gate=p["moe_w_gate"][l].astype(BF16),
        moe_up=p["moe_w_up"][l].astype(BF16),
        moe_down=p["moe_w_down"][l].astype(BF16),
        norm1=p["norm1_w"][l][None, :].astype(F32),
        norm2=p["norm2_w"][l][None, :].astype(F32),
    )


def _trunk_layer(x, mods, w, lam_init, nbatch, seq, tabs, ctx):
    t, d = x.shape
    sh1, sc1, g1, sh2, sc2, g2 = mods
    h = norm_mod(x, w["norm1"], sc1, sh1, seq)
    gates = gates_proj(h, w["gates"])
    ab = fourier_channel(h, w["fourier"], tabs["dft_ch"])
    y_f = fourier_position(ab, tabs["dft_cos"], tabs["dft_sin"], seq)
    new_ctx = None
    if ctx is None:
        q_d, k_d, v_d, k_d32, v_d32 = diff_qkv(h, w["diff"], w["diff_qn"], w["diff_kn"], seq, None)
        q_m, k_m, v_m, ckv32, krb32 = mla_proj(h, w["mla_in"], w["mla"], seq, None)
        new_ctx = (k_d32, v_d32, ckv32, krb32[:, MLA_NOPE:MLA_QK_HD])
        ctx_d = ctx_m = None
    else:
        q_d, k_d, v_d = diff_qkv(h, w["diff"], w["diff_qn"], w["diff_kn"], seq, tabs["rope_diff"])
        q_m, k_m, v_m = mla_proj(h, w["mla_in"], w["mla"], seq, tabs["rope_mla"])
        kd_c, vd_c, ckv_c, kr_c = ctx
        ctx_d = (kd_c.astype(BF16), vd_c.astype(BF16))
        krb_c = jnp.pad(kr_c, ((0, 0), (MLA_NOPE, LANES - MLA_QK_HD)))
        ctx_m = mla_ctx_keys(ckv_c, krb_c, w["mla"])
    o_d = attention(q_d, k_d, v_d, ctx_d, nbatch=nbatch, seq=seq, heads=DIFF_HEADS, nmaps=2,
                    lamv=w["lamv"], subln=w["subln"], lam_init=lam_init)
    o_m = attention(q_m, k_m, v_m, ctx_m, nbatch=nbatch, seq=seq, heads=MLA_HEADS, nmaps=1)
    x1, h2p, aff_t = merge_out(x, y_f, o_d, o_m, gates, w["merge"], g1, w["norm2"], sc2, sh2, seq)
    idx, dst, gv, seg = route(aff_t, nbatch, seq)
    cap = idx.shape[1]
    rt = _pick_tile(nbatch * cap, 512)
    by_expert = lambda a: jnp.transpose(a[:, :, :N_EXPERTS], (2, 0, 1))
    idx_e = by_expert(idx).reshape(-1, 1, rt)
    dst_e = by_expert(dst).reshape(-1, 1, rt)
    gv_e = by_expert(gv).reshape(-1, 1)
    g = moe_experts(idx_e, dst_e, gv_e, h2p, w["moe_gate"], w["moe_up"], w["moe_down"], d)
    seg0 = seg[:, 0, :].reshape(t, 1)
    seg1 = seg[:, 1, :].reshape(t, 1)
    tm = _pick_tile(seq, 256)
    npairs = nbatch * N_EXPERTS * cap
    tile_bounds = jnp.concatenate([seg0[::tm, 0], jnp.full((1,), npairs, I32)])
    x2 = combine(x1, seg0, seg1, tile_bounds, g2, g, seq)
    return x2, new_ctx


def kernel(x_prompt, x_sample, cache_diff_k, cache_diff_v, cache_mla_ckv, cache_mla_krope, c, c_ctx, w_ada, b_ada, norm1_w, norm2_w, w_in, diff_qnorm_w, diff_knorm_w, diff_lambda_q1, diff_lambda_k1, diff_lambda_q2, diff_lambda_k2, diff_subln_w, mla_qa_norm_w, mla_w_qb, mla_kva_norm_w, mla_w_kvb, mla_qnorm_w, mla_knorm_w, w_br_fourier, w_br_diff, w_br_mla, w_out, moe_w_router, moe_w_gate, moe_w_up, moe_w_down):
    params = dict(w_in=w_in, norm1_w=norm1_w, norm2_w=norm2_w,
                  diff_qnorm_w=diff_qnorm_w, diff_knorm_w=diff_knorm_w,
                  diff_lambda_q1=diff_lambda_q1, diff_lambda_k1=diff_lambda_k1,
                  diff_lambda_q2=diff_lambda_q2, diff_lambda_k2=diff_lambda_k2, diff_subln_w=diff_subln_w,
                  mla_qa_norm_w=mla_qa_norm_w, mla_w_qb=mla_w_qb, mla_kva_norm_w=mla_kva_norm_w,
                  mla_w_kvb=mla_w_kvb, mla_qnorm_w=mla_qnorm_w, mla_knorm_w=mla_knorm_w,
                  w_br_fourier=w_br_fourier, w_br_diff=w_br_diff, w_br_mla=w_br_mla, w_out=w_out,
                  moe_w_router=moe_w_router, moe_w_gate=moe_w_gate, moe_w_up=moe_w_up, moe_w_down=moe_w_down)
    bp, lp, d = x_prompt.shape
    bs, ls, _ = x_sample.shape
    depth = w_in.shape[0]
    past = cache_diff_k.shape[2]

    cond = jnp.concatenate([c, c_ctx[None, :], jnp.zeros((COND_ROWS - bs - 1, d), F32)], axis=0)
    mods = adaln(cond, w_ada, b_ada)

    dft_ch = _dft_channel_table()
    rope_diff, rope_mla = _rope_tables(ls)
    tabs_p = dict(dft_ch=dft_ch)
    tabs_p["dft_cos"], tabs_p["dft_sin"] = _dft_position_tables(lp)
    tabs_s = dict(dft_ch=dft_ch, rope_diff=rope_diff, rope_mla=rope_mla)
    tabs_s["dft_cos"], tabs_s["dft_sin"] = _dft_position_tables(ls)

    y_p = x_prompt.reshape(bp * lp, d)
    y_s = x_sample.reshape(bs * ls, d)
    new_ctx = []
    for l in range(depth):
        w = _layer_weights(params, l)
        lam_init = 0.8 - 0.6 * math.exp(-0.3 * l)
        m = mods[l].reshape(COND_ROWS, N_ADA, d)
        mods_s = [m[:bs, j][:, None, :] for j in range(N_ADA)]
        mods_p = [m[bs:bs + 1, j][:, None, :] for j in range(N_ADA)]
        y_p, ctx_l = _trunk_layer(y_p, mods_p, w, lam_init, bp, lp, tabs_p, None)
        new_ctx.append(ctx_l)
        ctx = (cache_diff_k[:, l].reshape(bs * past, DIFF_W), cache_diff_v[:, l].reshape(bs * past, DIFF_W),
               cache_mla_ckv[:, l].reshape(bs * past, MLA_KV_LORA), cache_mla_krope[:, l].reshape(bs * past, MLA_ROPE))
        y_s, _ = _trunk_layer(y_s, mods_s, w, lam_init, bs, ls, tabs_s, ctx)

    stack = lambda j, shape: jnp.stack([n[j].reshape((bp, lp) + shape) for n in new_ctx], axis=1)
    return (y_p.reshape(bp, lp, d), y_s.reshape(bs, ls, d),
            stack(0, (DIFF_HEADS, 2, DIFF_HD)), stack(1, (DIFF_HEADS, 2 * DIFF_HD)),
            stack(2, (MLA_KV_LORA,)), stack(3, (MLA_ROPE,)))
```

```python
import functools
import math

import jax
import jax.numpy as jnp
import numpy as np
from jax import lax
from jax.experimental import pallas as pl
from jax.experimental.pallas import tpu as pltpu

F32, BF16, I32, U32 = jnp.float32, jnp.bfloat16, jnp.int32, jnp.uint32

GRID_W = 64
ROPE_BASE = 10000.0
EPS = 1e-6
N_ADA = 6
FOURIER_GROUPS = 4
FOURIER_CH = 128
FOURIER_W = FOURIER_GROUPS * FOURIER_CH
DIFF_HEADS = 4
DIFF_HD = 64
DIFF_W = DIFF_HEADS * 2 * DIFF_HD
MLA_HEADS = 8
MLA_NOPE = 64
MLA_ROPE = 32
MLA_QK_HD = MLA_NOPE + MLA_ROPE
MLA_V = 64
MLA_Q_LORA = 384
MLA_KV_LORA = 256
N_EXPERTS = 16
EC_FACTOR = 2

LANES = 128
SUBLANES = 8
VMEM_LIMIT_BYTES = 56 * 1024 * 1024
LOG2E = math.log2(math.e)
MLA_HEAD_PAD = LANES
MLA_PAD_W = MLA_HEADS * MLA_HEAD_PAD
COND_ROWS = 16


def _cparams(n_axes, **kw):
    return pltpu.CompilerParams(dimension_semantics=("arbitrary",) * n_axes,
                                vmem_limit_bytes=VMEM_LIMIT_BYTES, **kw)


def _dot(a, b):
    return jnp.dot(a, b, preferred_element_type=F32)


def _dot_nt(a, b):
    return lax.dot_general(a, b, (((1,), (1,)), ((), ())), preferred_element_type=F32)


def _pick_tile(n, target):
    t = min(n, target)
    while n % t:
        t //= 2
    return t


def _pack_bf16_pairs(x):
    half = x.shape[1] // 2
    bits = pltpu.bitcast(x.astype(jnp.bfloat16).astype(F32), U32)
    return (bits[:, :half] >> 16) | bits[:, half:]


def _store_token_tiles(ref, packed):
    rows, w = packed.shape
    n = w // LANES
    for c in range(n):
        ref[pl.ds(c, rows, stride=n), :] = packed[:, c * LANES:(c + 1) * LANES]


def _unpack_rows(load_chunk, nchunk):
    lo, hi = [], []
    for c in range(nchunk):
        wds = load_chunk(c)
        lo.append(pltpu.bitcast(wds << 16, F32).astype(BF16))
        hi.append(pltpu.bitcast(wds & jnp.uint32(0xFFFF0000), F32).astype(BF16))
    return jnp.concatenate(lo + hi, axis=1)


def _adaln_body(c_ref, w_ref, b_ref, o_ref):
    c = c_ref[...]
    a = (c * jax.nn.sigmoid(c)).astype(BF16)
    o_ref[...] = _dot(a, w_ref[...].astype(BF16)) + b_ref[...]


def adaln(cond, w_ada, b_ada):
    depth, d, n = w_ada.shape
    tn = _pick_tile(n, 1024)
    return pl.pallas_call(
        _adaln_body,
        grid=(depth, n // tn),
        in_specs=[pl.BlockSpec((COND_ROWS, d), lambda l, j: (0, 0)),
                  pl.BlockSpec((None, d, tn), lambda l, j: (l, 0, j)),
                  pl.BlockSpec((None, 1, tn), lambda l, j: (l, 0, j))],
        out_specs=pl.BlockSpec((None, COND_ROWS, tn), lambda l, j: (l, 0, j)),
        out_shape=jax.ShapeDtypeStruct((depth, COND_ROWS, n), F32),
        compiler_params=_cparams(2),
    )(cond, w_ada, b_ada.reshape(depth, 1, n))


def _norm_mod(x, nw, sc, sh):
    r = lax.rsqrt(jnp.mean(x * x, axis=-1, keepdims=True) + EPS)
    return (x * r) * nw * (1.0 + sc) + sh


def _norm_mod_body(x_ref, nw_ref, sc_ref, sh_ref, h_ref):
    h_ref[...] = _norm_mod(x_ref[...], nw_ref[...], sc_ref[...], sh_ref[...]).astype(BF16)


def _mod_index(nb, tiles_per_batch):
    if nb == 1:
        return lambda i, *_: (0, 0, 0)
    return lambda i, *_: (i // tiles_per_batch, 0, 0)


def norm_mod(x, nw, sc, sh, seq):
    t, d = x.shape
    tm = _pick_tile(seq, 512)
    mod_spec = pl.BlockSpec((None, 1, d), _mod_index(sc.shape[0], seq // tm))
    return pl.pallas_call(
        _norm_mod_body,
        grid=(t // tm,),
        in_specs=[pl.BlockSpec((tm, d), lambda i: (i, 0)),
                  pl.BlockSpec((1, d), lambda i: (0, 0)),
                  mod_spec, mod_spec],
        out_specs=pl.BlockSpec((tm, d), lambda i: (i, 0)),
        out_shape=jax.ShapeDtypeStruct((t, d), BF16),
        compiler_params=_cparams(1),
    )(x, nw, sc, sh)


def _gates_body(h_ref, w_ref, o_ref):
    o_ref[...] = jax.nn.sigmoid(_dot(h_ref[...], w_ref[...])).astype(BF16)


def gates_proj(h, w):
    t, d = h.shape
    n = w.shape[1]
    tm, tn = _pick_tile(t, 1024), _pick_tile(n, 1024)
    return pl.pallas_call(
        _gates_body,
        grid=(t // tm, n // tn),
        in_specs=[pl.BlockSpec((tm, d), lambda i, j: (i, 0)),
                  pl.BlockSpec((d, tn), lambda i, j: (0, j))],
        out_specs=pl.BlockSpec((tm, tn), lambda i, j: (i, j)),
        out_shape=jax.ShapeDtypeStruct((t, n), BF16),
        compiler_params=_cparams(2),
    )(h, w)


def _fourier_ch_body(h_ref, w_ref, cs_ref, o_ref):
    u = _dot(h_ref[...], w_ref[...]).astype(BF16)
    for g in range(FOURIER_GROUPS):
        ab = _dot(u[:, g * FOURIER_CH:(g + 1) * FOURIER_CH], cs_ref[...])
        o_ref[:, g * FOURIER_CH:(g + 1) * FOURIER_CH] = ab[:, :FOURIER_CH].astype(BF16)
        o_ref[:, FOURIER_W + g * FOURIER_CH:FOURIER_W + (g + 1) * FOURIER_CH] = ab[:, FOURIER_CH:].astype(BF16)


def fourier_channel(h, w, cs):
    t, d = h.shape
    tm = _pick_tile(t, 1024)
    return pl.pallas_call(
        _fourier_ch_body,
        grid=(t // tm,),
        in_specs=[pl.BlockSpec((tm, d), lambda i: (i, 0)),
                  pl.BlockSpec((d, FOURIER_W), lambda i: (0, 0)),
                  pl.BlockSpec((FOURIER_CH, 2 * FOURIER_CH), lambda i: (0, 0))],
        out_specs=pl.BlockSpec((tm, 2 * FOURIER_W), lambda i: (i, 0)),
        out_shape=jax.ShapeDtypeStruct((t, 2 * FOURIER_W), BF16),
        compiler_params=_cparams(1),
    )(h, w, cs)


def _fourier_pos_body(c_ref, s_ref, ab_ref, o_ref):
    y = _dot(c_ref[...], ab_ref[:, :FOURIER_W]) - _dot(s_ref[...], ab_ref[:, FOURIER_W:])
    o_ref[...] = y.astype(BF16)


def fourier_position(ab, cpos, spos, seq):
    t = ab.shape[0]
    tr = _pick_tile(seq, 512)
    nr = seq // tr
    return pl.pallas_call(
        _fourier_pos_body,
        grid=(t // seq, nr),
        in_specs=[pl.BlockSpec((tr, seq), lambda b, r: (r, 0)),
                  pl.BlockSpec((tr, seq), lambda b, r: (r, 0)),
                  pl.BlockSpec((seq, 2 * FOURIER_W), lambda b, r: (b, 0))],
        out_specs=pl.BlockSpec((tr, FOURIER_W), lambda b, r: (b * nr + r, 0)),
        out_shape=jax.ShapeDtypeStruct((t, FOURIER_W), BF16),
        compiler_params=_cparams(2),
    )(cpos, spos, ab)


def _rope(x, cos, sin, off):
    w = x.shape[1]
    lane = lax.broadcasted_iota(I32, (1, w), 1)
    first = (lane & off) == 0
    partner = jnp.where(first, pltpu.roll(x, w - off, 1), pltpu.roll(x, off, 1))
    return x * cos + partner * sin


def _half_tile_norm(x):
    outs = []
    lane = lax.broadcasted_iota(I32, (1, LANES), 1)
    low = lane < DIFF_HD
    for j in range(x.shape[1] // LANES):
        seg = x[:, j * LANES:(j + 1) * LANES]
        sq = seg * seg
        s_lo = jnp.sum(jnp.where(low, sq, 0.0), axis=-1, keepdims=True)
        s_hi = jnp.sum(jnp.where(low, 0.0, sq), axis=-1, keepdims=True)
        ms = jnp.where(low, s_lo, s_hi) * (1.0 / DIFF_HD)
        outs.append(seg * lax.rsqrt(ms + EPS))
    return jnp.concatenate(outs, axis=1)


def _tile_norm(x, width):
    outs = []
    for j in range(x.shape[1] // LANES):
        seg = x[:, j * LANES:(j + 1) * LANES]
        ms = jnp.sum(seg * seg, axis=-1, keepdims=True) * (1.0 / width)
        outs.append(seg * lax.rsqrt(ms + EPS))
    return jnp.concatenate(outs, axis=1)


def _diff_qkv_body(*refs, rope):
    if rope:
        h_ref, w_ref, qn_ref, kn_ref, cos_ref, sin_ref, q_out, k_out, v_out = refs
    else:
        h_ref, w_ref, qn_ref, kn_ref, q_out, k_out, v_out, k32_out, v32_out = refs
    z = _dot(h_ref[...], w_ref[...])
    q = _half_tile_norm(z[:, :DIFF_W]) * qn_ref[...]
    k = _half_tile_norm(z[:, DIFF_W:2 * DIFF_W]) * kn_ref[...]
    v = z[:, 2 * DIFF_W:]
    if rope:
        q = _rope(q, cos_ref[...], sin_ref[...], DIFF_HD // 4)
        k = _rope(k, cos_ref[...], sin_ref[...], DIFF_HD // 4)
    else:
        k32_out[...] = k
        v32_out[...] = v
    q_out[...] = (q * (DIFF_HD ** -0.5 * LOG2E)).astype(BF16)
    k_out[...] = k.astype(BF16)
    v_out[...] = v.astype(BF16)


def diff_qkv(h, w, qn, kn, seq, rope_tabs):
    t, d = h.shape
    rope = rope_tabs is not None
    tm = _pick_tile(seq, 512)
    row = lambda i: (i, 0)
    const = lambda i: (0, 0)
    in_specs = [pl.BlockSpec((tm, d), row), pl.BlockSpec((d, 3 * DIFF_W), const),
                pl.BlockSpec((1, DIFF_W), const), pl.BlockSpec((1, DIFF_W), const)]
    args = [h, w, qn, kn]
    out_shape = [jax.ShapeDtypeStruct((t, DIFF_W), BF16)] * 3
    if rope:
        nt = seq // tm
        tab = pl.BlockSpec((tm, DIFF_W), lambda i: (i % nt, 0))
        in_specs += [tab, tab]
        args += list(rope_tabs)
    else:
        out_shape = out_shape + [jax.ShapeDtypeStruct((t, DIFF_W), F32)] * 2
    return pl.pallas_call(
        functools.partial(_diff_qkv_body, rope=rope),
        grid=(t // tm,),
        in_specs=in_specs,
        out_specs=[pl.BlockSpec((tm, DIFF_W), row)] * len(out_shape),
        out_shape=out_shape,
        compiler_params=_cparams(1),
    )(*args)


def _mla_keys(ckv_bf, krb, wk_ref, wv_ref, kn_ref):
    k = _dot(ckv_bf, wk_ref[...]) + jnp.concatenate([krb] * MLA_HEADS, axis=1)
    k = _tile_norm(k, MLA_QK_HD) * kn_ref[...]
    v = _dot(ckv_bf, wv_ref[...])
    return k, v


def _mla_proj_body(*refs, rope):
    (h_ref, w_ref, qan_ref, wqb_ref, qn_ref, kvan_ref, wk_ref, wv_ref, kn_ref), refs = refs[:9], refs[9:]
    if rope:
        cos_ref, sin_ref, q_out, k_out, v_out = refs
    else:
        q_out, k_out, v_out, ckv32_out, kr32_out = refs
    z = _dot(h_ref[...], w_ref[...])
    q_a = z[:, :MLA_Q_LORA]
    kv_a = z[:, MLA_Q_LORA:MLA_Q_LORA + MLA_KV_LORA]
    krb = z[:, MLA_Q_LORA + MLA_KV_LORA:]
    c_q = q_a * lax.rsqrt(jnp.mean(q_a * q_a, axis=-1, keepdims=True) + EPS) * qan_ref[...]
    q = _tile_norm(_dot(c_q.astype(BF16), wqb_ref[...]), MLA_QK_HD) * qn_ref[...]
    c_kv = kv_a * lax.rsqrt(jnp.mean(kv_a * kv_a, axis=-1, keepdims=True) + EPS) * kvan_ref[...]
    k, v = _mla_keys(c_kv.astype(BF16), krb, wk_ref, wv_ref, kn_ref)
    if rope:
        q = _rope(q, cos_ref[...], sin_ref[...], MLA_ROPE // 4)
        k = _rope(k, cos_ref[...], sin_ref[...], MLA_ROPE // 4)
    else:
        ckv32_out[...] = c_kv
        kr32_out[...] = krb
    q_out[...] = (q * (MLA_QK_HD ** -0.5 * LOG2E)).astype(BF16)
    k_out[...] = k.astype(BF16)
    v_out[...] = v.astype(BF16)


def mla_proj(h, w, p, seq, rope_tabs):
    t, d = h.shape
    rope = rope_tabs is not None
    tm = _pick_tile(seq, 512)
    row = lambda i: (i, 0)
    const = lambda i: (0, 0)
    wcols = MLA_Q_LORA + MLA_KV_LORA + LANES
    in_specs = [pl.BlockSpec((tm, d), row), pl.BlockSpec((d, wcols), const),
                pl.BlockSpec((1, MLA_Q_LORA), const), pl.BlockSpec((MLA_Q_LORA, MLA_PAD_W), const),
                pl.BlockSpec((1, MLA_PAD_W), const), pl.BlockSpec((1, MLA_KV_LORA), const),
                pl.BlockSpec((MLA_KV_LORA, MLA_PAD_W), const), pl.BlockSpec((MLA_KV_LORA, MLA_PAD_W), const),
                pl.BlockSpec((1, MLA_PAD_W), const)]
    args = [h, w, p["qa_norm"], p["w_qb"], p["q_norm"], p["kva_norm"], p["w_k"], p["w_v"], p["k_norm"]]
    out_shape = [jax.ShapeDtypeStruct((t, MLA_PAD_W), BF16)] * 3
    out_specs = [pl.BlockSpec((tm, MLA_PAD_W), row)] * 3
    if rope:
        nt = seq // tm
        tab = pl.BlockSpec((tm, MLA_PAD_W), lambda i: (i % nt, 0))
        in_specs += [tab, tab]
        args += list(rope_tabs)
    else:
        out_shape += [jax.ShapeDtypeStruct((t, MLA_KV_LORA), F32), jax.ShapeDtypeStruct((t, LANES), F32)]
        out_specs += [pl.BlockSpec((tm, MLA_KV_LORA), row), pl.BlockSpec((tm, LANES), row)]
    return pl.pallas_call(
        functools.partial(_mla_proj_body, rope=rope),
        grid=(t // tm,),
        in_specs=in_specs,
        out_specs=out_specs,
        out_shape=out_shape,
        compiler_params=_cparams(1),
    )(*args)


def _mla_ctx_body(ckv_ref, krb_ref, wk_ref, wv_ref, kn_ref, k_out, v_out):
    k, v = _mla_keys(ckv_ref[...].astype(BF16), krb_ref[...], wk_ref, wv_ref, kn_ref)
    k_out[...] = k.astype(BF16)
    v_out[...] = v.astype(BF16)


def mla_ctx_keys(ckv, krb, p):
    t = ckv.shape[0]
    tm = _pick_tile(t, 512)
    row = lambda i: (i, 0)
    const = lambda i: (0, 0)
    return pl.pallas_call(
        _mla_ctx_body,
        grid=(t // tm,),
        in_specs=[pl.BlockSpec((tm, MLA_KV_LORA), row), pl.BlockSpec((tm, LANES), row),
                  pl.BlockSpec((MLA_KV_LORA, MLA_PAD_W), const), pl.BlockSpec((MLA_KV_LORA, MLA_PAD_W), const),
                  pl.BlockSpec((1, MLA_PAD_W), const)],
        out_specs=[pl.BlockSpec((tm, MLA_PAD_W), row)] * 2,
        out_shape=[jax.ShapeDtypeStruct((t, MLA_PAD_W), BF16)] * 2,
        compiler_params=_cparams(1),
    )(ckv, krb, p["w_k"], p["w_v"], p["k_norm"])


def _attn_body(*refs, nmaps, has_ctx, subln, tk, lam_init):
    refs = list(refs)
    lamv_ref = refs.pop(0) if nmaps == 2 else None
    q_ref, k_ref, v_ref = refs[:3]
    refs = refs[3:]
    segments = []
    if has_ctx:
        segments.append((refs[0], refs[1]))
        refs = refs[2:]
    segments.append((k_ref, v_ref))
    subln_ref = refs.pop(0) if subln else None
    o_ref = refs[0]

    q = q_ref[...]
    tq = q.shape[0]
    if nmaps == 2:
        lane = lax.broadcasted_iota(I32, (1, LANES), 1)
        zero = jnp.zeros_like(q)
        qs = [jnp.where(lane < DIFF_HD, q, zero), jnp.where(lane < DIFF_HD, zero, q)]
    else:
        qs = [q]
    m = [jnp.full((tq, 1), -jnp.inf, F32) for _ in qs]
    l = [jnp.zeros((tq, 1), F32) for _ in qs]
    acc = [jnp.zeros((tq, LANES), F32) for _ in qs]
    for kr, vr in segments:
        n = kr.shape[0]
        for c0 in range(0, n, tk):
            c1 = min(n, c0 + tk)
            kc = kr[c0:c1, :]
            vc = vr[c0:c1, :]
            for i, qi in enumerate(qs):
                s = _dot_nt(qi, kc)
                mn = jnp.maximum(m[i], jnp.max(s, axis=-1, keepdims=True))
                alpha = jnp.exp2(m[i] - mn)
                p = jnp.exp2(s - mn)
                l[i] = alpha * l[i] + jnp.sum(p, axis=-1, keepdims=True)
                acc[i] = alpha * acc[i] + _dot(p.astype(BF16), vc)
                m[i] = mn
    o = acc[0] / l[0]
    if nmaps == 2:
        lv = lamv_ref[...]
        lam = (jnp.exp(jnp.sum(lv[0:1] * lv[1:2], axis=-1, keepdims=True))
               - jnp.exp(jnp.sum(lv[2:3] * lv[3:4], axis=-1, keepdims=True)) + lam_init)
        o = o - lam * (acc[1] / l[1])
    if subln:
        o = o * lax.rsqrt(jnp.mean(o * o, axis=-1, keepdims=True) + EPS) * subln_ref[...] * (1.0 - lam_init)
    o_ref[...] = o.astype(BF16)


def attention(q, k, v, ctx, *, nbatch, seq, heads, nmaps, lamv=None, subln=None, lam_init=0.0):
    t, w = q.shape
    tq = _pick_tile(seq, 512)
    nq = seq // tq
    qspec = pl.BlockSpec((tq, LANES), lambda b, h, i: (b * nq + i, h))
    kvspec = pl.BlockSpec((seq, LANES), lambda b, h, i: (b, h))
    in_specs, args = [], []
    if nmaps == 2:
        in_specs.append(pl.BlockSpec((SUBLANES, LANES), lambda b, h, i: (0, 0)))
        args.append(lamv)
    in_specs += [qspec, kvspec, kvspec]
    args += [q, k, v]
    if ctx is not None:
        past = ctx[0].shape[0] // nbatch
        cspec = pl.BlockSpec((past, LANES), lambda b, h, i: (b, h))
        in_specs += [cspec, cspec]
        args += list(ctx)
    if subln is not None:
        in_specs.append(pl.BlockSpec((1, LANES), lambda b, h, i: (0, 0)))
        args.append(subln)
    return pl.pallas_call(
        functools.partial(_attn_body, nmaps=nmaps, has_ctx=ctx is not None, subln=subln is not None,
                          tk=512, lam_init=lam_init),
        grid=(nbatch, heads, nq),
        in_specs=in_specs,
        out_specs=qspec,
        out_shape=jax.ShapeDtypeStruct((t, w), BF16),
        compiler_params=_cparams(3),
    )(*args)


def _merge_body(x_ref, f_ref, od_ref, om_ref, g0_ref, g1_ref, g2_ref, wf_ref, wd_ref, wm_ref, wo_ref,
                gate1_ref, nw_ref, sc_ref, sh_ref, wrh_ref, wrl_ref,
                x1_ref, h2p_ref, aff_ref):
    merged = g0_ref[...].astype(F32) * _dot(f_ref[...], wf_ref[...])
    merged = merged + g1_ref[...].astype(F32) * _dot(od_ref[...], wd_ref[...])
    merged = merged + g2_ref[...].astype(F32) * _dot(om_ref[...], wm_ref[...])
    x1 = x_ref[...] + gate1_ref[...] * _dot(merged.astype(BF16), wo_ref[...])
    x1_ref[...] = x1
    h2 = _norm_mod(x1, nw_ref[...], sc_ref[...], sh_ref[...])
    _store_token_tiles(h2p_ref, _pack_bf16_pairs(h2))
    h_hi = h2.astype(jnp.bfloat16)
    h_lo = (h2 - h_hi.astype(F32)).astype(BF16)
    logits = _dot(h_hi, wrh_ref[...]) + _dot(h_lo, wrh_ref[...]) + _dot(h_hi, wrl_ref[...])
    lane = lax.broadcasted_iota(I32, (1, LANES), 1)
    logits = jnp.where(lane < N_EXPERTS, logits, -1e30)
    e = jnp.exp(logits - jnp.max(logits, axis=-1, keepdims=True))
    aff = e / jnp.sum(e, axis=-1, keepdims=True)
    aff_ref[...] = aff.T[:N_EXPERTS, :]


def merge_out(x, f, od, om, gates, w, gate1, nw, sc, sh, seq):
    t, d = x.shape
    tm = _pick_tile(seq, 256)
    nchunk = d // 2 // LANES
    row = lambda i: (i, 0)
    const = lambda i: (0, 0)
    once = pl.Buffered(1)
    mod_spec = pl.BlockSpec((None, 1, d), _mod_index(sc.shape[0], seq // tm))
    wspec = lambda a: pl.BlockSpec(a.shape, const, pipeline_mode=once)
    in_specs = [pl.BlockSpec((tm, d), row),
                pl.BlockSpec((tm, f.shape[1]), row), pl.BlockSpec((tm, od.shape[1]), row),
                pl.BlockSpec((tm, om.shape[1]), row),
                pl.BlockSpec((tm, d), lambda i: (i, 0)), pl.BlockSpec((tm, d), lambda i: (i, 1)),
                pl.BlockSpec((tm, d), lambda i: (i, 2)),
                wspec(w["br_f"]), wspec(w["br_d"]), wspec(w["br_m"]), wspec(w["out"]),
                mod_spec, pl.BlockSpec((1, d), const), mod_spec, mod_spec,
                wspec(w["router_hi"]), wspec(w["router_lo"])]
    return pl.pallas_call(
        _merge_body,
        grid=(t // tm,),
        in_specs=in_specs,
        out_specs=[pl.BlockSpec((tm, d), row),
                   pl.BlockSpec((tm * nchunk, LANES), row),
                   pl.BlockSpec((N_EXPERTS, tm), lambda i: (0, i))],
        out_shape=[jax.ShapeDtypeStruct((t, d), F32),
                   jax.ShapeDtypeStruct((t * nchunk, LANES), U32),
                   jax.ShapeDtypeStruct((N_EXPERTS, t), F32)],
        compiler_params=_cparams(1),
    )(x, f, od, om, gates, gates, gates, w["br_f"], w["br_d"], w["br_m"], w["out"],
      gate1, nw, sc, sh, w["router_hi"], w["router_lo"])


def _cumsum_lanes(x):
    n = x.shape[1]
    lane = lax.broadcasted_iota(I32, (1, n), 1)
    s = 1
    while s < n:
        x = x + jnp.where(lane >= s, pltpu.roll(x, s, 1), 0)
        s *= 2
    return x


def _route_body(aff_ref, idx_ref, dst_ref, gv_ref, seg_ref, key_scr, dst_scr, *, cap, row_chunk):
    b = pl.program_id(0)
    aff = aff_ref[...]
    ne, n = aff.shape
    bits = pltpu.bitcast(aff, I32)

    def search(i, thr):
        cand = thr | (1 << (30 - i))
        cnt = jnp.sum(jnp.where(bits >= cand, 1.0, 0.0), axis=-1, keepdims=True)
        return jnp.where(cnt >= cap, cand, thr)

    thr = lax.fori_loop(0, 31, search, jnp.zeros((ne, 1), I32))
    gt = bits > thr
    eq = (bits == thr).astype(I32)
    need = cap - jnp.sum(jnp.where(gt, 1.0, 0.0), axis=-1, keepdims=True).astype(I32)
    eq_before = _cumsum_lanes(eq) - eq
    sel = jnp.where(gt | ((eq > 0) & (eq_before < need)), 1, 0)
    cum = _cumsum_lanes(sel)
    key_scr[...] = sel * cum

    before = jnp.zeros((1, n), I32)
    for e in range(ne):
        dst_scr[e:e + 1, :] = before
        before = before + sel[e:e + 1, :]
    k_tok = before
    start = _cumsum_lanes(k_tok) - k_tok + b * (ne * cap)
    dst_scr[...] = dst_scr[...] + start
    seg_ref[0:1, :] = start
    seg_ref[1:2, :] = start + k_tok
    seg_ref[2:SUBLANES, :] = jnp.zeros((SUBLANES - 2, n), I32)

    tok = (lax.broadcasted_iota(I32, (1, n), 1) + b * n).astype(F32)
    lane = lax.broadcasted_iota(I32, (1, LANES), 1)
    idx_ref[...] = jnp.zeros(idx_ref.shape, I32)
    dst_ref[...] = jnp.zeros(dst_ref.shape, I32)
    gv_ref[...] = jnp.zeros(gv_ref.shape, F32)

    def per_expert(e, carry):
        key = key_scr[pl.ds(e, 1), :]
        dst = dst_scr[pl.ds(e, 1), :].astype(F32)
        af = aff_ref[pl.ds(e, 1), :]
        for c0 in range(0, cap, row_chunk):
            slot = lax.broadcasted_iota(I32, (row_chunk, 1), 0) + (c0 + 1)
            hit = key == slot
            i_col = jnp.sum(jnp.where(hit, tok, 0.0), axis=-1, keepdims=True).astype(I32)
            d_col = jnp.sum(jnp.where(hit, dst, 0.0), axis=-1, keepdims=True).astype(I32)
            g_col = jnp.sum(jnp.where(hit, af, 0.0), axis=-1, keepdims=True)
            rows = pl.ds(c0, row_chunk)
            idx_ref[rows, :] = jnp.where(lane == e, i_col, idx_ref[rows, :])
            dst_ref[rows, :] = jnp.where(lane == e, d_col, dst_ref[rows, :])
            gv_ref[rows, :] = jnp.where(lane == e, g_col, gv_ref[rows, :])
        return carry

    lax.fori_loop(0, ne, per_expert, 0)


def route(aff_t, nbatch, seq):
    ne = aff_t.shape[0]
    cap = EC_FACTOR * seq // ne
    row_chunk = min(cap, 32)
    tab = lambda dt: jax.ShapeDtypeStruct((nbatch, cap, LANES), dt)
    tspec = pl.BlockSpec((None, cap, LANES), lambda b: (b, 0, 0))
    return pl.pallas_call(
        functools.partial(_route_body, cap=cap, row_chunk=row_chunk),
        grid=(nbatch,),
        in_specs=[pl.BlockSpec((ne, seq), lambda b: (0, b))],
        out_specs=[tspec, tspec, tspec, pl.BlockSpec((None, SUBLANES, seq), lambda b: (b, 0, 0))],
        out_shape=[tab(I32), tab(I32), tab(F32), jax.ShapeDtypeStruct((nbatch, SUBLANES, seq), I32)],
        scratch_shapes=[pltpu.VMEM((ne, seq), I32), pltpu.VMEM((ne, seq), I32)],
        compiler_params=_cparams(1),
    )(aff_t)


DMA_UNROLL = 8


def _moe_body(idx_ref, idx_next_ref, dst_ref, gv_ref, h2p_ref, wg_ref, wu_ref, wd_ref, g_ref,
              xbuf, ybuf, sem_in, sem_out, *, rt, d):
    nw = d // 2 // LANES
    step = pl.program_id(0) * pl.num_programs(1) + pl.program_id(1)
    nsteps = pl.num_programs(0) * pl.num_programs(1)
    slot = step % 2

    def row_tile(r):
        return pl.ds(r * nw if isinstance(r, int) else pl.multiple_of(r * nw, nw), nw)

    def per_row(fn):
        def body(i, carry):
            for u in range(DMA_UNROLL):
                fn(i * DMA_UNROLL + u)
            return carry
        lax.fori_loop(0, rt // DMA_UNROLL, body, 0)

    def gather_copy(src_row, r, sl):
        return pltpu.make_async_copy(h2p_ref.at[row_tile(src_row), :], xbuf.at[sl, row_tile(r), :], sem_in.at[sl])

    def scatter_copy(r, dst_row):
        return pltpu.make_async_copy(ybuf.at[row_tile(r), :], g_ref.at[row_tile(dst_row), :], sem_out)

    @pl.when(step == 0)
    def _():
        per_row(lambda r: gather_copy(idx_ref[0, r], r, slot).start())

    @pl.when(step + 1 < nsteps)
    def _():
        per_row(lambda r: gather_copy(idx_next_ref[0, r], r, 1 - slot).start())

    per_row(lambda r: gather_copy(0, r, slot).wait())
    x = _unpack_rows(lambda c: xbuf[slot, pl.ds(c, rt, stride=nw), :], nw)
    a = _dot(x, wg_ref[...])
    u = _dot(x, wu_ref[...])
    mid = (a * jax.nn.sigmoid(a) * u).astype(BF16)
    y = _pack_bf16_pairs(_dot(mid, wd_ref[...]) * gv_ref[...])

    @pl.when(step > 0)
    def _():
        per_row(lambda r: scatter_copy(r, 0).wait())

    _store_token_tiles(ybuf, y)
    per_row(lambda r: scatter_copy(r, dst_ref[0, r]).start())

    @pl.when(step == nsteps - 1)
    def _():
        per_row(lambda r: scatter_copy(r, 0).wait())


def moe_experts(idx, dst, gv, h2p, wg, wu, wd, d):
    ne, _, ff = wg.shape
    rt = idx.shape[2]
    tiles = idx.shape[0] // ne
    nsteps = ne * tiles
    nw = d // 2 // LANES
    smem = lambda off: pl.BlockSpec((None, 1, rt),
                                    lambda e, j: (jnp.minimum(e * tiles + j + off, nsteps - 1), 0, 0),
                                    memory_space=pltpu.SMEM)
    return pl.pallas_call(
        functools.partial(_moe_body, rt=rt, d=d),
        grid=(ne, tiles),
        in_specs=[smem(0), smem(1), smem(0),
                  pl.BlockSpec((rt, 1), lambda e, j: (e * tiles + j, 0)),
                  pl.BlockSpec(memory_space=pl.ANY),
                  pl.BlockSpec((None, d, ff), lambda e, j: (e, 0, 0)),
                  pl.BlockSpec((None, d, ff), lambda e, j: (e, 0, 0)),
                  pl.BlockSpec((None, ff, d), lambda e, j: (e, 0, 0))],
        out_specs=pl.BlockSpec(memory_space=pl.ANY),
        out_shape=jax.ShapeDtypeStruct((nsteps * rt * nw, LANES), U32),
        scratch_shapes=[pltpu.VMEM((2, rt * nw, LANES), U32),
                        pltpu.VMEM((rt * nw, LANES), U32),
                        pltpu.SemaphoreType.DMA((2,)), pltpu.SemaphoreType.DMA],
        compiler_params=_cparams(2, has_side_effects=True),
    )(idx, idx, dst, gv, h2p, wg, wu, wd)


COMBINE_WINDOW = 1024


def _combine_body(tb_ref, x_ref, s0_ref, s1_ref, gate_ref, g_ref, o_ref, gbuf, sem, *, win, d, npairs):
    t = pl.program_id(0)
    nw = d // 2 // LANES
    first = tb_ref[t]
    last = tb_ref[t + 1]
    s0 = s0_ref[...]
    s1 = s1_ref[...]

    def window_start(lo):
        return jnp.minimum(lo, npairs - win)

    def window_copy(lo, slot):
        rows = pl.ds(pl.multiple_of(window_start(lo) * nw, nw), win * nw)
        return pltpu.make_async_copy(g_ref.at[rows, :], gbuf.at[slot], sem.at[slot])

    def window_sum(lo, slot):
        pos = lax.broadcasted_iota(I32, (1, win), 1) + window_start(lo)
        own = (pos >= s0) & (pos < s1) & (pos >= lo)
        sel = jnp.where(own, 1.0, 0.0).astype(BF16)
        rows = _unpack_rows(lambda c: gbuf[slot, pl.ds(c, win, stride=nw), :], nw)
        return _dot(sel, rows)

    slot = t % 2

    @pl.when(t == 0)
    def _():
        window_copy(first, slot).start()

    window_copy(first, slot).wait()

    @pl.when(t + 1 < pl.num_programs(0))
    def _():
        window_copy(last, 1 - slot).start()

    def extra_window(wi, acc):
        lo = first + wi * win
        cp = window_copy(lo, 2)
        cp.start()
        cp.wait()
        return acc + window_sum(lo, 2)

    nwin = (last - first + win - 1) // win
    acc = lax.fori_loop(1, nwin, extra_window, window_sum(first, slot))
    o_ref[...] = x_ref[...] + gate_ref[...] * acc


def combine(x1, seg0, seg1, tile_bounds, gate2, g, seq):
    t, d = x1.shape
    tm = _pick_tile(seq, 256)
    nw = d // 2 // LANES
    npairs = g.shape[0] // nw
    win = min(COMBINE_WINDOW, npairs)
    row = lambda i, tb: (i, 0)
    grid_spec = pltpu.PrefetchScalarGridSpec(
        num_scalar_prefetch=1,
        grid=(t // tm,),
        in_specs=[pl.BlockSpec((tm, d), row), pl.BlockSpec((tm, 1), row), pl.BlockSpec((tm, 1), row),
                  pl.BlockSpec((None, 1, d), _mod_index(gate2.shape[0], seq // tm)),
                  pl.BlockSpec(memory_space=pl.ANY)],
        out_specs=pl.BlockSpec((tm, d), row),
        scratch_shapes=[pltpu.VMEM((3, win * nw, LANES), U32), pltpu.SemaphoreType.DMA((3,))],
    )
    return pl.pallas_call(
        functools.partial(_combine_body, win=win, d=d, npairs=npairs),
        grid_spec=grid_spec,
        out_shape=jax.ShapeDtypeStruct((t, d), F32),
        compiler_params=_cparams(1),
    )(tile_bounds, x1, seg0, seg1, gate2, g)


def _dft_channel_table():
    k = np.arange(FOURIER_CH)
    ang = 2.0 * np.pi * ((k[:, None] * k[None, :]) % FOURIER_CH) / FOURIER_CH
    return jnp.asarray(np.concatenate([np.cos(ang), np.sin(ang)], axis=1), BF16)


def _dft_position_tables(seq):
    k = jnp.arange(seq, dtype=I32)
    ang = ((k[:, None] * k[None, :]) % seq).astype(F32) * (2.0 * math.pi / seq)
    scale = 1.0 / math.sqrt(seq * FOURIER_CH)
    return (jnp.cos(ang) * scale).astype(BF16), (jnp.sin(ang) * scale).astype(BF16)


def _rope_pattern(seq, width):
    nf = width // 4
    pos = np.arange(seq)
    inv = ROPE_BASE ** (-np.arange(nf, dtype=np.float64) / nf)
    ar = (pos // GRID_W)[:, None] * inv
    ac = (pos % GRID_W)[:, None] * inv
    cos = np.concatenate([np.cos(ar), np.cos(ar), np.cos(ac), np.cos(ac)], axis=1)
    sin = np.concatenate([-np.sin(ar), np.sin(ar), -np.sin(ac), np.sin(ac)], axis=1)
    return cos, sin


def _rope_tables(seq):
    cd, sd = _rope_pattern(seq, DIFF_HD)
    diff = (np.tile(cd, (1, 2 * DIFF_HEADS)), np.tile(sd, (1, 2 * DIFF_HEADS)))
    cm, sm = _rope_pattern(seq, MLA_ROPE)
    pad_r = MLA_HEAD_PAD - MLA_QK_HD
    cm = np.concatenate([np.ones((seq, MLA_NOPE)), cm, np.ones((seq, pad_r))], axis=1)
    sm = np.concatenate([np.zeros((seq, MLA_NOPE)), sm, np.zeros((seq, pad_r))], axis=1)
    mla = (np.tile(cm, (1, MLA_HEADS)), np.tile(sm, (1, MLA_HEADS)))
    as_f32 = lambda pair: tuple(jnp.asarray(a, F32) for a in pair)
    return as_f32(diff), as_f32(mla)


def _pad_heads(a, width):
    lead = a.shape[:-1]
    a = a.reshape(lead + (MLA_HEADS, width))
    a = jnp.pad(a, [(0, 0)] * len(lead) + [(0, 0), (0, MLA_HEAD_PAD - width)])
    return a.reshape(lead + (MLA_PAD_W,))


def _layer_weights(p, l):
    w_in = p["w_in"][l]
    c0 = FOURIER_W
    c1 = c0 + 3 * DIFF_W
    c2 = c1 + MLA_Q_LORA + MLA_KV_LORA
    c3 = c2 + MLA_ROPE
    kr_cols = jnp.pad(w_in[:, c2:c3], ((0, 0), (MLA_NOPE, LANES - MLA_QK_HD)))
    w_kvb = p["mla_w_kvb"][l].reshape(MLA_KV_LORA, MLA_HEADS, MLA_NOPE + MLA_V)
    router = jnp.pad(p["moe_w_router"][l], ((0, 0), (0, LANES - N_EXPERTS)))
    router_hi = router.astype(BF16)
    tile = lambda v, reps: jnp.tile(v, reps)[None, :].astype(F32)
    pad_gain = lambda v: jnp.tile(jnp.pad(v, (0, MLA_HEAD_PAD - MLA_QK_HD)), MLA_HEADS)[None, :].astype(F32)
    return dict(
        fourier=w_in[:, :c0].astype(BF16),
        diff=w_in[:, c0:c1].astype(BF16),
        mla_in=jnp.concatenate([w_in[:, c1:c2], kr_cols], axis=1).astype(BF16),
        gates=w_in[:, c3:].astype(BF16),
        diff_qn=tile(p["diff_qnorm_w"][l], 2 * DIFF_HEADS),
        diff_kn=tile(p["diff_knorm_w"][l], 2 * DIFF_HEADS),
        subln=p["diff_subln_w"][l][None, :].astype(F32),
        lamv=jnp.pad(jnp.stack([p["diff_lambda_q1"][l], p["diff_lambda_k1"][l],
                                p["diff_lambda_q2"][l], p["diff_lambda_k2"][l]]).astype(F32),
                     ((0, SUBLANES - 4), (0, LANES - DIFF_HD))),
        mla=dict(
            qa_norm=p["mla_qa_norm_w"][l][None, :].astype(F32),
            w_qb=_pad_heads(p["mla_w_qb"][l], MLA_QK_HD).astype(BF16),
            q_norm=pad_gain(p["mla_qnorm_w"][l]),
            kva_norm=p["mla_kva_norm_w"][l][None, :].astype(F32),
            w_k=_pad_heads(w_kvb[:, :, :MLA_NOPE].reshape(MLA_KV_LORA, -1), MLA_NOPE).astype(BF16),
            w_v=_pad_heads(w_kvb[:, :, MLA_NOPE:].reshape(MLA_KV_LORA, -1), MLA_V).astype(BF16),
            k_norm=pad_gain(p["mla_knorm_w"][l]),
        ),
        merge=dict(
            br_f=p["w_br_fourier"][l].astype(BF16),
            br_d=p["w_br_diff"][l].astype(BF16),
            br_m=jnp.pad(p["w_br_mla"][l].reshape(MLA_HEADS, MLA_V, -1),
                         ((0, 0), (0, MLA_HEAD_PAD - MLA_V), (0, 0))).reshape(MLA_PAD_W, -1).astype(BF16),
            out=p["w_out"][l].astype(BF16),
            router_hi=router_hi,
            router_lo=(router - router_hi.astype(F32)).astype(BF16),
        ),
        moe_gate=p["moe_w_gate"][l].astype(BF16),
        moe_up=p["moe_w_up"][l].astype(BF16),
        moe_down=p["moe_w_down"][l].astype(BF16),
        norm1=p["norm1_w"][l][None, :].astype(F32),
        norm2=p["norm2_w"][l][None, :].astype(F32),
    )


def _trunk_layer(x, mods, w, lam_init, nbatch, seq, tabs, ctx):
    t, d = x.shape
    sh1, sc1, g1, sh2, sc2, g2 = mods
    h = norm_mod(x, w["norm1"], sc1, sh1, seq)
    gates = gates_proj(h, w["gates"])
    ab = fourier_channel(h, w["fourier"], tabs["dft_ch"])
    y_f = fourier_position(ab, tabs["dft_cos"], tabs["dft_sin"], seq)
    new_ctx = None
    if ctx is None:
        q_d, k_d, v_d, k_d32, v_d32 = diff_qkv(h, w["diff"], w["diff_qn"], w["diff_kn"], seq, None)
        q_m, k_m, v_m, ckv32, krb32 = mla_proj(h, w["mla_in"], w["mla"], seq, None)
        new_ctx = (k_d32, v_d32, ckv32, krb32[:, MLA_NOPE:MLA_QK_HD])
        ctx_d = ctx_m = None
    else:
        q_d, k_d, v_d = diff_qkv(h, w["diff"], w["diff_qn"], w["diff_kn"], seq, tabs["rope_diff"])
        q_m, k_m, v_m = mla_proj(h, w["mla_in"], w["mla"], seq, tabs["rope_mla"])
        kd_c, vd_c, ckv_c, kr_c = ctx
        ctx_d = (kd_c.astype(BF16), vd_c.astype(BF16))
        krb_c = jnp.pad(kr_c, ((0, 0), (MLA_NOPE, LANES - MLA_QK_HD)))
        ctx_m = mla_ctx_keys(ckv_c, krb_c, w["mla"])
    o_d = attention(q_d, k_d, v_d, ctx_d, nbatch=nbatch, seq=seq, heads=DIFF_HEADS, nmaps=2,
                    lamv=w["lamv"], subln=w["subln"], lam_init=lam_init)
    o_m = attention(q_m, k_m, v_m, ctx_m, nbatch=nbatch, seq=seq, heads=MLA_HEADS, nmaps=1)
    x1, h2p, aff_t = merge_out(x, y_f, o_d, o_m, gates, w["merge"], g1, w["norm2"], sc2, sh2, seq)
    idx, dst, gv, seg = route(aff_t, nbatch, seq)
    cap = idx.shape[1]
    rt = _pick_tile(nbatch * cap, 512)
    by_expert = lambda a: jnp.transpose(a[:, :, :N_EXPERTS], (2, 0, 1))
    idx_e = by_expert(idx).reshape(-1, 1, rt)
    dst_e = by_expert(dst).reshape(-1, 1, rt)
    gv_e = by_expert(gv).reshape(-1, 1)
    g = moe_experts(idx_e, dst_e, gv_e, h2p, w["moe_gate"], w["moe_up"], w["moe_down"], d)
    seg0 = seg[:, 0, :].reshape(t, 1)
    seg1 = seg[:, 1, :].reshape(t, 1)
    tm = _pick_tile(seq, 256)
    npairs = nbatch * N_EXPERTS * cap
    tile_bounds = jnp.concatenate([seg0[::tm, 0], jnp.full((1,), npairs, I32)])
    x2 = combine(x1, seg0, seg1, tile_bounds, g2, g, seq)
    return x2, new_ctx


def kernel(x_prompt, x_sample, cache_diff_k, cache_diff_v, cache_mla_ckv, cache_mla_krope, c, c_ctx, w_ada, b_ada, norm1_w, norm2_w, w_in, diff_qnorm_w, diff_knorm_w, diff_lambda_q1, diff_lambda_k1, diff_lambda_q2, diff_lambda_k2, diff_subln_w, mla_qa_norm_w, mla_w_qb, mla_kva_norm_w, mla_w_kvb, mla_qnorm_w, mla_knorm_w, w_br_fourier, w_br_diff, w_br_mla, w_out, moe_w_router, moe_w_gate, moe_w_up, moe_w_down):
    params = dict(w_in=w_in, norm1_w=norm1_w, norm2_w=norm2_w,
                  diff_qnorm_w=diff_qnorm_w, diff_knorm_w=diff_knorm_w,
                  diff_lambda_q1=diff_lambda_q1, diff_lambda_k1=diff_lambda_k1,
                  diff_lambda_q2=diff_lambda_q2, diff_lambda_k2=diff_lambda_k2, diff_subln_w=diff_subln_w,
                  mla_qa_norm_w=mla_qa_norm_w, mla_w_qb=mla_w_qb, mla_kva_norm_w=mla_kva_norm_w,
                  mla_w_kvb=mla_w_kvb, mla_qnorm_w=mla_qnorm_w, mla_knorm_w=mla_knorm_w,
                  w_br_fourier=w_br_fourier, w_br_diff=w_br_diff, w_br_mla=w_br_mla, w_out=w_out,
                  moe_w_router=moe_w_router, moe_w_gate=moe_w_gate, moe_w_up=moe_w_up, moe_w_down=moe_w_down)
    bp, lp, d = x_prompt.shape
    bs, ls, _ = x_sample.shape
    depth = w_in.shape[0]
    past = cache_diff_k.shape[2]

    cond = jnp.concatenate([c, c_ctx[None, :], jnp.zeros((COND_ROWS - bs - 1, d), F32)], axis=0)
    mods = adaln(cond, w_ada, b_ada)

    dft_ch = _dft_channel_table()
    rope_diff, rope_mla = _rope_tables(ls)
    tabs_p = dict(dft_ch=dft_ch)
    tabs_p["dft_cos"], tabs_p["dft_sin"] = _dft_position_tables(lp)
    tabs_s = dict(dft_ch=dft_ch, rope_diff=rope_diff, rope_mla=rope_mla)
    tabs_s["dft_cos"], tabs_s["dft_sin"] = _dft_position_tables(ls)

    y_p = x_prompt.reshape(bp * lp, d)
    y_s = x_sample.reshape(bs * ls, d)
    new_ctx = []
    for l in range(depth):
        w = _layer_weights(params, l)
        lam_init = 0.8 - 0.6 * math.exp(-0.3 * l)
        m = mods[l].reshape(COND_ROWS, N_ADA, d)
        mods_s = [m[:bs, j][:, None, :] for j in range(N_ADA)]
        mods_p = [m[bs:bs + 1, j][:, None, :] for j in range(N_ADA)]
        y_p, ctx_l = _trunk_layer(y_p, mods_p, w, lam_init, bp, lp, tabs_p, None)
        new_ctx.append(ctx_l)
        ctx = (cache_diff_k[:, l].reshape(bs * past, DIFF_W), cache_diff_v[:, l].reshape(bs * past, DIFF_W),
               cache_mla_ckv[:, l].reshape(bs * past, MLA_KV_LORA), cache_mla_krope[:, l].reshape(bs * past, MLA_ROPE))
        y_s, _ = _trunk_layer(y_s, mods_s, w, lam_init, bs, ls, tabs_s, ctx)

    stack = lambda j, shape: jnp.stack([n[j].reshape((bp, lp) + shape) for n in new_ctx], axis=1)
    return (y_p.reshape(bp, lp, d), y_s.reshape(bs, ls, d),
            stack(0, (DIFF_HEADS, 2, DIFF_HD)), stack(1, (DIFF_HEADS, 2 * DIFF_HD)),
            stack(2, (MLA_KV_LORA,)), stack(3, (MLA_ROPE,)))
```

```python
import functools
import math

import jax
import jax.numpy as jnp
import numpy as np
from jax import lax
from jax.experimental import pallas as pl
from jax.experimental.pallas import tpu as pltpu

F32, BF16, I32, U32 = jnp.float32, jnp.bfloat16, jnp.int32, jnp.uint32

GRID_W = 64
ROPE_BASE = 10000.0
EPS = 1e-6
N_ADA = 6
FOURIER_GROUPS = 4
FOURIER_CH = 128
FOURIER_W = FOURIER_GROUPS * FOURIER_CH
DIFF_HEADS = 4
DIFF_HD = 64
DIFF_W = DIFF_HEADS * 2 * DIFF_HD
MLA_HEADS = 8
MLA_NOPE = 64
MLA_ROPE = 32
MLA_QK_HD = MLA_NOPE + MLA_ROPE
MLA_V = 64
MLA_Q_LORA = 384
MLA_KV_LORA = 256
N_EXPERTS = 16
EC_FACTOR = 2

LANES = 128
SUBLANES = 8
VMEM_LIMIT_BYTES = 56 * 1024 * 1024
LOG2E = math.log2(math.e)
MLA_HEAD_PAD = LANES
MLA_PAD_W = MLA_HEADS * MLA_HEAD_PAD
COND_ROWS = 16


def _cparams(n_axes, **kw):
    return pltpu.CompilerParams(dimension_semantics=("arbitrary",) * n_axes,
                                vmem_limit_bytes=VMEM_LIMIT_BYTES, **kw)


def _dot(a, b):
    return jnp.dot(a, b, preferred_element_type=F32)


def _dot_nt(a, b):
    return lax.dot_general(a, b, (((1,), (1,)), ((), ())), preferred_element_type=F32)


def _pick_tile(n, target):
    t = min(n, target)
    while n % t:
        t //= 2
    return t


def _pack_bf16_pairs(x):
    half = x.shape[1] // 2
    bits = pltpu.bitcast(x.astype(jnp.bfloat16).astype(F32), U32)
    return (bits[:, :half] >> 16) | bits[:, half:]


def _store_token_tiles(ref, packed):
    rows, w = packed.shape
    n = w // LANES
    for c in range(n):
        ref[pl.ds(c, rows, stride=n), :] = packed[:, c * LANES:(c + 1) * LANES]


def _unpack_rows(load_chunk, nchunk):
    lo, hi = [], []
    for c in range(nchunk):
        wds = load_chunk(c)
        lo.append(pltpu.bitcast(wds << 16, F32).astype(BF16))
        hi.append(pltpu.bitcast(wds & jnp.uint32(0xFFFF0000), F32).astype(BF16))
    return jnp.concatenate(lo + hi, axis=1)


def _adaln_body(c_ref, w_ref, b_ref, o_ref):
    c = c_ref[...]
    a = (c * jax.nn.sigmoid(c)).astype(BF16)
    o_ref[...] = _dot(a, w_ref[...].astype(BF16)) + b_ref[...]


def adaln(cond, w_ada, b_ada):
    depth, d, n = w_ada.shape
    tn = _pick_tile(n, 1024)
    return pl.pallas_call(
        _adaln_body,
        grid=(depth, n // tn),
        in_specs=[pl.BlockSpec((COND_ROWS, d), lambda l, j: (0, 0)),
                  pl.BlockSpec((None, d, tn), lambda l, j: (l, 0, j)),
                  pl.BlockSpec((None, 1, tn), lambda l, j: (l, 0, j))],
        out_specs=pl.BlockSpec((None, COND_ROWS, tn), lambda l, j: (l, 0, j)),
        out_shape=jax.ShapeDtypeStruct((depth, COND_ROWS, n), F32),
        compiler_params=_cparams(2),
    )(cond, w_ada, b_ada.reshape(depth, 1, n))


def _norm_mod(x, nw, sc, sh):
    r = lax.rsqrt(jnp.mean(x * x, axis=-1, keepdims=True) + EPS)
    return (x * r) * nw * (1.0 + sc) + sh


def _norm_mod_body(x_ref, nw_ref, sc_ref, sh_ref, h_ref):
    h_ref[...] = _norm_mod(x_ref[...], nw_ref[...], sc_ref[...], sh_ref[...]).astype(BF16)


def _mod_index(nb, tiles_per_batch):
    if nb == 1:
        return lambda i, *_: (0, 0, 0)
    return lambda i, *_: (i // tiles_per_batch, 0, 0)


def norm_mod(x, nw, sc, sh, seq):
    t, d = x.shape
    tm = _pick_tile(seq, 512)
    mod_spec = pl.BlockSpec((None, 1, d), _mod_index(sc.shape[0], seq // tm))
    return pl.pallas_call(
        _norm_mod_body,
        grid=(t // tm,),
        in_specs=[pl.BlockSpec((tm, d), lambda i: (i, 0)),
                  pl.BlockSpec((1, d), lambda i: (0, 0)),
                  mod_spec, mod_spec],
        out_specs=pl.BlockSpec((tm, d), lambda i: (i, 0)),
        out_shape=jax.ShapeDtypeStruct((t, d), BF16),
        compiler_params=_cparams(1),
    )(x, nw, sc, sh)


def _gates_body(h_ref, w_ref, o_ref):
    o_ref[...] = jax.nn.sigmoid(_dot(h_ref[...], w_ref[...])).astype(BF16)


def gates_proj(h, w):
    t, d = h.shape
    n = w.shape[1]
    tm, tn = _pick_tile(t, 1024), _pick_tile(n, 1024)
    return pl.pallas_call(
        _gates_body,
        grid=(t // tm, n // tn),
        in_specs=[pl.BlockSpec((tm, d), lambda i, j: (i, 0)),
                  pl.BlockSpec((d, tn), lambda i, j: (0, j))],
        out_specs=pl.BlockSpec((tm, tn), lambda i, j: (i, j)),
        out_shape=jax.ShapeDtypeStruct((t, n), BF16),
        compiler_params=_cparams(2),
    )(h, w)


def _fourier_ch_body(h_ref, w_ref, cs_ref, o_ref):
    u = _dot(h_ref[...], w_ref[...]).astype(BF16)
    for g in range(FOURIER_GROUPS):
        ab = _dot(u[:, g * FOURIER_CH:(g + 1) * FOURIER_CH], cs_ref[...])
        o_ref[:, g * FOURIER_CH:(g + 1) * FOURIER_CH] = ab[:, :FOURIER_CH].astype(BF16)
        o_ref[:, FOURIER_W + g * FOURIER_CH:FOURIER_W + (g + 1) * FOURIER_CH] = ab[:, FOURIER_CH:].astype(BF16)


def fourier_channel(h, w, cs):
    t, d = h.shape
    tm = _pick_tile(t, 1024)
    return pl.pallas_call(
        _fourier_ch_body,
        grid=(t // tm,),
        in_specs=[pl.BlockSpec((tm, d), lambda i: (i, 0)),
                  pl.BlockSpec((d, FOURIER_W), lambda i: (0, 0)),
                  pl.BlockSpec((FOURIER_CH, 2 * FOURIER_CH), lambda i: (0, 0))],
        out_specs=pl.BlockSpec((tm, 2 * FOURIER_W), lambda i: (i, 0)),
        out_shape=jax.ShapeDtypeStruct((t, 2 * FOURIER_W), BF16),
        compiler_params=_cparams(1),
    )(h, w, cs)


def _fourier_pos_body(c_ref, s_ref, ab_ref, o_ref):
    y = _dot(c_ref[...], ab_ref[:, :FOURIER_W]) - _dot(s_ref[...], ab_ref[:, FOURIER_W:])
    o_ref[...] = y.astype(BF16)


def fourier_position(ab, cpos, spos, seq):
    t = ab.shape[0]
    tr = _pick_tile(seq, 512)
    nr = seq // tr
    return pl.pallas_call(
        _fourier_pos_body,
        grid=(t // seq, nr),
        in_specs=[pl.BlockSpec((tr, seq), lambda b, r: (r, 0)),
                  pl.BlockSpec((tr, seq), lambda b, r: (r, 0)),
                  pl.BlockSpec((seq, 2 * FOURIER_W), lambda b, r: (b, 0))],
        out_specs=pl.BlockSpec((tr, FOURIER_W), lambda b, r: (b * nr + r, 0)),
        out_shape=jax.ShapeDtypeStruct((t, FOURIER_W), BF16),
        compiler_params=_cparams(2),
    )(cpos, spos, ab)


def _rope(x, cos, sin, off):
    w = x.shape[1]
    lane = lax.broadcasted_iota(I32, (1, w), 1)
    first = (lane & off) == 0
    partner = jnp.where(first, pltpu.roll(x, w - off, 1), pltpu.roll(x, off, 1))
    return x * cos + partner * sin


def _half_tile_norm(x):
    outs = []
    lane = lax.broadcasted_iota(I32, (1, LANES), 1)
    low = lane < DIFF_HD
    for j in range(x.shape[1] // LANES):
        seg = x[:, j * LANES:(j + 1) * LANES]
        sq = seg * seg
        s_lo = jnp.sum(jnp.where(low, sq, 0.0), axis=-1, keepdims=True)
        s_hi = jnp.sum(jnp.where(low, 0.0, sq), axis=-1, keepdims=True)
        ms = jnp.where(low, s_lo, s_hi) * (1.0 / DIFF_HD)
        outs.append(seg * lax.rsqrt(ms + EPS))
    return jnp.concatenate(outs, axis=1)


def _diff_qkv_body(*refs, rope):
    if rope:
        h_ref, w_ref, qn_ref, kn_ref, cos_ref, sin_ref, q_out, k_out, v_out = refs
    else:
        h_ref, w_ref, qn_ref, kn_ref, q_out, k_out, v_out, k32_out, v32_out = refs
    z = _dot(h_ref[...], w_ref[...])
    q = _half_tile_norm(z[:, :DIFF_W]) * qn_ref[...]
    k = _half_tile_norm(z[:, DIFF_W:2 * DIFF_W]) * kn_ref[...]
    v = z[:, 2 * DIFF_W:]
    if rope:
        q = _rope(q, cos_ref[...], sin_ref[...], DIFF_HD // 4)
        k = _rope(k, cos_ref[...], sin_ref[...], DIFF_HD // 4)
    else:
        k32_out[...] = k
        v32_out[...] = v
    q_out[...] = (q * (DIFF_HD ** -0.5 * LOG2E)).astype(BF16)
    k_out[...] = k.astype(BF16)
    v_out[...] = v.astype(BF16)


def diff_qkv(h, w, qn, kn, seq, rope_tabs):
    t, d = h.shape
    rope = rope_tabs is not None
    tm = _pick_tile(seq, 512)
    row = lambda i: (i, 0)
    const = lambda i: (0, 0)
    in_specs = [pl.BlockSpec((tm, d), row), pl.BlockSpec((d, 3 * DIFF_W), const),
                pl.BlockSpec((1, DIFF_W), const), pl.BlockSpec((1, DIFF_W), const)]
    args = [h, w, qn, kn]
    out_shape = [jax.ShapeDtypeStruct((t, DIFF_W), BF16)] * 3
    if rope:
        nt = seq // tm
        tab = pl.BlockSpec((tm, DIFF_W), lambda i: (i % nt, 0))
        in_specs += [tab, tab]
        args += list(rope_tabs)
    else:
        out_shape = out_shape + [jax.ShapeDtypeStruct((t, DIFF_W), F32)] * 2
    return pl.pallas_call(
        functools.partial(_diff_qkv_body, rope=rope),
        grid=(t // tm,),
        in_specs=in_specs,
        out_specs=[pl.BlockSpec((tm, DIFF_W), row)] * len(out_shape),
        out_shape=out_shape,
        compiler_params=_cparams(1),
    )(*args)


def _mla_keys(ckv_bf, krb, wk_ref, wv_ref, kn_ref, rope_tab):
    kn = _dot(ckv_bf, wk_ref[...])
    gain = kn_ref[...]
    kr = krb * gain
    if rope_tab is not None:
        kr = _rope(kr, rope_tab[0], rope_tab[1], MLA_ROPE // 4)
    kr_sq = jnp.sum(krb * krb, axis=-1, keepdims=True)
    outs = []
    for j in range(MLA_HEADS):
        seg = kn[:, j * LANES:(j + 1) * LANES]
        ms = (jnp.sum(seg * seg, axis=-1, keepdims=True) + kr_sq) * (1.0 / MLA_QK_HD)
        outs.append((seg * gain + kr) * lax.rsqrt(ms + EPS))
    return jnp.concatenate(outs, axis=1), _dot(ckv_bf, wv_ref[...])


def _mla_proj_body(*refs, rope):
    (h_ref, w_ref, qan_ref, wqb_ref, qn_ref, kvan_ref, wk_ref, wv_ref, kn_ref), refs = refs[:9], refs[9:]
    if rope:
        cos_ref, sin_ref, q_out, k_out, v_out = refs
        rope_tab = (cos_ref[...], sin_ref[...])
    else:
        q_out, k_out, v_out, ckv32_out, kr32_out = refs
        rope_tab = None
    z = _dot(h_ref[...], w_ref[...])
    q_a = z[:, :MLA_Q_LORA]
    kv_a = z[:, MLA_Q_LORA:MLA_Q_LORA + MLA_KV_LORA]
    krb = z[:, MLA_Q_LORA + MLA_KV_LORA:]
    c_q = q_a * lax.rsqrt(jnp.mean(q_a * q_a, axis=-1, keepdims=True) + EPS) * qan_ref[...]
    zq = _dot(c_q.astype(BF16), wqb_ref[...])
    gain = qn_ref[...]
    outs = []
    for j in range(MLA_HEADS):
        seg = zq[:, j * LANES:(j + 1) * LANES]
        ms = jnp.sum(seg * seg, axis=-1, keepdims=True) * (1.0 / MLA_QK_HD)
        val = seg * gain
        if rope:
            val = val * rope_tab[0] + zq[:, MLA_PAD_W + j * LANES:MLA_PAD_W + (j + 1) * LANES] * rope_tab[1]
        outs.append(val * (lax.rsqrt(ms + EPS) * (MLA_QK_HD ** -0.5 * LOG2E)))
    q_out[...] = jnp.concatenate(outs, axis=1).astype(BF16)
    c_kv = kv_a * lax.rsqrt(jnp.mean(kv_a * kv_a, axis=-1, keepdims=True) + EPS) * kvan_ref[...]
    k, v = _mla_keys(c_kv.astype(BF16), krb, wk_ref, wv_ref, kn_ref, rope_tab)
    if not rope:
        ckv32_out[...] = c_kv
        kr32_out[...] = krb
    k_out[...] = k.astype(BF16)
    v_out[...] = v.astype(BF16)


def mla_proj(h, w, p, seq, rope_tabs):
    t, d = h.shape
    rope = rope_tabs is not None
    tm = _pick_tile(seq, 512)
    row = lambda i: (i, 0)
    const = lambda i: (0, 0)
    wcols = MLA_Q_LORA + MLA_KV_LORA + LANES
    w_qb = p["w_qb_rope"] if rope else p["w_qb"]
    in_specs = [pl.BlockSpec((tm, d), row), pl.BlockSpec((d, wcols), const),
                pl.BlockSpec((1, MLA_Q_LORA), const), pl.BlockSpec(w_qb.shape, const),
                pl.BlockSpec((1, LANES), const), pl.BlockSpec((1, MLA_KV_LORA), const),
                pl.BlockSpec((MLA_KV_LORA, MLA_PAD_W), const), pl.BlockSpec((MLA_KV_LORA, MLA_PAD_W), const),
                pl.BlockSpec((1, LANES), const)]
    args = [h, w, p["qa_norm"], w_qb, p["q_norm"], p["kva_norm"], p["w_k"], p["w_v"], p["k_norm"]]
    out_shape = [jax.ShapeDtypeStruct((t, MLA_PAD_W), BF16)] * 3
    out_specs = [pl.BlockSpec((tm, MLA_PAD_W), row)] * 3
    if rope:
        nt = seq // tm
        tab = pl.BlockSpec((tm, LANES), lambda i: (i % nt, 0))
        in_specs += [tab, tab]
        args += list(rope_tabs)
    else:
        out_shape += [jax.ShapeDtypeStruct((t, MLA_KV_LORA), F32), jax.ShapeDtypeStruct((t, LANES), F32)]
        out_specs += [pl.BlockSpec((tm, MLA_KV_LORA), row), pl.BlockSpec((tm, LANES), row)]
    return pl.pallas_call(
        functools.partial(_mla_proj_body, rope=rope),
        grid=(t // tm,),
        in_specs=in_specs,
        out_specs=out_specs,
        out_shape=out_shape,
        compiler_params=_cparams(1),
    )(*args)


def _mla_ctx_body(ckv_ref, krb_ref, wk_ref, wv_ref, kn_ref, k_out, v_out):
    k, v = _mla_keys(ckv_ref[...].astype(BF16), krb_ref[...], wk_ref, wv_ref, kn_ref, None)
    k_out[...] = k.astype(BF16)
    v_out[...] = v.astype(BF16)


def mla_ctx_keys(ckv, krb, p):
    t = ckv.shape[0]
    tm = _pick_tile(t, 512)
    row = lambda i: (i, 0)
    const = lambda i: (0, 0)
    return pl.pallas_call(
        _mla_ctx_body,
        grid=(t // tm,),
        in_specs=[pl.BlockSpec((tm, MLA_KV_LORA), row), pl.BlockSpec((tm, LANES), row),
                  pl.BlockSpec((MLA_KV_LORA, MLA_PAD_W), const), pl.BlockSpec((MLA_KV_LORA, MLA_PAD_W), const),
                  pl.BlockSpec((1, LANES), const)],
        out_specs=[pl.BlockSpec((tm, MLA_PAD_W), row)] * 2,
        out_shape=[jax.ShapeDtypeStruct((t, MLA_PAD_W), BF16)] * 2,
        compiler_params=_cparams(1),
    )(ckv, krb, p["w_k"], p["w_v"], p["k_norm"])


SOFTMAX_SUM_FLOOR = 2.0 ** -100


def _attn_body(*refs, nmaps, has_ctx, subln, tk, lam_init):
    refs = list(refs)
    kmax_scr = refs.pop()
    lamv_ref = refs.pop(0) if nmaps == 2 else None
    q_ref, k_ref, v_ref = refs[:3]
    refs = refs[3:]
    segments = []
    if has_ctx:
        segments.append((refs[0], refs[1]))
        refs = refs[2:]
    segments.append((k_ref, v_ref))
    subln_ref = refs.pop(0) if subln else None
    o_ref = refs[0]

    q = q_ref[...]
    tq = q.shape[0]
    if nmaps == 2:
        lane = lax.broadcasted_iota(I32, (1, LANES), 1)
        masks = [lane < DIFF_HD, lane >= DIFF_HD]
        qs = [jnp.where(mk, q, jnp.zeros_like(q)) for mk in masks]
    else:
        masks = [None]
        qs = [q]

    @pl.when(pl.program_id(2) == 0)
    def _():
        for i, mk in enumerate(masks):
            best = jnp.zeros((1, 1), F32)
            for kr, _ in segments:
                kf = kr[...].astype(F32)
                sq = kf * kf if mk is None else jnp.where(mk, kf * kf, 0.0)
                best = jnp.maximum(best, jnp.max(jnp.sum(sq, axis=-1, keepdims=True), axis=0, keepdims=True))
            kmax_scr[i:i + 1, :] = jnp.broadcast_to(best, (1, LANES))

    def key_chunks():
        for kr, vr in segments:
            n = kr.shape[0]
            for c0 in range(0, n, tk):
                c1 = min(n, c0 + tk)
                yield kr[c0:c1, :], vr[c0:c1, :]

    def finish(acc, l):
        o = acc[0] / l[0]
        if nmaps == 2:
            lv = lamv_ref[...]
            lam = (jnp.exp(jnp.sum(lv[0:1] * lv[1:2], axis=-1, keepdims=True))
                   - jnp.exp(jnp.sum(lv[2:3] * lv[3:4], axis=-1, keepdims=True)) + lam_init)
            o = o - lam * (acc[1] / l[1])
        if subln:
            o = o * lax.rsqrt(jnp.mean(o * o, axis=-1, keepdims=True) + EPS) * subln_ref[...] * (1.0 - lam_init)
        o_ref[...] = o.astype(BF16)

    acc, l = [], []
    for i, qi in enumerate(qs):
        qf = qi.astype(F32)
        shift = jnp.sqrt(jnp.sum(qf * qf, axis=-1, keepdims=True) * kmax_scr[i:i + 1, 0:1])
        a = jnp.zeros((tq, LANES), F32)
        li = jnp.zeros((tq, 1), F32)
        for kc, vc in key_chunks():
            p = jnp.exp2(_dot_nt(qi, kc) - shift)
            li = li + jnp.sum(p, axis=-1, keepdims=True)
            a = a + _dot(p.astype(BF16), vc)
        acc.append(a)
        l.append(li)
    finish(acc, l)

    lmin = functools.reduce(jnp.minimum, [jnp.min(li) for li in l])

    @pl.when(lmin < SOFTMAX_SUM_FLOOR)
    def _():
        m = [jnp.full((tq, 1), -jnp.inf, F32) for _ in qs]
        l2 = [jnp.zeros((tq, 1), F32) for _ in qs]
        acc2 = [jnp.zeros((tq, LANES), F32) for _ in qs]
        for kc, vc in key_chunks():
            for i, qi in enumerate(qs):
                s = _dot_nt(qi, kc)
                mn = jnp.maximum(m[i], jnp.max(s, axis=-1, keepdims=True))
                alpha = jnp.exp2(m[i] - mn)
                p = jnp.exp2(s - mn)
                l2[i] = alpha * l2[i] + jnp.sum(p, axis=-1, keepdims=True)
                acc2[i] = alpha * acc2[i] + _dot(p.astype(BF16), vc)
                m[i] = mn
        finish(acc2, l2)


def attention(q, k, v, ctx, *, nbatch, seq, heads, nmaps, lamv=None, subln=None, lam_init=0.0):
    t, w = q.shape
    tq = _pick_tile(seq, 512)
    nq = seq // tq
    qspec = pl.BlockSpec((tq, LANES), lambda b, h, i: (b * nq + i, h))
    kvspec = pl.BlockSpec((seq, LANES), lambda b, h, i: (b, h))
    in_specs, args = [], []
    if nmaps == 2:
        in_specs.append(pl.BlockSpec((SUBLANES, LANES), lambda b, h, i: (0, 0)))
        args.append(lamv)
    in_specs += [qspec, kvspec, kvspec]
    args += [q, k, v]
    if ctx is not None:
        past = ctx[0].shape[0] // nbatch
        cspec = pl.BlockSpec((past, LANES), lambda b, h, i: (b, h))
        in_specs += [cspec, cspec]
        args += list(ctx)
    if subln is not None:
        in_specs.append(pl.BlockSpec((1, LANES), lambda b, h, i: (0, 0)))
        args.append(subln)
    return pl.pallas_call(
        functools.partial(_attn_body, nmaps=nmaps, has_ctx=ctx is not None, subln=subln is not None,
                          tk=512, lam_init=lam_init),
        grid=(nbatch, heads, nq),
        in_specs=in_specs,
        out_specs=qspec,
        out_shape=jax.ShapeDtypeStruct((t, w), BF16),
        scratch_shapes=[pltpu.VMEM((SUBLANES, LANES), F32)],
        compiler_params=_cparams(3),
    )(*args)


def _merge_body(x_ref, f_ref, od_ref, om_ref, g0_ref, g1_ref, g2_ref, wf_ref, wd_ref, wm_ref, wo_ref,
                gate1_ref, nw_ref, sc_ref, sh_ref, wrh_ref, wrl_ref,
                x1_ref, h2p_ref, aff_ref):
    merged = g0_ref[...].astype(F32) * _dot(f_ref[...], wf_ref[...])
    merged = merged + g1_ref[...].astype(F32) * _dot(od_ref[...], wd_ref[...])
    merged = merged + g2_ref[...].astype(F32) * _dot(om_ref[...], wm_ref[...])
    x1 = x_ref[...] + gate1_ref[...] * _dot(merged.astype(BF16), wo_ref[...])
    x1_ref[...] = x1
    h2 = _norm_mod(x1, nw_ref[...], sc_ref[...], sh_ref[...])
    _store_token_tiles(h2p_ref, _pack_bf16_pairs(h2))
    h_hi = h2.astype(jnp.bfloat16)
    h_lo = (h2 - h_hi.astype(F32)).astype(BF16)
    logits = _dot(h_hi, wrh_ref[...]) + _dot(h_lo, wrh_ref[...]) + _dot(h_hi, wrl_ref[...])
    lane = lax.broadcasted_iota(I32, (1, LANES), 1)
    logits = jnp.where(lane < N_EXPERTS, logits, -1e30)
    e = jnp.exp(logits - jnp.max(logits, axis=-1, keepdims=True))
    aff = e / jnp.sum(e, axis=-1, keepdims=True)
    aff_ref[...] = aff.T[:N_EXPERTS, :]


def merge_out(x, f, od, om, gates, w, gate1, nw, sc, sh, seq):
    t, d = x.shape
    tm = _pick_tile(seq, 256)
    nchunk = d // 2 // LANES
    row = lambda i: (i, 0)
    const = lambda i: (0, 0)
    once = pl.Buffered(1)
    mod_spec = pl.BlockSpec((None, 1, d), _mod_index(sc.shape[0], seq // tm))
    wspec = lambda a: pl.BlockSpec(a.shape, const, pipeline_mode=once)
    in_specs = [pl.BlockSpec((tm, d), row),
                pl.BlockSpec((tm, f.shape[1]), row), pl.BlockSpec((tm, od.shape[1]), row),
                pl.BlockSpec((tm, om.shape[1]), row),
                pl.BlockSpec((tm, d), lambda i: (i, 0)), pl.BlockSpec((tm, d), lambda i: (i, 1)),
                pl.BlockSpec((tm, d), lambda i: (i, 2)),
                wspec(w["br_f"]), wspec(w["br_d"]), wspec(w["br_m"]), wspec(w["out"]),
                mod_spec, pl.BlockSpec((1, d), const), mod_spec, mod_spec,
                wspec(w["router_hi"]), wspec(w["router_lo"])]
    return pl.pallas_call(
        _merge_body,
        grid=(t // tm,),
        in_specs=in_specs,
        out_specs=[pl.BlockSpec((tm, d), row),
                   pl.BlockSpec((tm * nchunk, LANES), row),
                   pl.BlockSpec((N_EXPERTS, tm), lambda i: (0, i))],
        out_shape=[jax.ShapeDtypeStruct((t, d), F32),
                   jax.ShapeDtypeStruct((t * nchunk, LANES), U32),
                   jax.ShapeDtypeStruct((N_EXPERTS, t), F32)],
        compiler_params=_cparams(1),
    )(x, f, od, om, gates, gates, gates, w["br_f"], w["br_d"], w["br_m"], w["out"],
      gate1, nw, sc, sh, w["router_hi"], w["router_lo"])


def _cumsum_lanes(x):
    n = x.shape[1]
    lane = lax.broadcasted_iota(I32, (1, n), 1)
    s = 1
    while s < n:
        x = x + jnp.where(lane >= s, pltpu.roll(x, s, 1), 0)
        s *= 2
    return x


def _route_body(aff_ref, idx_ref, dst_ref, gv_ref, seg_ref, key_scr, dst_scr, *, cap, row_chunk):
    b = pl.program_id(0)
    aff = aff_ref[...]
    ne, n = aff.shape
    bits = pltpu.bitcast(aff, I32)

    def search(i, thr):
        cand = thr | (1 << (30 - i))
        cnt = jnp.sum(jnp.where(bits >= cand, 1.0, 0.0), axis=-1, keepdims=True)
        return jnp.where(cnt >= cap, cand, thr)

    thr = lax.fori_loop(0, 31, search, jnp.zeros((ne, 1), I32))
    gt = bits > thr
    eq = (bits == thr).astype(I32)
    need = cap - jnp.sum(jnp.where(gt, 1.0, 0.0), axis=-1, keepdims=True).astype(I32)
    eq_before = _cumsum_lanes(eq) - eq
    sel = jnp.where(gt | ((eq > 0) & (eq_before < need)), 1, 0)
    cum = _cumsum_lanes(sel)
    key_scr[...] = sel * cum

    before = jnp.zeros((1, n), I32)
    for e in range(ne):
        dst_scr[e:e + 1, :] = before
        before = before + sel[e:e + 1, :]
    k_tok = before
    start = _cumsum_lanes(k_tok) - k_tok + b * (ne * cap)
    dst_scr[...] = dst_scr[...] + start
    seg_ref[0:1, :] = start
    seg_ref[1:2, :] = start + k_tok
    seg_ref[2:SUBLANES, :] = jnp.zeros((SUBLANES - 2, n), I32)

    tok = (lax.broadcasted_iota(I32, (1, n), 1) + b * n).astype(F32)
    lane = lax.broadcasted_iota(I32, (1, LANES), 1)
    idx_ref[...] = jnp.zeros(idx_ref.shape, I32)
    dst_ref[...] = jnp.zeros(dst_ref.shape, I32)
    gv_ref[...] = jnp.zeros(gv_ref.shape, F32)

    def per_expert(e, carry):
        key = key_scr[pl.ds(e, 1), :]
        dst = dst_scr[pl.ds(e, 1), :].astype(F32)
        af = aff_ref[pl.ds(e, 1), :]
        for c0 in range(0, cap, row_chunk):
            slot = lax.broadcasted_iota(I32, (row_chunk, 1), 0) + (c0 + 1)
            hit = key == slot
            i_col = jnp.sum(jnp.where(hit, tok, 0.0), axis=-1, keepdims=True).astype(I32)
            d_col = jnp.sum(jnp.where(hit, dst, 0.0), axis=-1, keepdims=True).astype(I32)
            g_col = jnp.sum(jnp.where(hit, af, 0.0), axis=-1, keepdims=True)
            rows = pl.ds(c0, row_chunk)
            idx_ref[rows, :] = jnp.where(lane == e, i_col, idx_ref[rows, :])
            dst_ref[rows, :] = jnp.where(lane == e, d_col, dst_ref[rows, :])
            gv_ref[rows, :] = jnp.where(lane == e, g_col, gv_ref[rows, :])
        return carry

    lax.fori_loop(0, ne, per_expert, 0)


def route(aff_t, nbatch, seq):
    ne = aff_t.shape[0]
    cap = EC_FACTOR * seq // ne
    row_chunk = min(cap, 32)
    tab = lambda dt: jax.ShapeDtypeStruct((nbatch, cap, LANES), dt)
    tspec = pl.BlockSpec((None, cap, LANES), lambda b: (b, 0, 0))
    return pl.pallas_call(
        functools.partial(_route_body, cap=cap, row_chunk=row_chunk),
        grid=(nbatch,),
        in_specs=[pl.BlockSpec((ne, seq), lambda b: (0, b))],
        out_specs=[tspec, tspec, tspec, pl.BlockSpec((None, SUBLANES, seq), lambda b: (b, 0, 0))],
        out_shape=[tab(I32), tab(I32), tab(F32), jax.ShapeDtypeStruct((nbatch, SUBLANES, seq), I32)],
        scratch_shapes=[pltpu.VMEM((ne, seq), I32), pltpu.VMEM((ne, seq), I32)],
        compiler_params=_cparams(1),
    )(aff_t)


DMA_UNROLL = 8


def _moe_body(idx_ref, idx_next_ref, dst_ref, gv_ref, h2p_ref, wg_ref, wu_ref, wd_ref, g_ref,
              xbuf, ybuf, sem_in, sem_out, *, rt, d):
    nw = d // 2 // LANES
    step = pl.program_id(0) * pl.num_programs(1) + pl.program_id(1)
    nsteps = pl.num_programs(0) * pl.num_programs(1)
    slot = step % 2

    def row_tile(r):
        return pl.ds(r * nw if isinstance(r, int) else pl.multiple_of(r * nw, nw), nw)

    def per_row(fn):
        def body(i, carry):
            for u in range(DMA_UNROLL):
                fn(i * DMA_UNROLL + u)
            return carry
        lax.fori_loop(0, rt // DMA_UNROLL, body, 0)

    def gather_copy(src_row, r, sl):
        return pltpu.make_async_copy(h2p_ref.at[row_tile(src_row), :], xbuf.at[sl, row_tile(r), :], sem_in.at[sl])

    def scatter_copy(r, dst_row):
        return pltpu.make_async_copy(ybuf.at[row_tile(r), :], g_ref.at[row_tile(dst_row), :], sem_out)

    @pl.when(step == 0)
    def _():
        per_row(lambda r: gather_copy(idx_ref[0, r], r, slot).start())

    @pl.when(step + 1 < nsteps)
    def _():
        per_row(lambda r: gather_copy(idx_next_ref[0, r], r, 1 - slot).start())

    per_row(lambda r: gather_copy(0, r, slot).wait())
    x = _unpack_rows(lambda c: xbuf[slot, pl.ds(c, rt, stride=nw), :], nw)
    a = _dot(x, wg_ref[...])
    u = _dot(x, wu_ref[...])
    mid = (a * jax.nn.sigmoid(a) * u).astype(BF16)
    y = _pack_bf16_pairs(_dot(mid, wd_ref[...]) * gv_ref[...])

    @pl.when(step > 0)
    def _():
        per_row(lambda r: scatter_copy(r, 0).wait())

    _store_token_tiles(ybuf, y)
    per_row(lambda r: scatter_copy(r, dst_ref[0, r]).start())

    @pl.when(step == nsteps - 1)
    def _():
        per_row(lambda r: scatter_copy(r, 0).wait())


def moe_experts(idx, dst, gv, h2p, wg, wu, wd, d):
    ne, _, ff = wg.shape
    rt = idx.shape[2]
    tiles = idx.shape[0] // ne
    nsteps = ne * tiles
    nw = d // 2 // LANES
    smem = lambda off: pl.BlockSpec((None, 1, rt),
                                    lambda e, j: (jnp.minimum(e * tiles + j + off, nsteps - 1), 0, 0),
                                    memory_space=pltpu.SMEM)
    return pl.pallas_call(
        functools.partial(_moe_body, rt=rt, d=d),
        grid=(ne, tiles),
        in_specs=[smem(0), smem(1), smem(0),
                  pl.BlockSpec((rt, 1), lambda e, j: (e * tiles + j, 0)),
                  pl.BlockSpec(memory_space=pl.ANY),
                  pl.BlockSpec((None, d, ff), lambda e, j: (e, 0, 0)),
                  pl.BlockSpec((None, d, ff), lambda e, j: (e, 0, 0)),
                  pl.BlockSpec((None, ff, d), lambda e, j: (e, 0, 0))],
        out_specs=pl.BlockSpec(memory_space=pl.ANY),
        out_shape=jax.ShapeDtypeStruct((nsteps * rt * nw, LANES), U32),
        scratch_shapes=[pltpu.VMEM((2, rt * nw, LANES), U32),
                        pltpu.VMEM((rt * nw, LANES), U32),
                        pltpu.SemaphoreType.DMA((2,)), pltpu.SemaphoreType.DMA],
        compiler_params=_cparams(2, has_side_effects=True),
    )(idx, idx, dst, gv, h2p, wg, wu, wd)


COMBINE_WINDOW = 1024


def _combine_body(tb_ref, x_ref, s0_ref, s1_ref, gate_ref, g_ref, o_ref, gbuf, sem, *, win, d, npairs):
    t = pl.program_id(0)
    nw = d // 2 // LANES
    first = tb_ref[t]
    last = tb_ref[t + 1]
    s0 = s0_ref[...]
    s1 = s1_ref[...]

    def window_start(lo):
        return jnp.minimum(lo, npairs - win)

    def window_copy(lo, slot):
        rows = pl.ds(pl.multiple_of(window_start(lo) * nw, nw), win * nw)
        return pltpu.make_async_copy(g_ref.at[rows, :], gbuf.at[slot], sem.at[slot])

    def window_sum(lo, slot):
        pos = lax.broadcasted_iota(I32, (1, win), 1) + window_start(lo)
        own = (pos >= s0) & (pos < s1) & (pos >= lo)
        sel = jnp.where(own, 1.0, 0.0).astype(BF16)
        rows = _unpack_rows(lambda c: gbuf[slot, pl.ds(c, win, stride=nw), :], nw)
        return _dot(sel, rows)

    slot = t % 2

    @pl.when(t == 0)
    def _():
        window_copy(first, slot).start()

    window_copy(first, slot).wait()

    @pl.when(t + 1 < pl.num_programs(0))
    def _():
        window_copy(last, 1 - slot).start()

    def extra_window(wi, acc):
        lo = first + wi * win
        cp = window_copy(lo, 2)
        cp.start()
        cp.wait()
        return acc + window_sum(lo, 2)

    nwin = (last - first + win - 1) // win
    acc = lax.fori_loop(1, nwin, extra_window, window_sum(first, slot))
    o_ref[...] = x_ref[...] + gate_ref[...] * acc


def combine(x1, seg0, seg1, tile_bounds, gate2, g, seq):
    t, d = x1.shape
    tm = _pick_tile(seq, 256)
    nw = d // 2 // LANES
    npairs = g.shape[0] // nw
    win = min(COMBINE_WINDOW, npairs)
    row = lambda i, tb: (i, 0)
    grid_spec = pltpu.PrefetchScalarGridSpec(
        num_scalar_prefetch=1,
        grid=(t // tm,),
        in_specs=[pl.BlockSpec((tm, d), row), pl.BlockSpec((tm, 1), row), pl.BlockSpec((tm, 1), row),
                  pl.BlockSpec((None, 1, d), _mod_index(gate2.shape[0], seq // tm)),
                  pl.BlockSpec(memory_space=pl.ANY)],
        out_specs=pl.BlockSpec((tm, d), row),
        scratch_shapes=[pltpu.VMEM((3, win * nw, LANES), U32), pltpu.SemaphoreType.DMA((3,))],
    )
    return pl.pallas_call(
        functools.partial(_combine_body, win=win, d=d, npairs=npairs),
        grid_spec=grid_spec,
        out_shape=jax.ShapeDtypeStruct((t, d), F32),
        compiler_params=_cparams(1),
    )(tile_bounds, x1, seg0, seg1, gate2, g)


def _dft_channel_table():
    k = np.arange(FOURIER_CH)
    ang = 2.0 * np.pi * ((k[:, None] * k[None, :]) % FOURIER_CH) / FOURIER_CH
    return jnp.asarray(np.concatenate([np.cos(ang), np.sin(ang)], axis=1), BF16)


def _dft_position_tables(seq):
    k = jnp.arange(seq, dtype=I32)
    ang = ((k[:, None] * k[None, :]) % seq).astype(F32) * (2.0 * math.pi / seq)
    scale = 1.0 / math.sqrt(seq * FOURIER_CH)
    return (jnp.cos(ang) * scale).astype(BF16), (jnp.sin(ang) * scale).astype(BF16)


def _rope_pattern(seq, width):
    nf = width // 4
    pos = np.arange(seq)
    inv = ROPE_BASE ** (-np.arange(nf, dtype=np.float64) / nf)
    ar = (pos // GRID_W)[:, None] * inv
    ac = (pos % GRID_W)[:, None] * inv
    cos = np.concatenate([np.cos(ar), np.cos(ar), np.cos(ac), np.cos(ac)], axis=1)
    sin = np.concatenate([-np.sin(ar), np.sin(ar), -np.sin(ac), np.sin(ac)], axis=1)
    return cos, sin


def _rope_tables(seq):
    cd, sd = _rope_pattern(seq, DIFF_HD)
    diff = (np.tile(cd, (1, 2 * DIFF_HEADS)), np.tile(sd, (1, 2 * DIFF_HEADS)))
    cm, sm = _rope_pattern(seq, MLA_ROPE)
    pad_r = MLA_HEAD_PAD - MLA_QK_HD
    cm = np.concatenate([np.ones((seq, MLA_NOPE)), cm, np.ones((seq, pad_r))], axis=1)
    sm = np.concatenate([np.zeros((seq, MLA_NOPE)), sm, np.zeros((seq, pad_r))], axis=1)
    mla = (cm, sm)
    as_f32 = lambda pair: tuple(jnp.asarray(a, F32) for a in pair)
    return as_f32(diff), as_f32(mla)


def _pad_heads(a, width):
    lead = a.shape[:-1]
    a = a.reshape(lead + (MLA_HEADS, width))
    a = jnp.pad(a, [(0, 0)] * len(lead) + [(0, 0), (0, MLA_HEAD_PAD - width)])
    return a.reshape(lead + (MLA_PAD_W,))


def _layer_weights(p, l):
    w_in = p["w_in"][l]
    c0 = FOURIER_W
    c1 = c0 + 3 * DIFF_W
    c2 = c1 + MLA_Q_LORA + MLA_KV_LORA
    c3 = c2 + MLA_ROPE
    kr_cols = jnp.pad(w_in[:, c2:c3], ((0, 0), (MLA_NOPE, LANES - MLA_QK_HD)))
    w_kvb = p["mla_w_kvb"][l].reshape(MLA_KV_LORA, MLA_HEADS, MLA_NOPE + MLA_V)
    router = jnp.pad(p["moe_w_router"][l], ((0, 0), (0, LANES - N_EXPERTS)))
    router_hi = router.astype(BF16)
    tile = lambda v, reps: jnp.tile(v, reps)[None, :].astype(F32)
    pad_gain = lambda v: jnp.pad(v, (0, MLA_HEAD_PAD - MLA_QK_HD))[None, :].astype(F32)
    w_qb = _pad_heads(p["mla_w_qb"][l], MLA_QK_HD)
    lane = np.arange(LANES)
    rot = (lane >= MLA_NOPE) & (lane < MLA_QK_HD)
    off = MLA_ROPE // 4
    partner = np.where(rot, np.where((lane & off) == 0, lane + off, lane - off), lane)
    w_qb_partner = ((w_qb.reshape(MLA_Q_LORA, MLA_HEADS, LANES) * pad_gain(p["mla_qnorm_w"][l]))[:, :, partner]
                    * jnp.asarray(rot, F32)).reshape(MLA_Q_LORA, MLA_PAD_W)
    return dict(
        fourier=w_in[:, :c0].astype(BF16),
        diff=w_in[:, c0:c1].astype(BF16),
        mla_in=jnp.concatenate([w_in[:, c1:c2], kr_cols], axis=1).astype(BF16),
        gates=w_in[:, c3:].astype(BF16),
        diff_qn=tile(p["diff_qnorm_w"][l], 2 * DIFF_HEADS),
        diff_kn=tile(p["diff_knorm_w"][l], 2 * DIFF_HEADS),
        subln=p["diff_subln_w"][l][None, :].astype(F32),
        lamv=jnp.pad(jnp.stack([p["diff_lambda_q1"][l], p["diff_lambda_k1"][l],
                                p["diff_lambda_q2"][l], p["diff_lambda_k2"][l]]).astype(F32),
                     ((0, SUBLANES - 4), (0, LANES - DIFF_HD))),
        mla=dict(
            qa_norm=p["mla_qa_norm_w"][l][None, :].astype(F32),
            w_qb=w_qb.astype(BF16),
            w_qb_rope=jnp.concatenate([w_qb, w_qb_partner], axis=1).astype(BF16),
            q_norm=pad_gain(p["mla_qnorm_w"][l]),
            kva_norm=p["mla_kva_norm_w"][l][None, :].astype(F32),
            w_k=_pad_heads(w_kvb[:, :, :MLA_NOPE].reshape(MLA_KV_LORA, -1), MLA_NOPE).astype(BF16),
            w_v=_pad_heads(w_kvb[:, :, MLA_NOPE:].reshape(MLA_KV_LORA, -1), MLA_V).astype(BF16),
            k_norm=pad_gain(p["mla_knorm_w"][l]),
        ),
        merge=dict(
            br_f=p["w_br_fourier"][l].astype(BF16),
            br_d=p["w_br_diff"][l].astype(BF16),
            br_m=jnp.pad(p["w_br_mla"][l].reshape(MLA_HEADS, MLA_V, -1),
                         ((0, 0), (0, MLA_HEAD_PAD - MLA_V), (0, 0))).reshape(MLA_PAD_W, -1).astype(BF16),
            out=p["w_out"][l].astype(BF16),
            router_hi=router_hi,
            router_lo=(router - router_hi.astype(F32)).astype(BF16),
        ),
        moe_gate=p["moe_w_gate"][l].astype(BF16),
        moe_up=p["moe_w_up"][l].astype(BF16),
        moe_down=p["moe_w_down"][l].astype(BF16),
        norm1=p["norm1_w"][l][None, :].astype(F32),
        norm2=p["norm2_w"][l][None, :].astype(F32),
    )


def _trunk_layer(x, mods, w, lam_init, nbatch, seq, tabs, ctx):
    t, d = x.shape
    sh1, sc1, g1, sh2, sc2, g2 = mods
    h = norm_mod(x, w["norm1"], sc1, sh1, seq)
    gates = gates_proj(h, w["gates"])
    ab = fourier_channel(h, w["fourier"], tabs["dft_ch"])
    y_f = fourier_position(ab, tabs["dft_cos"], tabs["dft_sin"], seq)
    new_ctx = None
    if ctx is None:
        q_d, k_d, v_d, k_d32, v_d32 = diff_qkv(h, w["diff"], w["diff_qn"], w["diff_kn"], seq, None)
        q_m, k_m, v_m, ckv32, krb32 = mla_proj(h, w["mla_in"], w["mla"], seq, None)
        new_ctx = (k_d32, v_d32, ckv32, krb32[:, MLA_NOPE:MLA_QK_HD])
        ctx_d = ctx_m = None
    else:
        q_d, k_d, v_d = diff_qkv(h, w["diff"], w["diff_qn"], w["diff_kn"], seq, tabs["rope_diff"])
        q_m, k_m, v_m = mla_proj(h, w["mla_in"], w["mla"], seq, tabs["rope_mla"])
        kd_c, vd_c, ckv_c, kr_c = ctx
        ctx_d = (kd_c.astype(BF16), vd_c.astype(BF16))
        krb_c = jnp.pad(kr_c, ((0, 0), (MLA_NOPE, LANES - MLA_QK_HD)))
        ctx_m = mla_ctx_keys(ckv_c, krb_c, w["mla"])
    o_d = attention(q_d, k_d, v_d, ctx_d, nbatch=nbatch, seq=seq, heads=DIFF_HEADS, nmaps=2,
                    lamv=w["lamv"], subln=w["subln"], lam_init=lam_init)
    o_m = attention(q_m, k_m, v_m, ctx_m, nbatch=nbatch, seq=seq, heads=MLA_HEADS, nmaps=1)
    x1, h2p, aff_t = merge_out(x, y_f, o_d, o_m, gates, w["merge"], g1, w["norm2"], sc2, sh2, seq)
    idx, dst, gv, seg = route(aff_t, nbatch, seq)
    cap = idx.shape[1]
    rt = _pick_tile(nbatch * cap, 512)
    by_expert = lambda a: jnp.transpose(a[:, :, :N_EXPERTS], (2, 0, 1))
    idx_e = by_expert(idx).reshape(-1, 1, rt)
    dst_e = by_expert(dst).reshape(-1, 1, rt)
    gv_e = by_expert(gv).reshape(-1, 1)
    g = moe_experts(idx_e, dst_e, gv_e, h2p, w["moe_gate"], w["moe_up"], w["moe_down"], d)
    seg0 = seg[:, 0, :].reshape(t, 1)
    seg1 = seg[:, 1, :].reshape(t, 1)
    tm = _pick_tile(seq, 256)
    npairs = nbatch * N_EXPERTS * cap
    tile_bounds = jnp.concatenate([seg0[::tm, 0], jnp.full((1,), npairs, I32)])
    x2 = combine(x1, seg0, seg1, tile_bounds, g2, g, seq)
    return x2, new_ctx


def kernel(x_prompt, x_sample, cache_diff_k, cache_diff_v, cache_mla_ckv, cache_mla_krope, c, c_ctx, w_ada, b_ada, norm1_w, norm2_w, w_in, diff_qnorm_w, diff_knorm_w, diff_lambda_q1, diff_lambda_k1, diff_lambda_q2, diff_lambda_k2, diff_subln_w, mla_qa_norm_w, mla_w_qb, mla_kva_norm_w, mla_w_kvb, mla_qnorm_w, mla_knorm_w, w_br_fourier, w_br_diff, w_br_mla, w_out, moe_w_router, moe_w_gate, moe_w_up, moe_w_down):
    params = dict(w_in=w_in, norm1_w=norm1_w, norm2_w=norm2_w,
                  diff_qnorm_w=diff_qnorm_w, diff_knorm_w=diff_knorm_w,
                  diff_lambda_q1=diff_lambda_q1, diff_lambda_k1=diff_lambda_k1,
                  diff_lambda_q2=diff_lambda_q2, diff_lambda_k2=diff_lambda_k2, diff_subln_w=diff_subln_w,
                  mla_qa_norm_w=mla_qa_norm_w, mla_w_qb=mla_w_qb, mla_kva_norm_w=mla_kva_norm_w,
                  mla_w_kvb=mla_w_kvb, mla_qnorm_w=mla_qnorm_w, mla_knorm_w=mla_knorm_w,
                  w_br_fourier=w_br_fourier, w_br_diff=w_br_diff, w_br_mla=w_br_mla, w_out=w_out,
                  moe_w_router=moe_w_router, moe_w_gate=moe_w_gate, moe_w_up=moe_w_up, moe_w_down=moe_w_down)
    bp, lp, d = x_prompt.shape
    bs, ls, _ = x_sample.shape
    depth = w_in.shape[0]
    past = cache_diff_k.shape[2]

    cond = jnp.concatenate([c, c_ctx[None, :], jnp.zeros((COND_ROWS - bs - 1, d), F32)], axis=0)
    mods = adaln(cond, w_ada, b_ada)

    dft_ch = _dft_channel_table()
    rope_diff, rope_mla = _rope_tables(ls)
    tabs_p = dict(dft_ch=dft_ch)
    tabs_p["dft_cos"], tabs_p["dft_sin"] = _dft_position_tables(lp)
    tabs_s = dict(dft_ch=dft_ch, rope_diff=rope_diff, rope_mla=rope_mla)
    tabs_s["dft_cos"], tabs_s["dft_sin"] = _dft_position_tables(ls)

    y_p = x_prompt.reshape(bp * lp, d)
    y_s = x_sample.reshape(bs * ls, d)
    new_ctx = []
    for l in range(depth):
        w = _layer_weights(params, l)
        lam_init = 0.8 - 0.6 * math.exp(-0.3 * l)
        m = mods[l].reshape(COND_ROWS, N_ADA, d)
        mods_s = [m[:bs, j][:, None, :] for j in range(N_ADA)]
        mods_p = [m[bs:bs + 1, j][:, None, :] for j in range(N_ADA)]
        y_p, ctx_l = _trunk_layer(y_p, mods_p, w, lam_init, bp, lp, tabs_p, None)
        new_ctx.append(ctx_l)
        ctx = (cache_diff_k[:, l].reshape(bs * past, DIFF_W), cache_diff_v[:, l].reshape(bs * past, DIFF_W),
               cache_mla_ckv[:, l].reshape(bs * past, MLA_KV_LORA), cache_mla_krope[:, l].reshape(bs * past, MLA_ROPE))
        y_s, _ = _trunk_layer(y_s, mods_s, w, lam_init, bs, ls, tabs_s, ctx)

    stack = lambda j, shape: jnp.stack([n[j].reshape((bp, lp) + shape) for n in new_ctx], axis=1)
    return (y_p.reshape(bp, lp, d), y_s.reshape(bs, ls, d),
            stack(0, (DIFF_HEADS, 2, DIFF_HD)), stack(1, (DIFF_HEADS, 2 * DIFF_HD)),
            stack(2, (MLA_KV_LORA,)), stack(3, (MLA_ROPE,)))
```

```python
import functools
import math

import jax
import jax.numpy as jnp
import numpy as np
from jax import lax
from jax.experimental import pallas as pl
from jax.experimental.pallas import tpu as pltpu

F32, BF16, I32, U32 = jnp.float32, jnp.bfloat16, jnp.int32, jnp.uint32

GRID_W = 64
ROPE_BASE = 10000.0
EPS = 1e-6
N_ADA = 6
FOURIER_GROUPS = 4
FOURIER_CH = 128
FOURIER_W = FOURIER_GROUPS * FOURIER_CH
DIFF_HEADS = 4
DIFF_HD = 64
DIFF_W = DIFF_HEADS * 2 * DIFF_HD
MLA_HEADS = 8
MLA_NOPE = 64
MLA_ROPE = 32
MLA_QK_HD = MLA_NOPE + MLA_ROPE
MLA_V = 64
MLA_Q_LORA = 384
MLA_KV_LORA = 256
N_EXPERTS = 16
EC_FACTOR = 2

LANES = 128
SUBLANES = 8
VMEM_LIMIT_BYTES = 56 * 1024 * 1024
LOG2E = math.log2(math.e)
MLA_HEAD_PAD = LANES
MLA_PAD_W = MLA_HEADS * MLA_HEAD_PAD
COND_ROWS = 16


def _cparams(n_axes, **kw):
    return pltpu.CompilerParams(dimension_semantics=("arbitrary",) * n_axes,
                                vmem_limit_bytes=VMEM_LIMIT_BYTES, **kw)


def _dot(a, b):
    return jnp.dot(a, b, preferred_element_type=F32)


def _dot_nt(a, b):
    return lax.dot_general(a, b, (((1,), (1,)), ((), ())), preferred_element_type=F32)


def _pick_tile(n, target):
    t = min(n, target)
    while n % t:
        t //= 2
    return t


def _pack_bf16_pairs(x):
    half = x.shape[1] // 2
    bits = pltpu.bitcast(x.astype(jnp.bfloat16).astype(F32), U32)
    return (bits[:, :half] >> 16) | bits[:, half:]


def _store_token_tiles(ref, packed):
    rows, w = packed.shape
    n = w // LANES
    for c in range(n):
        ref[pl.ds(c, rows, stride=n), :] = packed[:, c * LANES:(c + 1) * LANES]


def _unpack_rows(load_chunk, nchunk):
    lo, hi = [], []
    for c in range(nchunk):
        wds = load_chunk(c)
        lo.append(pltpu.bitcast(wds << 16, F32).astype(BF16))
        hi.append(pltpu.bitcast(wds & jnp.uint32(0xFFFF0000), F32).astype(BF16))
    return jnp.concatenate(lo + hi, axis=1)


def _adaln_body(c_ref, w_ref, b_ref, o_ref):
    c = c_ref[...]
    a = (c * jax.nn.sigmoid(c)).astype(BF16)
    o_ref[...] = _dot(a, w_ref[...].astype(BF16)) + b_ref[...]


def adaln(cond, w_ada, b_ada):
    depth, d, n = w_ada.shape
    tn = _pick_tile(n, 1024)
    return pl.pallas_call(
        _adaln_body,
        grid=(depth, n // tn),
        in_specs=[pl.BlockSpec((COND_ROWS, d), lambda l, j: (0, 0)),
                  pl.BlockSpec((None, d, tn), lambda l, j: (l, 0, j)),
                  pl.BlockSpec((None, 1, tn), lambda l, j: (l, 0, j))],
        out_specs=pl.BlockSpec((None, COND_ROWS, tn), lambda l, j: (l, 0, j)),
        out_shape=jax.ShapeDtypeStruct((depth, COND_ROWS, n), F32),
        compiler_params=_cparams(2),
    )(cond, w_ada, b_ada.reshape(depth, 1, n))


def _norm_mod(x, nw, sc, sh):
    r = lax.rsqrt(jnp.mean(x * x, axis=-1, keepdims=True) + EPS)
    return (x * r) * nw * (1.0 + sc) + sh


def _mod_index(nb, tiles_per_batch):
    if nb == 1:
        return lambda i, *_: (0, 0, 0)
    return lambda i, *_: (i // tiles_per_batch, 0, 0)


def _gates_body(x_ref, nw_ref, sc_ref, sh_ref, w_ref, h_ref, o_ref):
    @pl.when(pl.program_id(1) == 0)
    def _():
        h_ref[...] = _norm_mod(x_ref[...], nw_ref[...], sc_ref[...], sh_ref[...]).astype(BF16)

    o_ref[...] = jax.nn.sigmoid(_dot(h_ref[...], w_ref[...])).astype(BF16)


def norm_gates_proj(x, nw, sc, sh, w, seq):
    t, d = x.shape
    n = w.shape[1]
    nb = sc.shape[0]
    tm, tn = _pick_tile(seq if nb > 1 else t, 1024), _pick_tile(n, 1024)
    mod_spec = pl.BlockSpec((None, 1, d), _mod_index(nb, seq // tm))
    return pl.pallas_call(
        _gates_body,
        grid=(t // tm, n // tn),
        in_specs=[pl.BlockSpec((tm, d), lambda i, j: (i, 0)),
                  pl.BlockSpec((1, d), lambda i, j: (0, 0)),
                  mod_spec, mod_spec,
                  pl.BlockSpec((d, tn), lambda i, j: (0, j))],
        out_specs=[pl.BlockSpec((tm, d), lambda i, j: (i, 0)),
                   pl.BlockSpec((tm, tn), lambda i, j: (i, j))],
        out_shape=[jax.ShapeDtypeStruct((t, d), BF16), jax.ShapeDtypeStruct((t, n), BF16)],
        compiler_params=_cparams(2),
    )(x, nw, sc, sh, w)


def _fourier_ch_body(h_ref, w_ref, cs_ref, o_ref):
    u = _dot(h_ref[...], w_ref[...]).astype(BF16)
    for g in range(FOURIER_GROUPS):
        ab = _dot(u[:, g * FOURIER_CH:(g + 1) * FOURIER_CH], cs_ref[...])
        o_ref[:, g * FOURIER_CH:(g + 1) * FOURIER_CH] = ab[:, :FOURIER_CH].astype(BF16)
        o_ref[:, FOURIER_W + g * FOURIER_CH:FOURIER_W + (g + 1) * FOURIER_CH] = ab[:, FOURIER_CH:].astype(BF16)


def fourier_channel(h, w, cs):
    t, d = h.shape
    tm = _pick_tile(t, 1024)
    return pl.pallas_call(
        _fourier_ch_body,
        grid=(t // tm,),
        in_specs=[pl.BlockSpec((tm, d), lambda i: (i, 0)),
                  pl.BlockSpec((d, FOURIER_W), lambda i: (0, 0)),
                  pl.BlockSpec((FOURIER_CH, 2 * FOURIER_CH), lambda i: (0, 0))],
        out_specs=pl.BlockSpec((tm, 2 * FOURIER_W), lambda i: (i, 0)),
        out_shape=jax.ShapeDtypeStruct((t, 2 * FOURIER_W), BF16),
        compiler_params=_cparams(1),
    )(h, w, cs)


def _fourier_pos_body(c_ref, s_ref, ab_ref, o_ref):
    y = _dot(c_ref[...], ab_ref[:, :FOURIER_W]) - _dot(s_ref[...], ab_ref[:, FOURIER_W:])
    o_ref[...] = y.astype(BF16)


def fourier_position(ab, cpos, spos, seq):
    t = ab.shape[0]
    tr = _pick_tile(seq, 512)
    nr = seq // tr
    return pl.pallas_call(
        _fourier_pos_body,
        grid=(t // seq, nr),
        in_specs=[pl.BlockSpec((tr, seq), lambda b, r: (r, 0)),
                  pl.BlockSpec((tr, seq), lambda b, r: (r, 0)),
                  pl.BlockSpec((seq, 2 * FOURIER_W), lambda b, r: (b, 0))],
        out_specs=pl.BlockSpec((tr, FOURIER_W), lambda b, r: (b * nr + r, 0)),
        out_shape=jax.ShapeDtypeStruct((t, FOURIER_W), BF16),
        compiler_params=_cparams(2),
    )(cpos, spos, ab)


def _rope(x, cos, sin, off):
    w = x.shape[1]
    lane = lax.broadcasted_iota(I32, (1, w), 1)
    first = (lane & off) == 0
    partner = jnp.where(first, pltpu.roll(x, w - off, 1), pltpu.roll(x, off, 1))
    return x * cos + partner * sin


def _half_tile_norm(x):
    outs = []
    lane = lax.broadcasted_iota(I32, (1, LANES), 1)
    low = lane < DIFF_HD
    for j in range(x.shape[1] // LANES):
        seg = x[:, j * LANES:(j + 1) * LANES]
        sq = seg * seg
        s_lo = jnp.sum(jnp.where(low, sq, 0.0), axis=-1, keepdims=True)
        s_hi = jnp.sum(jnp.where(low, 0.0, sq), axis=-1, keepdims=True)
        ms = jnp.where(low, s_lo, s_hi) * (1.0 / DIFF_HD)
        outs.append(seg * lax.rsqrt(ms + EPS))
    return jnp.concatenate(outs, axis=1)


def _diff_qkv_body(*refs, rope):
    if rope:
        h_ref, w_ref, qn_ref, kn_ref, cos_ref, sin_ref, q_out, k_out, v_out = refs
    else:
        h_ref, w_ref, qn_ref, kn_ref, q_out, k_out, v_out, k32_out, v32_out = refs
    z = _dot(h_ref[...], w_ref[...])
    q = _half_tile_norm(z[:, :DIFF_W]) * qn_ref[...]
    k = _half_tile_norm(z[:, DIFF_W:2 * DIFF_W]) * kn_ref[...]
    v = z[:, 2 * DIFF_W:]
    if rope:
        q = _rope(q, cos_ref[...], sin_ref[...], DIFF_HD // 4)
        k = _rope(k, cos_ref[...], sin_ref[...], DIFF_HD // 4)
    else:
        k32_out[...] = k
        v32_out[...] = v
    q_out[...] = (q * (DIFF_HD ** -0.5 * LOG2E)).astype(BF16)
    k_out[...] = k.astype(BF16)
    v_out[...] = v.astype(BF16)


def diff_qkv(h, w, qn, kn, seq, rope_tabs):
    t, d = h.shape
    rope = rope_tabs is not None
    tm = _pick_tile(seq, 512)
    row = lambda i: (i, 0)
    const = lambda i: (0, 0)
    in_specs = [pl.BlockSpec((tm, d), row), pl.BlockSpec((d, 3 * DIFF_W), const),
                pl.BlockSpec((1, DIFF_W), const), pl.BlockSpec((1, DIFF_W), const)]
    args = [h, w, qn, kn]
    out_shape = [jax.ShapeDtypeStruct((t, DIFF_W), BF16)] * 3
    if rope:
        nt = seq // tm
        tab = pl.BlockSpec((tm, DIFF_W), lambda i: (i % nt, 0))
        in_specs += [tab, tab]
        args += list(rope_tabs)
    else:
        out_shape = out_shape + [jax.ShapeDtypeStruct((t, DIFF_W), F32)] * 2
    return pl.pallas_call(
        functools.partial(_diff_qkv_body, rope=rope),
        grid=(t // tm,),
        in_specs=in_specs,
        out_specs=[pl.BlockSpec((tm, DIFF_W), row)] * len(out_shape),
        out_shape=out_shape,
        compiler_params=_cparams(1),
    )(*args)


def _mla_keys(ckv_bf, krb, wk_ref, wv_ref, kn_ref, rope_tab):
    kn = _dot(ckv_bf, wk_ref[...])
    gain = kn_ref[...]
    kr = krb * gain
    if rope_tab is not None:
        kr = _rope(kr, rope_tab[0], rope_tab[1], MLA_ROPE // 4)
    kr_sq = jnp.sum(krb * krb, axis=-1, keepdims=True)
    outs = []
    for j in range(MLA_HEADS):
        seg = kn[:, j * LANES:(j + 1) * LANES]
        ms = (jnp.sum(seg * seg, axis=-1, keepdims=True) + kr_sq) * (1.0 / MLA_QK_HD)
        outs.append((seg * gain + kr) * lax.rsqrt(ms + EPS))
    return jnp.concatenate(outs, axis=1), _dot(ckv_bf, wv_ref[...])


def _mla_proj_body(*refs, rope):
    (h_ref, w_ref, qan_ref, wqb_ref, qn_ref, kvan_ref, wk_ref, wv_ref, kn_ref), refs = refs[:9], refs[9:]
    if rope:
        cos_ref, sin_ref, q_out, k_out, v_out = refs
        rope_tab = (cos_ref[...], sin_ref[...])
    else:
        q_out, k_out, v_out, ckv32_out, kr32_out = refs
        rope_tab = None
    z = _dot(h_ref[...], w_ref[...])
    q_a = z[:, :MLA_Q_LORA]
    kv_a = z[:, MLA_Q_LORA:MLA_Q_LORA + MLA_KV_LORA]
    krb = z[:, MLA_Q_LORA + MLA_KV_LORA:]
    c_q = q_a * lax.rsqrt(jnp.mean(q_a * q_a, axis=-1, keepdims=True) + EPS) * qan_ref[...]
    zq = _dot(c_q.astype(BF16), wqb_ref[...])
    gain = qn_ref[...]
    outs = []
    for j in range(MLA_HEADS):
        seg = zq[:, j * LANES:(j + 1) * LANES]
        ms = jnp.sum(seg * seg, axis=-1, keepdims=True) * (1.0 / MLA_QK_HD)
        val = seg * gain
        if rope:
            val = val * rope_tab[0] + zq[:, MLA_PAD_W + j * LANES:MLA_PAD_W + (j + 1) * LANES] * rope_tab[1]
        outs.append(val * (lax.rsqrt(ms + EPS) * (MLA_QK_HD ** -0.5 * LOG2E)))
    q_out[...] = jnp.concatenate(outs, axis=1).astype(BF16)
    c_kv = kv_a * lax.rsqrt(jnp.mean(kv_a * kv_a, axis=-1, keepdims=True) + EPS) * kvan_ref[...]
    k, v = _mla_keys(c_kv.astype(BF16), krb, wk_ref, wv_ref, kn_ref, rope_tab)
    if not rope:
        ckv32_out[...] = c_kv
        kr32_out[...] = krb
    k_out[...] = k.astype(BF16)
    v_out[...] = v.astype(BF16)


def mla_proj(h, w, p, seq, rope_tabs):
    t, d = h.shape
    rope = rope_tabs is not None
    tm = _pick_tile(seq, 512)
    row = lambda i: (i, 0)
    const = lambda i: (0, 0)
    wcols = MLA_Q_LORA + MLA_KV_LORA + LANES
    w_qb = p["w_qb_rope"] if rope else p["w_qb"]
    in_specs = [pl.BlockSpec((tm, d), row), pl.BlockSpec((d, wcols), const),
                pl.BlockSpec((1, MLA_Q_LORA), const), pl.BlockSpec(w_qb.shape, const),
                pl.BlockSpec((1, LANES), const), pl.BlockSpec((1, MLA_KV_LORA), const),
                pl.BlockSpec((MLA_KV_LORA, MLA_PAD_W), const), pl.BlockSpec((MLA_KV_LORA, MLA_PAD_W), const),
                pl.BlockSpec((1, LANES), const)]
    args = [h, w, p["qa_norm"], w_qb, p["q_norm"], p["kva_norm"], p["w_k"], p["w_v"], p["k_norm"]]
    out_shape = [jax.ShapeDtypeStruct((t, MLA_PAD_W), BF16)] * 3
    out_specs = [pl.BlockSpec((tm, MLA_PAD_W), row)] * 3
    if rope:
        nt = seq // tm
        tab = pl.BlockSpec((tm, LANES), lambda i: (i % nt, 0))
        in_specs += [tab, tab]
        args += list(rope_tabs)
    else:
        out_shape += [jax.ShapeDtypeStruct((t, MLA_KV_LORA), F32), jax.ShapeDtypeStruct((t, LANES), F32)]
        out_specs += [pl.BlockSpec((tm, MLA_KV_LORA), row), pl.BlockSpec((tm, LANES), row)]
    return pl.pallas_call(
        functools.partial(_mla_proj_body, rope=rope),
        grid=(t // tm,),
        in_specs=in_specs,
        out_specs=out_specs,
        out_shape=out_shape,
        compiler_params=_cparams(1),
    )(*args)


def _mla_ctx_body(ckv_ref, krb_ref, wk_ref, wv_ref, kn_ref, k_out, v_out):
    k, v = _mla_keys(ckv_ref[...].astype(BF16), krb_ref[...], wk_ref, wv_ref, kn_ref, None)
    k_out[...] = k.astype(BF16)
    v_out[...] = v.astype(BF16)


def mla_ctx_keys(ckv, krb, p):
    t = ckv.shape[0]
    tm = _pick_tile(t, 512)
    row = lambda i: (i, 0)
    const = lambda i: (0, 0)
    return pl.pallas_call(
        _mla_ctx_body,
        grid=(t // tm,),
        in_specs=[pl.BlockSpec((tm, MLA_KV_LORA), row), pl.BlockSpec((tm, LANES), row),
                  pl.BlockSpec((MLA_KV_LORA, MLA_PAD_W), const), pl.BlockSpec((MLA_KV_LORA, MLA_PAD_W), const),
                  pl.BlockSpec((1, LANES), const)],
        out_specs=[pl.BlockSpec((tm, MLA_PAD_W), row)] * 2,
        out_shape=[jax.ShapeDtypeStruct((t, MLA_PAD_W), BF16)] * 2,
        compiler_params=_cparams(1),
    )(ckv, krb, p["w_k"], p["w_v"], p["k_norm"])


SOFTMAX_SUM_FLOOR = 2.0 ** -100


def _attn_body(*refs, nmaps, has_ctx, subln, tk, lam_init):
    refs = list(refs)
    kmax_scr = refs.pop()
    lamv_ref = refs.pop(0) if nmaps == 2 else None
    q_ref, k_ref, v_ref = refs[:3]
    refs = refs[3:]
    segments = []
    if has_ctx:
        segments.append((refs[0], refs[1]))
        refs = refs[2:]
    segments.append((k_ref, v_ref))
    subln_ref = refs.pop(0) if subln else None
    o_ref = refs[0]

    q = q_ref[...]
    tq = q.shape[0]
    if nmaps == 2:
        lane = lax.broadcasted_iota(I32, (1, LANES), 1)
        masks = [lane < DIFF_HD, lane >= DIFF_HD]
        qs = [jnp.where(mk, q, jnp.zeros_like(q)) for mk in masks]
    else:
        masks = [None]
        qs = [q]

    @pl.when(pl.program_id(2) == 0)
    def _():
        for i, mk in enumerate(masks):
            best = jnp.zeros((1, 1), F32)
            for kr, _ in segments:
                kf = kr[...].astype(F32)
                sq = kf * kf if mk is None else jnp.where(mk, kf * kf, 0.0)
                best = jnp.maximum(best, jnp.max(jnp.sum(sq, axis=-1, keepdims=True), axis=0, keepdims=True))
            kmax_scr[i:i + 1, :] = jnp.broadcast_to(best, (1, LANES))

    def key_chunks():
        for kr, vr in segments:
            n = kr.shape[0]
            for c0 in range(0, n, tk):
                c1 = min(n, c0 + tk)
                yield kr[c0:c1, :], vr[c0:c1, :]

    def finish(acc, l):
        o = acc[0] / l[0]
        if nmaps == 2:
            lv = lamv_ref[...]
            lam = (jnp.exp(jnp.sum(lv[0:1] * lv[1:2], axis=-1, keepdims=True))
                   - jnp.exp(jnp.sum(lv[2:3] * lv[3:4], axis=-1, keepdims=True)) + lam_init)
            o = o - lam * (acc[1] / l[1])
        if subln:
            o = o * lax.rsqrt(jnp.mean(o * o, axis=-1, keepdims=True) + EPS) * subln_ref[...] * (1.0 - lam_init)
        o_ref[...] = o.astype(BF16)

    acc, l = [], []
    for i, qi in enumerate(qs):
        qf = qi.astype(F32)
        shift = jnp.sqrt(jnp.sum(qf * qf, axis=-1, keepdims=True) * kmax_scr[i:i + 1, 0:1])
        a = jnp.zeros((tq, LANES), F32)
        li = jnp.zeros((tq, 1), F32)
        for kc, vc in key_chunks():
            p = jnp.exp2(_dot_nt(qi, kc) - shift)
            li = li + jnp.sum(p, axis=-1, keepdims=True)
            a = a + _dot(p.astype(BF16), vc)
        acc.append(a)
        l.append(li)
    finish(acc, l)

    lmin = functools.reduce(jnp.minimum, [jnp.min(li) for li in l])

    @pl.when(lmin < SOFTMAX_SUM_FLOOR)
    def _():
        m = [jnp.full((tq, 1), -jnp.inf, F32) for _ in qs]
        l2 = [jnp.zeros((tq, 1), F32) for _ in qs]
        acc2 = [jnp.zeros((tq, LANES), F32) for _ in qs]
        for kc, vc in key_chunks():
            for i, qi in enumerate(qs):
                s = _dot_nt(qi, kc)
                mn = jnp.maximum(m[i], jnp.max(s, axis=-1, keepdims=True))
                alpha = jnp.exp2(m[i] - mn)
                p = jnp.exp2(s - mn)
                l2[i] = alpha * l2[i] + jnp.sum(p, axis=-1, keepdims=True)
                acc2[i] = alpha * acc2[i] + _dot(p.astype(BF16), vc)
                m[i] = mn
        finish(acc2, l2)


def attention(q, k, v, ctx, *, nbatch, seq, heads, nmaps, lamv=None, subln=None, lam_init=0.0):
    t, w = q.shape
    tq = _pick_tile(seq, 512)
    nq = seq // tq
    qspec = pl.BlockSpec((tq, LANES), lambda b, h, i: (b * nq + i, h))
    kvspec = pl.BlockSpec((seq, LANES), lambda b, h, i: (b, h))
    in_specs, args = [], []
    if nmaps == 2:
        in_specs.append(pl.BlockSpec((SUBLANES, LANES), lambda b, h, i: (0, 0)))
        args.append(lamv)
    in_specs += [qspec, kvspec, kvspec]
    args += [q, k, v]
    if ctx is not None:
        past = ctx[0].shape[0] // nbatch
        cspec = pl.BlockSpec((past, LANES), lambda b, h, i: (b, h))
        in_specs += [cspec, cspec]
        args += list(ctx)
    if subln is not None:
        in_specs.append(pl.BlockSpec((1, LANES), lambda b, h, i: (0, 0)))
        args.append(subln)
    return pl.pallas_call(
        functools.partial(_attn_body, nmaps=nmaps, has_ctx=ctx is not None, subln=subln is not None,
                          tk=512, lam_init=lam_init),
        grid=(nbatch, heads, nq),
        in_specs=in_specs,
        out_specs=qspec,
        out_shape=jax.ShapeDtypeStruct((t, w), BF16),
        scratch_shapes=[pltpu.VMEM((SUBLANES, LANES), F32)],
        compiler_params=_cparams(3),
    )(*args)


def _merge_body(x_ref, f_ref, od_ref, om_ref, g0_ref, g1_ref, g2_ref, wf_ref, wd_ref, wm_ref, wo_ref,
                gate1_ref, nw_ref, sc_ref, sh_ref, wrh_ref, wrl_ref,
                x1_ref, h2p_ref, aff_ref):
    merged = g0_ref[...].astype(F32) * _dot(f_ref[...], wf_ref[...])
    merged = merged + g1_ref[...].astype(F32) * _dot(od_ref[...], wd_ref[...])
    merged = merged + g2_ref[...].astype(F32) * _dot(om_ref[...], wm_ref[...])
    x1 = x_ref[...] + gate1_ref[...] * _dot(merged.astype(BF16), wo_ref[...])
    x1_ref[...] = x1
    h2 = _norm_mod(x1, nw_ref[...], sc_ref[...], sh_ref[...])
    _store_token_tiles(h2p_ref, _pack_bf16_pairs(h2))
    h_hi = h2.astype(jnp.bfloat16)
    h_lo = (h2 - h_hi.astype(F32)).astype(BF16)
    logits = _dot(h_hi, wrh_ref[...]) + _dot(h_lo, wrh_ref[...]) + _dot(h_hi, wrl_ref[...])
    lane = lax.broadcasted_iota(I32, (1, LANES), 1)
    logits = jnp.where(lane < N_EXPERTS, logits, -1e30)
    e = jnp.exp(logits - jnp.max(logits, axis=-1, keepdims=True))
    aff = e / jnp.sum(e, axis=-1, keepdims=True)
    aff_ref[...] = aff.T[:N_EXPERTS, :]


def merge_out(x, f, od, om, gates, w, gate1, nw, sc, sh, seq):
    t, d = x.shape
    tm = _pick_tile(seq, 256)
    nchunk = d // 2 // LANES
    row = lambda i: (i, 0)
    const = lambda i: (0, 0)
    once = pl.Buffered(1)
    mod_spec = pl.BlockSpec((None, 1, d), _mod_index(sc.shape[0], seq // tm))
    wspec = lambda a: pl.BlockSpec(a.shape, const, pipeline_mode=once)
    in_specs = [pl.BlockSpec((tm, d), row),
                pl.BlockSpec((tm, f.shape[1]), row), pl.BlockSpec((tm, od.shape[1]), row),
                pl.BlockSpec((tm, om.shape[1]), row),
                pl.BlockSpec((tm, d), lambda i: (i, 0)), pl.BlockSpec((tm, d), lambda i: (i, 1)),
                pl.BlockSpec((tm, d), lambda i: (i, 2)),
                wspec(w["br_f"]), wspec(w["br_d"]), wspec(w["br_m"]), wspec(w["out"]),
                mod_spec, pl.BlockSpec((1, d), const), mod_spec, mod_spec,
                wspec(w["router_hi"]), wspec(w["router_lo"])]
    return pl.pallas_call(
        _merge_body,
        grid=(t // tm,),
        in_specs=in_specs,
        out_specs=[pl.BlockSpec((tm, d), row),
                   pl.BlockSpec((tm * nchunk, LANES), row),
                   pl.BlockSpec((N_EXPERTS, tm), lambda i: (0, i))],
        out_shape=[jax.ShapeDtypeStruct((t, d), F32),
                   jax.ShapeDtypeStruct((t * nchunk, LANES), U32),
                   jax.ShapeDtypeStruct((N_EXPERTS, t), F32)],
        compiler_params=_cparams(1),
    )(x, f, od, om, gates, gates, gates, w["br_f"], w["br_d"], w["br_m"], w["out"],
      gate1, nw, sc, sh, w["router_hi"], w["router_lo"])


def _cumsum_lanes(x):
    n = x.shape[1]
    lane = lax.broadcasted_iota(I32, (1, n), 1)
    s = 1
    while s < n:
        x = x + jnp.where(lane >= s, pltpu.roll(x, s, 1), 0)
        s *= 2
    return x


def _route_body(aff_ref, idx_ref, dst_ref, gv_ref, seg_ref, key_scr, dst_scr, *, cap, row_chunk):
    b = pl.program_id(0)
    aff = aff_ref[...]
    ne, n = aff.shape
    bits = pltpu.bitcast(aff, I32)

    def search(i, thr):
        cand = thr | (1 << (30 - i))
        cnt = jnp.sum(jnp.where(bits >= cand, 1.0, 0.0), axis=-1, keepdims=True)
        return jnp.where(cnt >= cap, cand, thr)

    thr = lax.fori_loop(0, 31, search, jnp.zeros((ne, 1), I32))
    gt = bits > thr
    eq = (bits == thr).astype(I32)
    need = cap - jnp.sum(jnp.where(gt, 1.0, 0.0), axis=-1, keepdims=True).astype(I32)
    eq_before = _cumsum_lanes(eq) - eq
    sel = jnp.where(gt | ((eq > 0) & (eq_before < need)), 1, 0)
    cum = _cumsum_lanes(sel)
    key_scr[...] = sel * cum

    before = jnp.zeros((1, n), I32)
    for e in range(ne):
        dst_scr[e:e + 1, :] = before
        before = before + sel[e:e + 1, :]
    k_tok = before
    start = _cumsum_lanes(k_tok) - k_tok + b * (ne * cap)
    dst_scr[...] = dst_scr[...] + start
    seg_ref[0:1, :] = start
    seg_ref[1:2, :] = start + k_tok
    seg_ref[2:SUBLANES, :] = jnp.zeros((SUBLANES - 2, n), I32)

    tok = (lax.broadcasted_iota(I32, (1, n), 1) + b * n).astype(F32)
    lane = lax.broadcasted_iota(I32, (1, LANES), 1)
    idx_ref[...] = jnp.zeros(idx_ref.shape, I32)
    dst_ref[...] = jnp.zeros(dst_ref.shape, I32)
    gv_ref[...] = jnp.zeros(gv_ref.shape, F32)

    def per_expert(e, carry):
        key = key_scr[pl.ds(e, 1), :]
        dst = dst_scr[pl.ds(e, 1), :].astype(F32)
        af = aff_ref[pl.ds(e, 1), :]
        for c0 in range(0, cap, row_chunk):
            slot = lax.broadcasted_iota(I32, (row_chunk, 1), 0) + (c0 + 1)
            hit = key == slot
            i_col = jnp.sum(jnp.where(hit, tok, 0.0), axis=-1, keepdims=True).astype(I32)
            d_col = jnp.sum(jnp.where(hit, dst, 0.0), axis=-1, keepdims=True).astype(I32)
            g_col = jnp.sum(jnp.where(hit, af, 0.0), axis=-1, keepdims=True)
            rows = pl.ds(c0, row_chunk)
            idx_ref[rows, :] = jnp.where(lane == e, i_col, idx_ref[rows, :])
            dst_ref[rows, :] = jnp.where(lane == e, d_col, dst_ref[rows, :])
            gv_ref[rows, :] = jnp.where(lane == e, g_col, gv_ref[rows, :])
        return carry

    lax.fori_loop(0, ne, per_expert, 0)


def route(aff_t, nbatch, seq):
    ne = aff_t.shape[0]
    cap = EC_FACTOR * seq // ne
    row_chunk = min(cap, 32)
    tab = lambda dt: jax.ShapeDtypeStruct((nbatch, cap, LANES), dt)
    tspec = pl.BlockSpec((None, cap, LANES), lambda b: (b, 0, 0))
    return pl.pallas_call(
        functools.partial(_route_body, cap=cap, row_chunk=row_chunk),
        grid=(nbatch,),
        in_specs=[pl.BlockSpec((ne, seq), lambda b: (0, b))],
        out_specs=[tspec, tspec, tspec, pl.BlockSpec((None, SUBLANES, seq), lambda b: (b, 0, 0))],
        out_shape=[tab(I32), tab(I32), tab(F32), jax.ShapeDtypeStruct((nbatch, SUBLANES, seq), I32)],
        scratch_shapes=[pltpu.VMEM((ne, seq), I32), pltpu.VMEM((ne, seq), I32)],
        compiler_params=_cparams(1),
    )(aff_t)


DMA_UNROLL = 8


def _moe_body(idx_a, idx_b, idx_a_next, dst_a, dst_b, gv_a, gv_b, h2p_ref, wg_ref, wu_ref, wd_ref, g_ref,
              xa, xb, ya, yb, sem, *, rt, d):
    nw = d // 2 // LANES
    step = pl.program_id(0) * pl.num_programs(1) + pl.program_id(1)
    last = pl.num_programs(0) * pl.num_programs(1) - 1
    in_a, in_b, out_a, out_b = (sem.at[i] for i in range(4))

    def row_tile(r):
        return pl.ds(r * nw if isinstance(r, int) else pl.multiple_of(r * nw, nw), nw)

    def looped(fn):
        def body(i, carry):
            for u in range(DMA_UNROLL):
                fn(i * DMA_UNROLL + u)
            return carry
        lax.fori_loop(0, rt // DMA_UNROLL, body, 0)

    def inline(fn):
        for r in range(rt):
            fn(r)

    def gather(idx_ref, r, xbuf, s):
        src = 0 if idx_ref is None else idx_ref[0, r]
        return pltpu.make_async_copy(h2p_ref.at[row_tile(src), :], xbuf.at[row_tile(r), :], s)

    def scatter(ybuf, r, dst_ref, s):
        dst = 0 if dst_ref is None else dst_ref[0, r]
        return pltpu.make_async_copy(ybuf.at[row_tile(r), :], g_ref.at[row_tile(dst), :], s)

    def drain_gather(xbuf, s):
        looped(lambda r: gather(None, r, xbuf, s).wait())

    def drain_scatter(ybuf, s):
        looped(lambda r: scatter(ybuf, r, None, s).wait())

    def ffn(xbuf, gv_ref):
        x = _unpack_rows(lambda c: xbuf[pl.ds(c, rt, stride=nw), :], nw)
        a = _dot(x, wg_ref[...])
        u = _dot(x, wu_ref[...])
        mid = (a * jax.nn.sigmoid(a) * u).astype(BF16)
        return _pack_bf16_pairs(_dot(mid, wd_ref[...]) * gv_ref[...])

    @pl.when(step == 0)
    def _():
        looped(lambda r: gather(idx_a, r, xa, in_a).start())

    drain_gather(xa, in_a)
    inline(lambda r: gather(idx_b, r, xb, in_b).start())
    y = ffn(xa, gv_a)

    @pl.when(step > 0)
    def _():
        drain_scatter(ya, out_a)

    _store_token_tiles(ya, y)

    drain_gather(xb, in_b)
    inline(lambda r: gather(idx_a_next, r, xa, in_a).start())
    inline(lambda r: scatter(ya, r, dst_a, out_a).start())
    y = ffn(xb, gv_b)

    @pl.when(step > 0)
    def _():
        drain_scatter(yb, out_b)

    _store_token_tiles(yb, y)
    looped(lambda r: scatter(yb, r, dst_b, out_b).start())

    @pl.when(step == last)
    def _():
        drain_gather(xa, in_a)
        drain_scatter(ya, out_a)
        drain_scatter(yb, out_b)


def moe_experts(idx, dst, gv, h2p, wg, wu, wd, d):
    ne, _, ff = wg.shape
    rt = idx.shape[2]
    ntiles = idx.shape[0]
    pairs = ntiles // ne // 2
    nw = d // 2 // LANES
    tile_a = lambda e, j: 2 * (e * pairs + j)
    smem = lambda tile: pl.BlockSpec((None, 1, rt), lambda e, j: (tile(e, j), 0, 0), memory_space=pltpu.SMEM)
    vmem_col = lambda tile: pl.BlockSpec((rt, 1), lambda e, j: (tile(e, j), 0))
    tile_b = lambda e, j: tile_a(e, j) + 1
    tile_a_next = lambda e, j: jnp.minimum(tile_a(e, j) + 2, ntiles - 2)
    wspec = lambda a: pl.BlockSpec((None,) + a.shape[1:], lambda e, j: (e, 0, 0))
    row_buf = pltpu.VMEM((rt * nw, LANES), U32)
    return pl.pallas_call(
        functools.partial(_moe_body, rt=rt, d=d),
        grid=(ne, pairs),
        in_specs=[smem(tile_a), smem(tile_b), smem(tile_a_next), smem(tile_a), smem(tile_b),
                  vmem_col(tile_a), vmem_col(tile_b),
                  pl.BlockSpec(memory_space=pl.ANY), wspec(wg), wspec(wu), wspec(wd)],
        out_specs=pl.BlockSpec(memory_space=pl.ANY),
        out_shape=jax.ShapeDtypeStruct((ntiles * rt * nw, LANES), U32),
        scratch_shapes=[row_buf, row_buf, row_buf, row_buf, pltpu.SemaphoreType.DMA((4,))],
        compiler_params=_cparams(2, has_side_effects=True),
    )(idx, idx, idx, dst, dst, gv, gv, h2p, wg, wu, wd)


COMBINE_WINDOW = 1024


def _combine_body(tb_ref, x_ref, s0_ref, s1_ref, gate_ref, g_ref, o_ref, gbuf, sem, *, win, d, npairs):
    t = pl.program_id(0)
    nw = d // 2 // LANES
    first = tb_ref[t]
    last = tb_ref[t + 1]
    s0 = s0_ref[...]
    s1 = s1_ref[...]

    def window_start(lo):
        return jnp.minimum(lo, npairs - win)

    def window_copy(lo, slot):
        rows = pl.ds(pl.multiple_of(window_start(lo) * nw, nw), win * nw)
        return pltpu.make_async_copy(g_ref.at[rows, :], gbuf.at[slot], sem.at[slot])

    def window_sum(lo, slot):
        pos = lax.broadcasted_iota(I32, (1, win), 1) + window_start(lo)
        own = (pos >= s0) & (pos < s1) & (pos >= lo)
        sel = jnp.where(own, 1.0, 0.0).astype(BF16)
        rows = _unpack_rows(lambda c: gbuf[slot, pl.ds(c, win, stride=nw), :], nw)
        return _dot(sel, rows)

    slot = t % 2

    @pl.when(t == 0)
    def _():
        window_copy(first, slot).start()

    window_copy(first, slot).wait()

    @pl.when(t + 1 < pl.num_programs(0))
    def _():
        window_copy(last, 1 - slot).start()

    def extra_window(wi, acc):
        lo = first + wi * win
        cp = window_copy(lo, 2)
        cp.start()
        cp.wait()
        return acc + window_sum(lo, 2)

    nwin = (last - first + win - 1) // win
    acc = lax.fori_loop(1, nwin, extra_window, window_sum(first, slot))
    o_ref[...] = x_ref[...] + gate_ref[...] * acc


def combine(x1, seg0, seg1, tile_bounds, gate2, g, seq):
    t, d = x1.shape
    tm = _pick_tile(seq, 256)
    nw = d // 2 // LANES
    npairs = g.shape[0] // nw
    win = min(COMBINE_WINDOW, npairs)
    row = lambda i, tb: (i, 0)
    grid_spec = pltpu.PrefetchScalarGridSpec(
        num_scalar_prefetch=1,
        grid=(t // tm,),
        in_specs=[pl.BlockSpec((tm, d), row), pl.BlockSpec((tm, 1), row), pl.BlockSpec((tm, 1), row),
                  pl.BlockSpec((None, 1, d), _mod_index(gate2.shape[0], seq // tm)),
                  pl.BlockSpec(memory_space=pl.ANY)],
        out_specs=pl.BlockSpec((tm, d), row),
        scratch_shapes=[pltpu.VMEM((3, win * nw, LANES), U32), pltpu.SemaphoreType.DMA((3,))],
    )
    return pl.pallas_call(
        functools.partial(_combine_body, win=win, d=d, npairs=npairs),
        grid_spec=grid_spec,
        out_shape=jax.ShapeDtypeStruct((t, d), F32),
        compiler_params=_cparams(1),
    )(tile_bounds, x1, seg0, seg1, gate2, g)


def _dft_channel_table():
    k = np.arange(FOURIER_CH)
    ang = 2.0 * np.pi * ((k[:, None] * k[None, :]) % FOURIER_CH) / FOURIER_CH
    return jnp.asarray(np.concatenate([np.cos(ang), np.sin(ang)], axis=1), BF16)


def _dft_position_tables(seq):
    r = 1 << ((seq.bit_length() - 1) // 2)
    n = jnp.arange(seq, dtype=I32)[None, :]
    k1 = jnp.arange(seq // r, dtype=I32)[:, None] * r
    k2 = jnp.arange(r, dtype=I32)[:, None]
    ang = lambda k: ((k * n) % seq).astype(F32) * (2.0 * math.pi / seq)
    a, b = ang(k1), ang(k2)
    ca, sa = jnp.cos(a)[:, None, :], jnp.sin(a)[:, None, :]
    cb, sb = jnp.cos(b)[None, :, :], jnp.sin(b)[None, :, :]
    scale = 1.0 / math.sqrt(seq * FOURIER_CH)
    cos = ((ca * cb - sa * sb) * scale).astype(BF16).reshape(seq, seq)
    sin = ((sa * cb + ca * sb) * scale).astype(BF16).reshape(seq, seq)
    return cos, sin


def _rope_pattern(seq, width):
    nf = width // 4
    pos = np.arange(seq)
    inv = ROPE_BASE ** (-np.arange(nf, dtype=np.float64) / nf)
    ar = (pos // GRID_W)[:, None] * inv
    ac = (pos % GRID_W)[:, None] * inv
    cos = np.concatenate([np.cos(ar), np.cos(ar), np.cos(ac), np.cos(ac)], axis=1)
    sin = np.concatenate([-np.sin(ar), np.sin(ar), -np.sin(ac), np.sin(ac)], axis=1)
    return cos, sin


def _rope_tables(seq):
    cd, sd = _rope_pattern(seq, DIFF_HD)
    diff = (np.tile(cd, (1, 2 * DIFF_HEADS)), np.tile(sd, (1, 2 * DIFF_HEADS)))
    cm, sm = _rope_pattern(seq, MLA_ROPE)
    pad_r = MLA_HEAD_PAD - MLA_QK_HD
    cm = np.concatenate([np.ones((seq, MLA_NOPE)), cm, np.ones((seq, pad_r))], axis=1)
    sm = np.concatenate([np.zeros((seq, MLA_NOPE)), sm, np.zeros((seq, pad_r))], axis=1)
    mla = (cm, sm)
    as_f32 = lambda pair: tuple(jnp.asarray(a, F32) for a in pair)
    return as_f32(diff), as_f32(mla)


def _pad_heads(a, width):
    lead = a.shape[:-1]
    a = a.reshape(lead + (MLA_HEADS, width))
    a = jnp.pad(a, [(0, 0)] * len(lead) + [(0, 0), (0, MLA_HEAD_PAD - width)])
    return a.reshape(lead + (MLA_PAD_W,))


def _layer_weights(p, l):
    w_in = p["w_in"][l]
    c0 = FOURIER_W
    c1 = c0 + 3 * DIFF_W
    c2 = c1 + MLA_Q_LORA + MLA_KV_LORA
    c3 = c2 + MLA_ROPE
    kr_cols = jnp.pad(w_in[:, c2:c3], ((0, 0), (MLA_NOPE, LANES - MLA_QK_HD)))
    w_kvb = p["mla_w_kvb"][l].reshape(MLA_KV_LORA, MLA_HEADS, MLA_NOPE + MLA_V)
    router = jnp.pad(p["moe_w_router"][l], ((0, 0), (0, LANES - N_EXPERTS)))
    router_hi = router.astype(BF16)
    tile = lambda v, reps: jnp.tile(v, reps)[None, :].astype(F32)
    pad_gain = lambda v: jnp.pad(v, (0, MLA_HEAD_PAD - MLA_QK_HD))[None, :].astype(F32)
    w_qb = _pad_heads(p["mla_w_qb"][l], MLA_QK_HD)
    lane = np.arange(LANES)
    rot = (lane >= MLA_NOPE) & (lane < MLA_QK_HD)
    off = MLA_ROPE // 4
    partner = np.where(rot, np.where((lane & off) == 0, lane + off, lane - off), lane)
    w_qb_partner = ((w_qb.reshape(MLA_Q_LORA, MLA_HEADS, LANES) * pad_gain(p["mla_qnorm_w"][l]))[:, :, partner]
                    * jnp.asarray(rot, F32)).reshape(MLA_Q_LORA, MLA_PAD_W)
    return dict(
        fourier=w_in[:, :c0].astype(BF16),
        diff=w_in[:, c0:c1].astype(BF16),
        mla_in=jnp.concatenate([w_in[:, c1:c2], kr_cols], axis=1).astype(BF16),
        gates=w_in[:, c3:].astype(BF16),
        diff_qn=tile(p["diff_qnorm_w"][l], 2 * DIFF_HEADS),
        diff_kn=tile(p["diff_knorm_w"][l], 2 * DIFF_HEADS),
        subln=p["diff_subln_w"][l][None, :].astype(F32),
        lamv=jnp.pad(jnp.stack([p["diff_lambda_q1"][l], p["diff_lambda_k1"][l],
                                p["diff_lambda_q2"][l], p["diff_lambda_k2"][l]]).astype(F32),
                     ((0, SUBLANES - 4), (0, LANES - DIFF_HD))),
        mla=dict(
            qa_norm=p["mla_qa_norm_w"][l][None, :].astype(F32),
            w_qb=w_qb.astype(BF16),
            w_qb_rope=jnp.concatenate([w_qb, w_qb_partner], axis=1).astype(BF16),
            q_norm=pad_gain(p["mla_qnorm_w"][l]),
            kva_norm=p["mla_kva_norm_w"][l][None, :].astype(F32),
            w_k=_pad_heads(w_kvb[:, :, :MLA_NOPE].reshape(MLA_KV_LORA, -1), MLA_NOPE).astype(BF16),
            w_v=_pad_heads(w_kvb[:, :, MLA_NOPE:].reshape(MLA_KV_LORA, -1), MLA_V).astype(BF16),
            k_norm=pad_gain(p["mla_knorm_w"][l]),
        ),
        merge=dict(
            br_f=p["w_br_fourier"][l].astype(BF16),
            br_d=p["w_br_diff"][l].astype(BF16),
            br_m=jnp.pad(p["w_br_mla"][l].reshape(MLA_HEADS, MLA_V, -1),
                         ((0, 0), (0, MLA_HEAD_PAD - MLA_V), (0, 0))).reshape(MLA_PAD_W, -1).astype(BF16),
            out=p["w_out"][l].astype(BF16),
            router_hi=router_hi,
            router_lo=(router - router_hi.astype(F32)).astype(BF16),
        ),
        moe_gate=p["moe_w_gate"][l].astype(BF16),
        moe_up=p["moe_w_up"][l].astype(BF16),
        moe_down=p["moe_w_down"][l].astype(BF16),
        norm1=p["norm1_w"][l][None, :].astype(F32),
        norm2=p["norm2_w"][l][None, :].astype(F32),
    )


def _trunk_layer(x, mods, w, lam_init, nbatch, seq, tabs, ctx):
    t, d = x.shape
    sh1, sc1, g1, sh2, sc2, g2 = mods
    h, gates = norm_gates_proj(x, w["norm1"], sc1, sh1, w["gates"], seq)
    ab = fourier_channel(h, w["fourier"], tabs["dft_ch"])
    y_f = fourier_position(ab, tabs["dft_cos"], tabs["dft_sin"], seq)
    new_ctx = None
    if ctx is None:
        q_d, k_d, v_d, k_d32, v_d32 = diff_qkv(h, w["diff"], w["diff_qn"], w["diff_kn"], seq, None)
        q_m, k_m, v_m, ckv32, krb32 = mla_proj(h, w["mla_in"], w["mla"], seq, None)
        new_ctx = (k_d32, v_d32, ckv32, krb32[:, MLA_NOPE:MLA_QK_HD])
        ctx_d = ctx_m = None
    else:
        q_d, k_d, v_d = diff_qkv(h, w["diff"], w["diff_qn"], w["diff_kn"], seq, tabs["rope_diff"])
        q_m, k_m, v_m = mla_proj(h, w["mla_in"], w["mla"], seq, tabs["rope_mla"])
        kd_c, vd_c, ckv_c, kr_c = ctx
        ctx_d = (kd_c.astype(BF16), vd_c.astype(BF16))
        krb_c = jnp.pad(kr_c, ((0, 0), (MLA_NOPE, LANES - MLA_QK_HD)))
        ctx_m = mla_ctx_keys(ckv_c, krb_c, w["mla"])
    o_d = attention(q_d, k_d, v_d, ctx_d, nbatch=nbatch, seq=seq, heads=DIFF_HEADS, nmaps=2,
                    lamv=w["lamv"], subln=w["subln"], lam_init=lam_init)
    o_m = attention(q_m, k_m, v_m, ctx_m, nbatch=nbatch, seq=seq, heads=MLA_HEADS, nmaps=1)
    x1, h2p, aff_t = merge_out(x, y_f, o_d, o_m, gates, w["merge"], g1, w["norm2"], sc2, sh2, seq)
    idx, dst, gv, seg = route(aff_t, nbatch, seq)
    cap = idx.shape[1]
    rt = _pick_tile(nbatch * cap // 2, 512)
    by_expert = lambda a: jnp.transpose(a[:, :, :N_EXPERTS], (2, 0, 1))
    idx_e = by_expert(idx).reshape(-1, 1, rt)
    dst_e = by_expert(dst).reshape(-1, 1, rt)
    gv_e = by_expert(gv).reshape(-1, 1)
    g = moe_experts(idx_e, dst_e, gv_e, h2p, w["moe_gate"], w["moe_up"], w["moe_down"], d)
    seg0 = seg[:, 0, :].reshape(t, 1)
    seg1 = seg[:, 1, :].reshape(t, 1)
    tm = _pick_tile(seq, 256)
    npairs = nbatch * N_EXPERTS * cap
    tile_bounds = jnp.concatenate([seg0[::tm, 0], jnp.full((1,), npairs, I32)])
    x2 = combine(x1, seg0, seg1, tile_bounds, g2, g, seq)
    return x2, new_ctx


def kernel(x_prompt, x_sample, cache_diff_k, cache_diff_v, cache_mla_ckv, cache_mla_krope, c, c_ctx, w_ada, b_ada, norm1_w, norm2_w, w_in, diff_qnorm_w, diff_knorm_w, diff_lambda_q1, diff_lambda_k1, diff_lambda_q2, diff_lambda_k2, diff_subln_w, mla_qa_norm_w, mla_w_qb, mla_kva_norm_w, mla_w_kvb, mla_qnorm_w, mla_knorm_w, w_br_fourier, w_br_diff, w_br_mla, w_out, moe_w_router, moe_w_gate, moe_w_up, moe_w_down):
    params = dict(w_in=w_in, norm1_w=norm1_w, norm2_w=norm2_w,
                  diff_qnorm_w=diff_qnorm_w, diff_knorm_w=diff_knorm_w,
                  diff_lambda_q1=diff_lambda_q1, diff_lambda_k1=diff_lambda_k1,
                  diff_lambda_q2=diff_lambda_q2, diff_lambda_k2=diff_lambda_k2, diff_subln_w=diff_subln_w,
                  mla_qa_norm_w=mla_qa_norm_w, mla_w_qb=mla_w_qb, mla_kva_norm_w=mla_kva_norm_w,
                  mla_w_kvb=mla_w_kvb, mla_qnorm_w=mla_qnorm_w, mla_knorm_w=mla_knorm_w,
                  w_br_fourier=w_br_fourier, w_br_diff=w_br_diff, w_br_mla=w_br_mla, w_out=w_out,
                  moe_w_router=moe_w_router, moe_w_gate=moe_w_gate, moe_w_up=moe_w_up, moe_w_down=moe_w_down)
    bp, lp, d = x_prompt.shape
    bs, ls, _ = x_sample.shape
    depth = w_in.shape[0]
    past = cache_diff_k.shape[2]

    cond = jnp.concatenate([c, c_ctx[None, :], jnp.zeros((COND_ROWS - bs - 1, d), F32)], axis=0)
    mods = adaln(cond, w_ada, b_ada)

    dft_ch = _dft_channel_table()
    rope_diff, rope_mla = _rope_tables(ls)
    tabs_p = dict(dft_ch=dft_ch)
    tabs_p["dft_cos"], tabs_p["dft_sin"] = _dft_position_tables(lp)
    tabs_s = dict(dft_ch=dft_ch, rope_diff=rope_diff, rope_mla=rope_mla)
    tabs_s["dft_cos"], tabs_s["dft_sin"] = _dft_position_tables(ls)

    y_p = x_prompt.reshape(bp * lp, d)
    y_s = x_sample.reshape(bs * ls, d)
    new_ctx = []
    for l in range(depth):
        w = _layer_weights(params, l)
        lam_init = 0.8 - 0.6 * math.exp(-0.3 * l)
        m = mods[l].reshape(COND_ROWS, N_ADA, d)
        mods_s = [m[:bs, j][:, None, :] for j in range(N_ADA)]
        mods_p = [m[bs:bs + 1, j][:, None, :] for j in range(N_ADA)]
        y_p, ctx_l = _trunk_layer(y_p, mods_p, w, lam_init, bp, lp, tabs_p, None)
        new_ctx.append(ctx_l)
        ctx = (cache_diff_k[:, l].reshape(bs * past, DIFF_W), cache_diff_v[:, l].reshape(bs * past, DIFF_W),
               cache_mla_ckv[:, l].reshape(bs * past, MLA_KV_LORA), cache_mla_krope[:, l].reshape(bs * past, MLA_ROPE))
        y_s, _ = _trunk_layer(y_s, mods_s, w, lam_init, bs, ls, tabs_s, ctx)

    stack = lambda j, shape: jnp.stack([n[j].reshape((bp, lp) + shape) for n in new_ctx], axis=1)
    return (y_p.reshape(bp, lp, d), y_s.reshape(bs, ls, d),
            stack(0, (DIFF_HEADS, 2, DIFF_HD)), stack(1, (DIFF_HEADS, 2 * DIFF_HD)),
            stack(2, (MLA_KV_LORA,)), stack(3, (MLA_ROPE,)))
```

```python
import functools
import math

import jax
import jax.numpy as jnp
import numpy as np
from jax import lax
from jax.experimental import pallas as pl
from jax.experimental.pallas import tpu as pltpu

F32, BF16, I32, U32 = jnp.float32, jnp.bfloat16, jnp.int32, jnp.uint32

GRID_W = 64
ROPE_BASE = 10000.0
EPS = 1e-6
N_ADA = 6
FOURIER_GROUPS = 4
FOURIER_CH = 128
FOURIER_W = FOURIER_GROUPS * FOURIER_CH
DIFF_HEADS = 4
DIFF_HD = 64
DIFF_W = DIFF_HEADS * 2 * DIFF_HD
MLA_HEADS = 8
MLA_NOPE = 64
MLA_ROPE = 32
MLA_QK_HD = MLA_NOPE + MLA_ROPE
MLA_V = 64
MLA_Q_LORA = 384
MLA_KV_LORA = 256
N_EXPERTS = 16
EC_FACTOR = 2

LANES = 128
SUBLANES = 8
VMEM_LIMIT_BYTES = 56 * 1024 * 1024
LOG2E = math.log2(math.e)
MLA_HEAD_PAD = LANES
MLA_PAD_W = MLA_HEADS * MLA_HEAD_PAD
COND_ROWS = 16


def _cparams(n_axes, **kw):
    return pltpu.CompilerParams(dimension_semantics=("arbitrary",) * n_axes,
                                vmem_limit_bytes=VMEM_LIMIT_BYTES, **kw)


def _dot(a, b):
    return jnp.dot(a, b, preferred_element_type=F32)


def _dot_nt(a, b):
    return lax.dot_general(a, b, (((1,), (1,)), ((), ())), preferred_element_type=F32)


def _pick_tile(n, target):
    t = min(n, target)
    while n % t:
        t //= 2
    return t


def _pack_bf16_pairs(x):
    half = x.shape[1] // 2
    bits = pltpu.bitcast(x.astype(jnp.bfloat16).astype(F32), U32)
    return (bits[:, :half] >> 16) | bits[:, half:]


def _store_token_tiles(ref, packed):
    rows, w = packed.shape
    n = w // LANES
    for c in range(n):
        ref[pl.ds(c, rows, stride=n), :] = packed[:, c * LANES:(c + 1) * LANES]


def _unpack_rows(load_chunk, nchunk):
    lo, hi = [], []
    for c in range(nchunk):
        wds = load_chunk(c)
        lo.append(pltpu.bitcast(wds << 16, F32).astype(BF16))
        hi.append(pltpu.bitcast(wds & jnp.uint32(0xFFFF0000), F32).astype(BF16))
    return jnp.concatenate(lo + hi, axis=1)


def _adaln_body(c_ref, w_ref, b_ref, o_ref):
    c = c_ref[...]
    a = (c * jax.nn.sigmoid(c)).astype(BF16)
    o_ref[...] = _dot(a, w_ref[...].astype(BF16)) + b_ref[...]


def adaln(cond, w_ada, b_ada):
    depth, d, n = w_ada.shape
    tn = _pick_tile(n, 1024)
    return pl.pallas_call(
        _adaln_body,
        grid=(depth, n // tn),
        in_specs=[pl.BlockSpec((COND_ROWS, d), lambda l, j: (0, 0)),
                  pl.BlockSpec((None, d, tn), lambda l, j: (l, 0, j)),
                  pl.BlockSpec((None, 1, tn), lambda l, j: (l, 0, j))],
        out_specs=pl.BlockSpec((None, COND_ROWS, tn), lambda l, j: (l, 0, j)),
        out_shape=jax.ShapeDtypeStruct((depth, COND_ROWS, n), F32),
        compiler_params=_cparams(2),
    )(cond, w_ada, b_ada.reshape(depth, 1, n))


def _norm_mod(x, nw, sc, sh):
    r = lax.rsqrt(jnp.mean(x * x, axis=-1, keepdims=True) + EPS)
    return (x * r) * nw * (1.0 + sc) + sh


def _mod_index(nb, tiles_per_batch):
    if nb == 1:
        return lambda i, *_: (0, 0, 0)
    return lambda i, *_: (i // tiles_per_batch, 0, 0)


def _gates_body(x_ref, nw_ref, sc_ref, sh_ref, w_ref, h_ref, o_ref):
    @pl.when(pl.program_id(1) == 0)
    def _():
        h_ref[...] = _norm_mod(x_ref[...], nw_ref[...], sc_ref[...], sh_ref[...]).astype(BF16)

    o_ref[...] = jax.nn.sigmoid(_dot(h_ref[...], w_ref[...])).astype(BF16)


def norm_gates_proj(x, nw, sc, sh, w, seq):
    t, d = x.shape
    n = w.shape[1]
    nb = sc.shape[0]
    tm, tn = _pick_tile(seq if nb > 1 else t, 1024), _pick_tile(n, 1024)
    mod_spec = pl.BlockSpec((None, 1, d), _mod_index(nb, seq // tm))
    return pl.pallas_call(
        _gates_body,
        grid=(t // tm, n // tn),
        in_specs=[pl.BlockSpec((tm, d), lambda i, j: (i, 0)),
                  pl.BlockSpec((1, d), lambda i, j: (0, 0)),
                  mod_spec, mod_spec,
                  pl.BlockSpec((d, tn), lambda i, j: (0, j))],
        out_specs=[pl.BlockSpec((tm, d), lambda i, j: (i, 0)),
                   pl.BlockSpec((tm, tn), lambda i, j: (i, j))],
        out_shape=[jax.ShapeDtypeStruct((t, d), BF16), jax.ShapeDtypeStruct((t, n), BF16)],
        compiler_params=_cparams(2),
    )(x, nw, sc, sh, w)


def _fourier_ch_body(h_ref, w_ref, cs_ref, o_ref):
    u = _dot(h_ref[...], w_ref[...]).astype(BF16)
    for g in range(FOURIER_GROUPS):
        ab = _dot(u[:, g * FOURIER_CH:(g + 1) * FOURIER_CH], cs_ref[...])
        o_ref[:, g * FOURIER_CH:(g + 1) * FOURIER_CH] = ab[:, :FOURIER_CH].astype(BF16)
        o_ref[:, FOURIER_W + g * FOURIER_CH:FOURIER_W + (g + 1) * FOURIER_CH] = ab[:, FOURIER_CH:].astype(BF16)


def fourier_channel(h, w, cs):
    t, d = h.shape
    tm = _pick_tile(t, 1024)
    return pl.pallas_call(
        _fourier_ch_body,
        grid=(t // tm,),
        in_specs=[pl.BlockSpec((tm, d), lambda i: (i, 0)),
                  pl.BlockSpec((d, FOURIER_W), lambda i: (0, 0)),
                  pl.BlockSpec((FOURIER_CH, 2 * FOURIER_CH), lambda i: (0, 0))],
        out_specs=pl.BlockSpec((tm, 2 * FOURIER_W), lambda i: (i, 0)),
        out_shape=jax.ShapeDtypeStruct((t, 2 * FOURIER_W), BF16),
        compiler_params=_cparams(1),
    )(h, w, cs)


def _fourier_pos_body(c_ref, s_ref, ab_ref, o_ref):
    y = _dot(c_ref[...], ab_ref[:, :FOURIER_W]) - _dot(s_ref[...], ab_ref[:, FOURIER_W:])
    o_ref[...] = y.astype(BF16)


def fourier_position(ab, cpos, spos, seq):
    t = ab.shape[0]
    tr = _pick_tile(seq, 512)
    nr = seq // tr
    return pl.pallas_call(
        _fourier_pos_body,
        grid=(t // seq, nr),
        in_specs=[pl.BlockSpec((tr, seq), lambda b, r: (r, 0)),
                  pl.BlockSpec((tr, seq), lambda b, r: (r, 0)),
                  pl.BlockSpec((seq, 2 * FOURIER_W), lambda b, r: (b, 0))],
        out_specs=pl.BlockSpec((tr, FOURIER_W), lambda b, r: (b * nr + r, 0)),
        out_shape=jax.ShapeDtypeStruct((t, FOURIER_W), BF16),
        compiler_params=_cparams(2),
    )(cpos, spos, ab)


def _rope(x, cos, sin, off):
    w = x.shape[1]
    lane = lax.broadcasted_iota(I32, (1, w), 1)
    first = (lane & off) == 0
    partner = jnp.where(first, pltpu.roll(x, w - off, 1), pltpu.roll(x, off, 1))
    return x * cos + partner * sin


def _half_tile_norm(x):
    outs = []
    lane = lax.broadcasted_iota(I32, (1, LANES), 1)
    low = lane < DIFF_HD
    for j in range(x.shape[1] // LANES):
        seg = x[:, j * LANES:(j + 1) * LANES]
        sq = seg * seg
        s_lo = jnp.sum(jnp.where(low, sq, 0.0), axis=-1, keepdims=True)
        s_hi = jnp.sum(jnp.where(low, 0.0, sq), axis=-1, keepdims=True)
        ms = jnp.where(low, s_lo, s_hi) * (1.0 / DIFF_HD)
        outs.append(seg * lax.rsqrt(ms + EPS))
    return jnp.concatenate(outs, axis=1)


def _diff_qkv_body(*refs, rope):
    if rope:
        h_ref, w_ref, qn_ref, kn_ref, cos_ref, sin_ref, q_out, k_out, v_out = refs
    else:
        h_ref, w_ref, qn_ref, kn_ref, q_out, k_out, v_out, k32_out, v32_out = refs
    z = _dot(h_ref[...], w_ref[...])
    q = _half_tile_norm(z[:, :DIFF_W]) * qn_ref[...]
    k = _half_tile_norm(z[:, DIFF_W:2 * DIFF_W]) * kn_ref[...]
    v = z[:, 2 * DIFF_W:]
    if rope:
        q = _rope(q, cos_ref[...], sin_ref[...], DIFF_HD // 4)
        k = _rope(k, cos_ref[...], sin_ref[...], DIFF_HD // 4)
    else:
        k32_out[...] = k
        v32_out[...] = v
    q_out[...] = (q * (DIFF_HD ** -0.5 * LOG2E)).astype(BF16)
    k_out[...] = k.astype(BF16)
    v_out[...] = v.astype(BF16)


def diff_qkv(h, w, qn, kn, seq, rope_tabs):
    t, d = h.shape
    rope = rope_tabs is not None
    tm = _pick_tile(seq, 512)
    row = lambda i: (i, 0)
    const = lambda i: (0, 0)
    in_specs = [pl.BlockSpec((tm, d), row), pl.BlockSpec((d, 3 * DIFF_W), const),
                pl.BlockSpec((1, DIFF_W), const), pl.BlockSpec((1, DIFF_W), const)]
    args = [h, w, qn, kn]
    out_shape = [jax.ShapeDtypeStruct((t, DIFF_W), BF16)] * 3
    if rope:
        nt = seq // tm
        tab = pl.BlockSpec((tm, DIFF_W), lambda i: (i % nt, 0))
        in_specs += [tab, tab]
        args += list(rope_tabs)
    else:
        out_shape = out_shape + [jax.ShapeDtypeStruct((t, DIFF_W), F32)] * 2
    return pl.pallas_call(
        functools.partial(_diff_qkv_body, rope=rope),
        grid=(t // tm,),
        in_specs=in_specs,
        out_specs=[pl.BlockSpec((tm, DIFF_W), row)] * len(out_shape),
        out_shape=out_shape,
        compiler_params=_cparams(1),
    )(*args)


def _mla_keys(ckv_bf, krb, wk_ref, wv_ref, kn_ref, rope_tab):
    kn = _dot(ckv_bf, wk_ref[...])
    gain = kn_ref[...]
    kr = krb * gain
    if rope_tab is not None:
        kr = _rope(kr, rope_tab[0], rope_tab[1], MLA_ROPE // 4)
    kr_sq = jnp.sum(krb * krb, axis=-1, keepdims=True)
    outs = []
    for j in range(MLA_HEADS):
        seg = kn[:, j * LANES:(j + 1) * LANES]
        ms = (jnp.sum(seg * seg, axis=-1, keepdims=True) + kr_sq) * (1.0 / MLA_QK_HD)
        outs.append((seg * gain + kr) * lax.rsqrt(ms + EPS))
    return jnp.concatenate(outs, axis=1), _dot(ckv_bf, wv_ref[...])


def _mla_proj_body(*refs, rope):
    (h_ref, w_ref, qan_ref, wqb_ref, qn_ref, kvan_ref, wk_ref, wv_ref, kn_ref), refs = refs[:9], refs[9:]
    if rope:
        cos_ref, sin_ref, q_out, k_out, v_out = refs
        rope_tab = (cos_ref[...], sin_ref[...])
    else:
        q_out, k_out, v_out, ckv32_out, kr32_out = refs
        rope_tab = None
    z = _dot(h_ref[...], w_ref[...])
    q_a = z[:, :MLA_Q_LORA]
    kv_a = z[:, MLA_Q_LORA:MLA_Q_LORA + MLA_KV_LORA]
    krb = z[:, MLA_Q_LORA + MLA_KV_LORA:]
    c_q = q_a * lax.rsqrt(jnp.mean(q_a * q_a, axis=-1, keepdims=True) + EPS) * qan_ref[...]
    zq = _dot(c_q.astype(BF16), wqb_ref[...])
    gain = qn_ref[...]
    outs = []
    for j in range(MLA_HEADS):
        seg = zq[:, j * LANES:(j + 1) * LANES]
        ms = jnp.sum(seg * seg, axis=-1, keepdims=True) * (1.0 / MLA_QK_HD)
        val = seg * gain
        if rope:
            val = val * rope_tab[0] + zq[:, MLA_PAD_W + j * LANES:MLA_PAD_W + (j + 1) * LANES] * rope_tab[1]
        outs.append(val * (lax.rsqrt(ms + EPS) * (MLA_QK_HD ** -0.5 * LOG2E)))
    q_out[...] = jnp.concatenate(outs, axis=1).astype(BF16)
    c_kv = kv_a * lax.rsqrt(jnp.mean(kv_a * kv_a, axis=-1, keepdims=True) + EPS) * kvan_ref[...]
    k, v = _mla_keys(c_kv.astype(BF16), krb, wk_ref, wv_ref, kn_ref, rope_tab)
    if not rope:
        ckv32_out[...] = c_kv
        kr32_out[...] = krb
    k_out[...] = k.astype(BF16)
    v_out[...] = v.astype(BF16)


def mla_proj(h, w, p, seq, rope_tabs):
    t, d = h.shape
    rope = rope_tabs is not None
    tm = _pick_tile(seq, 512)
    row = lambda i: (i, 0)
    const = lambda i: (0, 0)
    wcols = MLA_Q_LORA + MLA_KV_LORA + LANES
    w_qb = p["w_qb_rope"] if rope else p["w_qb"]
    in_specs = [pl.BlockSpec((tm, d), row), pl.BlockSpec((d, wcols), const),
                pl.BlockSpec((1, MLA_Q_LORA), const), pl.BlockSpec(w_qb.shape, const),
                pl.BlockSpec((1, LANES), const), pl.BlockSpec((1, MLA_KV_LORA), const),
                pl.BlockSpec((MLA_KV_LORA, MLA_PAD_W), const), pl.BlockSpec((MLA_KV_LORA, MLA_PAD_W), const),
                pl.BlockSpec((1, LANES), const)]
    args = [h, w, p["qa_norm"], w_qb, p["q_norm"], p["kva_norm"], p["w_k"], p["w_v"], p["k_norm"]]
    out_shape = [jax.ShapeDtypeStruct((t, MLA_PAD_W), BF16)] * 3
    out_specs = [pl.BlockSpec((tm, MLA_PAD_W), row)] * 3
    if rope:
        nt = seq // tm
        tab = pl.BlockSpec((tm, LANES), lambda i: (i % nt, 0))
        in_specs += [tab, tab]
        args += list(rope_tabs)
    else:
        out_shape += [jax.ShapeDtypeStruct((t, MLA_KV_LORA), F32), jax.ShapeDtypeStruct((t, LANES), F32)]
        out_specs += [pl.BlockSpec((tm, MLA_KV_LORA), row), pl.BlockSpec((tm, LANES), row)]
    return pl.pallas_call(
        functools.partial(_mla_proj_body, rope=rope),
        grid=(t // tm,),
        in_specs=in_specs,
        out_specs=out_specs,
        out_shape=out_shape,
        compiler_params=_cparams(1),
    )(*args)


def _mla_ctx_body(ckv_ref, krb_ref, wk_ref, wv_ref, kn_ref, k_out, v_out):
    k, v = _mla_keys(ckv_ref[...].astype(BF16), krb_ref[...], wk_ref, wv_ref, kn_ref, None)
    k_out[...] = k.astype(BF16)
    v_out[...] = v.astype(BF16)


def mla_ctx_keys(ckv, krb, p):
    t = ckv.shape[0]
    tm = _pick_tile(t, 512)
    row = lambda i: (i, 0)
    const = lambda i: (0, 0)
    return pl.pallas_call(
        _mla_ctx_body,
        grid=(t // tm,),
        in_specs=[pl.BlockSpec((tm, MLA_KV_LORA), row), pl.BlockSpec((tm, LANES), row),
                  pl.BlockSpec((MLA_KV_LORA, MLA_PAD_W), const), pl.BlockSpec((MLA_KV_LORA, MLA_PAD_W), const),
                  pl.BlockSpec((1, LANES), const)],
        out_specs=[pl.BlockSpec((tm, MLA_PAD_W), row)] * 2,
        out_shape=[jax.ShapeDtypeStruct((t, MLA_PAD_W), BF16)] * 2,
        compiler_params=_cparams(1),
    )(ckv, krb, p["w_k"], p["w_v"], p["k_norm"])


SOFTMAX_SUM_FLOOR = 2.0 ** -100


def _attn_body(*refs, nmaps, hp, has_ctx, subln, compact, tk, lam_init):
    refs = list(refs)
    kmax_scr = refs.pop()
    lamv_ref = refs.pop(0) if nmaps == 2 else None
    q_ref, k_ref, v_ref = refs[:3]
    refs = refs[3:]
    segments = []
    if has_ctx:
        segments.append((refs[0], refs[1]))
        refs = refs[2:]
    segments.append((k_ref, v_ref))
    subln_ref = refs.pop(0) if subln else None
    o_ref = refs[0]
    tq = q_ref.shape[0]
    lane = lax.broadcasted_iota(I32, (1, LANES), 1)
    masks = [lane < DIFF_HD, lane >= DIFF_HD] if nmaps == 2 else [None]

    def cols(hh):
        return slice(hh * LANES, (hh + 1) * LANES)

    @pl.when(pl.program_id(2) == 0)
    def _():
        for hh in range(hp):
            for i, mk in enumerate(masks):
                best = jnp.zeros((1, 1), F32)
                for kr, _ in segments:
                    kf = kr[:, cols(hh)].astype(F32)
                    sq = kf * kf if mk is None else jnp.where(mk, kf * kf, 0.0)
                    best = jnp.maximum(best, jnp.max(jnp.sum(sq, axis=-1, keepdims=True), axis=0, keepdims=True))
                row = hh * nmaps + i
                kmax_scr[row:row + 1, :] = jnp.broadcast_to(best, (1, LANES))

    def queries(hh):
        q = q_ref[:, cols(hh)]
        return [q if mk is None else jnp.where(mk, q, jnp.zeros_like(q)) for mk in masks]

    def key_chunks(hh):
        for kr, vr in segments:
            n = kr.shape[0]
            for c0 in range(0, n, tk):
                c1 = min(n, c0 + tk)
                yield kr[c0:c1, cols(hh)], vr[c0:c1, cols(hh)]

    def shifted(hh):
        acc, l = [], []
        for i, qi in enumerate(queries(hh)):
            qf = qi.astype(F32)
            row = hh * nmaps + i
            shift = jnp.sqrt(jnp.sum(qf * qf, axis=-1, keepdims=True) * kmax_scr[row:row + 1, 0:1])
            a = jnp.zeros((tq, LANES), F32)
            li = jnp.zeros((tq, 1), F32)
            for kc, vc in key_chunks(hh):
                p = jnp.exp2(_dot_nt(qi, kc) - shift)
                li = li + jnp.sum(p, axis=-1, keepdims=True)
                a = a + _dot(p.astype(BF16), vc)
            acc.append(a)
            l.append(li)
        return acc, l

    def running_max(hh):
        qs = queries(hh)
        m = [jnp.full((tq, 1), -jnp.inf, F32) for _ in qs]
        l = [jnp.zeros((tq, 1), F32) for _ in qs]
        acc = [jnp.zeros((tq, LANES), F32) for _ in qs]
        for kc, vc in key_chunks(hh):
            for i, qi in enumerate(qs):
                s = _dot_nt(qi, kc)
                mn = jnp.maximum(m[i], jnp.max(s, axis=-1, keepdims=True))
                alpha = jnp.exp2(m[i] - mn)
                p = jnp.exp2(s - mn)
                l[i] = alpha * l[i] + jnp.sum(p, axis=-1, keepdims=True)
                acc[i] = alpha * acc[i] + _dot(p.astype(BF16), vc)
                m[i] = mn
        return acc, l

    def head_out(acc, l):
        o = acc[0] / l[0]
        if nmaps == 2:
            lv = lamv_ref[...]
            lam = (jnp.exp(jnp.sum(lv[0:1] * lv[1:2], axis=-1, keepdims=True))
                   - jnp.exp(jnp.sum(lv[2:3] * lv[3:4], axis=-1, keepdims=True)) + lam_init)
            o = o - lam * (acc[1] / l[1])
        if subln:
            o = o * lax.rsqrt(jnp.mean(o * o, axis=-1, keepdims=True) + EPS) * subln_ref[...] * (1.0 - lam_init)
        return o

    def write(outs):
        if compact:
            for j in range(hp // 2):
                pair = jnp.where(lane < LANES // 2, outs[2 * j], pltpu.roll(outs[2 * j + 1], LANES // 2, 1))
                o_ref[:, cols(j)] = pair.astype(BF16)
        else:
            for hh in range(hp):
                o_ref[:, cols(hh)] = outs[hh].astype(BF16)

    results = [shifted(hh) for hh in range(hp)]
    write([head_out(acc, l) for acc, l in results])

    lmin = functools.reduce(jnp.minimum, [jnp.min(li) for _, l in results for li in l])

    @pl.when(lmin < SOFTMAX_SUM_FLOOR)
    def _():
        write([head_out(*running_max(hh)) for hh in range(hp)])


def attention(q, k, v, ctx, *, nbatch, seq, heads, nmaps, compact=False, lamv=None, subln=None, lam_init=0.0):
    t, w = q.shape
    tq = _pick_tile(seq, 512)
    nq = seq // tq
    hp = heads if seq <= 512 else (2 if compact else 1)
    assert hp * nmaps <= SUBLANES and heads % hp == 0
    wo = hp * LANES // 2 if compact else hp * LANES
    qspec = pl.BlockSpec((tq, hp * LANES), lambda b, h, i: (b * nq + i, h))
    kvspec = pl.BlockSpec((seq, hp * LANES), lambda b, h, i: (b, h))
    in_specs, args = [], []
    if nmaps == 2:
        in_specs.append(pl.BlockSpec((SUBLANES, LANES), lambda b, h, i: (0, 0)))
        args.append(lamv)
    in_specs += [qspec, kvspec, kvspec]
    args += [q, k, v]
    if ctx is not None:
        past = ctx[0].shape[0] // nbatch
        cspec = pl.BlockSpec((past, hp * LANES), lambda b, h, i: (b, h))
        in_specs += [cspec, cspec]
        args += list(ctx)
    if subln is not None:
        in_specs.append(pl.BlockSpec((1, LANES), lambda b, h, i: (0, 0)))
        args.append(subln)
    return pl.pallas_call(
        functools.partial(_attn_body, nmaps=nmaps, hp=hp, has_ctx=ctx is not None, subln=subln is not None,
                          compact=compact, tk=512, lam_init=lam_init),
        grid=(nbatch, heads // hp, nq),
        in_specs=in_specs,
        out_specs=pl.BlockSpec((tq, wo), lambda b, h, i: (b * nq + i, h)),
        out_shape=jax.ShapeDtypeStruct((t, w // 2 if compact else w), BF16),
        scratch_shapes=[pltpu.VMEM((SUBLANES, LANES), F32)],
        compiler_params=_cparams(3),
    )(*args)


def _merge_body(x_ref, f_ref, od_ref, om_ref, g0_ref, g1_ref, g2_ref, wf_ref, wd_ref, wm_ref, wo_ref,
                gate1_ref, nw_ref, sc_ref, sh_ref, wrh_ref, wrl_ref,
                x1_ref, h2p_ref, aff_ref):
    merged = g0_ref[...].astype(F32) * _dot(f_ref[...], wf_ref[...])
    merged = merged + g1_ref[...].astype(F32) * _dot(od_ref[...], wd_ref[...])
    merged = merged + g2_ref[...].astype(F32) * _dot(om_ref[...], wm_ref[...])
    x1 = x_ref[...] + gate1_ref[...] * _dot(merged.astype(BF16), wo_ref[...])
    x1_ref[...] = x1
    h2 = _norm_mod(x1, nw_ref[...], sc_ref[...], sh_ref[...])
    _store_token_tiles(h2p_ref, _pack_bf16_pairs(h2))
    h_hi = h2.astype(jnp.bfloat16)
    h_lo = (h2 - h_hi.astype(F32)).astype(BF16)
    hi_terms = _dot(h_hi, wrl_ref[...])
    logits = hi_terms[:, :LANES] + hi_terms[:, LANES:] + _dot(h_lo, wrh_ref[...])
    lane = lax.broadcasted_iota(I32, (1, LANES), 1)
    logits = jnp.where(lane < N_EXPERTS, logits, -1e30)
    e = jnp.exp(logits - jnp.max(logits, axis=-1, keepdims=True))
    aff = e / jnp.sum(e, axis=-1, keepdims=True)
    aff_ref[...] = aff.T[:N_EXPERTS, :]


def merge_out(x, f, od, om, gates, w, gate1, nw, sc, sh, seq):
    t, d = x.shape
    tm = _pick_tile(seq, 256)
    nchunk = d // 2 // LANES
    row = lambda i: (i, 0)
    const = lambda i: (0, 0)
    once = pl.Buffered(1)
    mod_spec = pl.BlockSpec((None, 1, d), _mod_index(sc.shape[0], seq // tm))
    wspec = lambda a: pl.BlockSpec(a.shape, const, pipeline_mode=once)
    in_specs = [pl.BlockSpec((tm, d), row),
                pl.BlockSpec((tm, f.shape[1]), row), pl.BlockSpec((tm, od.shape[1]), row),
                pl.BlockSpec((tm, om.shape[1]), row),
                pl.BlockSpec((tm, d), lambda i: (i, 0)), pl.BlockSpec((tm, d), lambda i: (i, 1)),
                pl.BlockSpec((tm, d), lambda i: (i, 2)),
                wspec(w["br_f"]), wspec(w["br_d"]), wspec(w["br_m"]), wspec(w["out"]),
                mod_spec, pl.BlockSpec((1, d), const), mod_spec, mod_spec,
                wspec(w["router_hi"]), wspec(w["router_hi_lo"])]
    return pl.pallas_call(
        _merge_body,
        grid=(t // tm,),
        in_specs=in_specs,
        out_specs=[pl.BlockSpec((tm, d), row),
                   pl.BlockSpec((tm * nchunk, LANES), row),
                   pl.BlockSpec((N_EXPERTS, tm), lambda i: (0, i))],
        out_shape=[jax.ShapeDtypeStruct((t, d), F32),
                   jax.ShapeDtypeStruct((t * nchunk, LANES), U32),
                   jax.ShapeDtypeStruct((N_EXPERTS, t), F32)],
        compiler_params=_cparams(1),
    )(x, f, od, om, gates, gates, gates, w["br_f"], w["br_d"], w["br_m"], w["out"],
      gate1, nw, sc, sh, w["router_hi"], w["router_hi_lo"])


def _cumsum_lanes(x):
    n = x.shape[1]
    lane = lax.broadcasted_iota(I32, (1, n), 1)
    s = 1
    while s < n:
        x = x + jnp.where(lane >= s, pltpu.roll(x, s, 1), 0)
        s *= 2
    return x


def _route_body(aff_ref, idx_ref, dst_ref, gv_ref, seg_ref, key_scr, dst_scr, *, cap, row_chunk):
    b = pl.program_id(0)
    aff = aff_ref[...]
    ne, n = aff.shape
    bits = pltpu.bitcast(aff, I32)

    def search(i, thr):
        cand = thr | (1 << (30 - i))
        cnt = jnp.sum(jnp.where(bits >= cand, 1.0, 0.0), axis=-1, keepdims=True)
        return jnp.where(cnt >= cap, cand, thr)

    thr = lax.fori_loop(0, 31, search, jnp.zeros((ne, 1), I32))
    gt = bits > thr
    eq = (bits == thr).astype(I32)
    need = cap - jnp.sum(jnp.where(gt, 1.0, 0.0), axis=-1, keepdims=True).astype(I32)
    eq_before = _cumsum_lanes(eq) - eq
    sel = jnp.where(gt | ((eq > 0) & (eq_before < need)), 1, 0)
    cum = _cumsum_lanes(sel)
    key_scr[...] = sel * cum

    before = jnp.zeros((1, n), I32)
    for e in range(ne):
        dst_scr[e:e + 1, :] = before
        before = before + sel[e:e + 1, :]
    k_tok = before
    start = _cumsum_lanes(k_tok) - k_tok + b * (ne * cap)
    dst_scr[...] = dst_scr[...] + start
    seg_ref[0:1, :] = start
    seg_ref[1:2, :] = start + k_tok
    seg_ref[2:SUBLANES, :] = jnp.zeros((SUBLANES - 2, n), I32)

    tok = lax.broadcasted_iota(I32, (1, n), 1) + b * n
    lane = lax.broadcasted_iota(I32, (1, LANES), 1)
    idx_ref[...] = jnp.zeros(idx_ref.shape, I32)
    dst_ref[...] = jnp.zeros(dst_ref.shape, I32)
    gv_ref[...] = jnp.zeros(gv_ref.shape, F32)
    byte = lambda v, shift: ((v >> shift) & 255).astype(F32)

    def per_expert(e, carry):
        key = key_scr[pl.ds(e, 1), :]
        dst = dst_scr[pl.ds(e, 1), :]
        af = aff_ref[pl.ds(e, 1), :]
        a_hi = af.astype(jnp.bfloat16).astype(F32)
        a_mid = (af - a_hi).astype(jnp.bfloat16).astype(F32)
        a_lo = af - a_hi - a_mid
        vals = jnp.concatenate([byte(tok, 16), byte(tok, 8), byte(tok, 0),
                                byte(dst, 16), byte(dst, 8), byte(dst, 0),
                                a_hi, a_mid, a_lo] + [jnp.zeros((1, n), F32)] * 7, axis=0).astype(BF16)
        for c0 in range(0, cap, row_chunk):
            slot = lax.broadcasted_iota(I32, (row_chunk, 1), 0) + (c0 + 1)
            hit = jnp.where(key == slot, 1.0, 0.0).astype(BF16)
            got = _dot_nt(hit, vals)
            word = lambda j: (got[:, j:j + 1].astype(I32) * 65536 + got[:, j + 1:j + 2].astype(I32) * 256
                              + got[:, j + 2:j + 3].astype(I32))
            g_col = got[:, 6:7] + got[:, 7:8] + got[:, 8:9]
            rows = pl.ds(c0, row_chunk)
            idx_ref[rows, :] = jnp.where(lane == e, word(0), idx_ref[rows, :])
            dst_ref[rows, :] = jnp.where(lane == e, word(3), dst_ref[rows, :])
            gv_ref[rows, :] = jnp.where(lane == e, g_col, gv_ref[rows, :])
        return carry

    lax.fori_loop(0, ne, per_expert, 0)


def route(aff_t, nbatch, seq):
    ne = aff_t.shape[0]
    cap = EC_FACTOR * seq // ne
    row_chunk = min(cap, 256)
    tab = lambda dt: jax.ShapeDtypeStruct((nbatch, cap, LANES), dt)
    tspec = pl.BlockSpec((None, cap, LANES), lambda b: (b, 0, 0))
    return pl.pallas_call(
        functools.partial(_route_body, cap=cap, row_chunk=row_chunk),
        grid=(nbatch,),
        in_specs=[pl.BlockSpec((ne, seq), lambda b: (0, b))],
        out_specs=[tspec, tspec, tspec, pl.BlockSpec((None, SUBLANES, seq), lambda b: (b, 0, 0))],
        out_shape=[tab(I32), tab(I32), tab(F32), jax.ShapeDtypeStruct((nbatch, SUBLANES, seq), I32)],
        scratch_shapes=[pltpu.VMEM((ne, seq), I32), pltpu.VMEM((ne, seq), I32)],
        compiler_params=_cparams(1),
    )(aff_t)


DMA_UNROLL = 8


def _moe_body(idx_a, idx_b, idx_a_next, dst_a, dst_b, gv_a, gv_b, h2p_ref, wg_ref, wu_ref, wd_ref, g_ref,
              xa, xb, ya, yb, sem, *, rt, d):
    nw = d // 2 // LANES
    step = pl.program_id(0) * pl.num_programs(1) + pl.program_id(1)
    last = pl.num_programs(0) * pl.num_programs(1) - 1
    in_a, in_b, out_a, out_b = (sem.at[i] for i in range(4))

    def row_tile(r):
        return pl.ds(r * nw if isinstance(r, int) else pl.multiple_of(r * nw, nw), nw)

    def looped(fn):
        def body(i, carry):
            for u in range(DMA_UNROLL):
                fn(i * DMA_UNROLL + u)
            return carry
        lax.fori_loop(0, rt // DMA_UNROLL, body, 0)

    def inline(fn):
        for r in range(rt):
            fn(r)

    def gather(idx_ref, r, xbuf, s):
        src = 0 if idx_ref is None else idx_ref[0, r]
        return pltpu.make_async_copy(h2p_ref.at[row_tile(src), :], xbuf.at[row_tile(r), :], s)

    def scatter(ybuf, r, dst_ref, s):
        dst = 0 if dst_ref is None else dst_ref[0, r]
        return pltpu.make_async_copy(ybuf.at[row_tile(r), :], g_ref.at[row_tile(dst), :], s)

    def drain_gather(xbuf, s):
        looped(lambda r: gather(None, r, xbuf, s).wait())

    def drain_scatter(ybuf, s):
        looped(lambda r: scatter(ybuf, r, None, s).wait())

    def ffn(xbuf, gv_ref):
        x = _unpack_rows(lambda c: xbuf[pl.ds(c, rt, stride=nw), :], nw)
        a = _dot(x, wg_ref[...])
        u = _dot(x, wu_ref[...])
        mid = (a * jax.nn.sigmoid(a) * u).astype(BF16)
        return _pack_bf16_pairs(_dot(mid, wd_ref[...]) * gv_ref[...])

    @pl.when(step == 0)
    def _():
        looped(lambda r: gather(idx_a, r, xa, in_a).start())

    drain_gather(xa, in_a)
    inline(lambda r: gather(idx_b, r, xb, in_b).start())
    y = ffn(xa, gv_a)

    @pl.when(step > 0)
    def _():
        drain_scatter(ya, out_a)

    _store_token_tiles(ya, y)

    drain_gather(xb, in_b)
    inline(lambda r: gather(idx_a_next, r, xa, in_a).start())
    inline(lambda r: scatter(ya, r, dst_a, out_a).start())
    y = ffn(xb, gv_b)

    @pl.when(step > 0)
    def _():
        drain_scatter(yb, out_b)

    _store_token_tiles(yb, y)
    looped(lambda r: scatter(yb, r, dst_b, out_b).start())

    @pl.when(step == last)
    def _():
        drain_gather(xa, in_a)
        drain_scatter(ya, out_a)
        drain_scatter(yb, out_b)


def moe_experts(idx, dst, gv, h2p, wg, wu, wd, d):
    ne, _, ff = wg.shape
    rt = idx.shape[2]
    ntiles = idx.shape[0]
    pairs = ntiles // ne // 2
    nw = d // 2 // LANES
    tile_a = lambda e, j: 2 * (e * pairs + j)
    smem = lambda tile: pl.BlockSpec((None, 1, rt), lambda e, j: (tile(e, j), 0, 0), memory_space=pltpu.SMEM)
    vmem_col = lambda tile: pl.BlockSpec((rt, 1), lambda e, j: (tile(e, j), 0))
    tile_b = lambda e, j: tile_a(e, j) + 1
    tile_a_next = lambda e, j: jnp.minimum(tile_a(e, j) + 2, ntiles - 2)
    wspec = lambda a: pl.BlockSpec((None,) + a.shape[1:], lambda e, j: (e, 0, 0))
    row_buf = pltpu.VMEM((rt * nw, LANES), U32)
    return pl.pallas_call(
        functools.partial(_moe_body, rt=rt, d=d),
        grid=(ne, pairs),
        in_specs=[smem(tile_a), smem(tile_b), smem(tile_a_next), smem(tile_a), smem(tile_b),
                  vmem_col(tile_a), vmem_col(tile_b),
                  pl.BlockSpec(memory_space=pl.ANY), wspec(wg), wspec(wu), wspec(wd)],
        out_specs=pl.BlockSpec(memory_space=pl.ANY),
        out_shape=jax.ShapeDtypeStruct((ntiles * rt * nw, LANES), U32),
        scratch_shapes=[row_buf, row_buf, row_buf, row_buf, pltpu.SemaphoreType.DMA((4,))],
        compiler_params=_cparams(2, has_side_effects=True),
    )(idx, idx, idx, dst, dst, gv, gv, h2p, wg, wu, wd)


COMBINE_WINDOW = 1024


def _combine_body(tb_ref, x_ref, s0_ref, s1_ref, gate_ref, g_ref, o_ref, gbuf, sem, *, win, d, npairs):
    t = pl.program_id(0)
    nw = d // 2 // LANES
    first = tb_ref[t]
    last = tb_ref[t + 1]
    s0 = s0_ref[...]
    s1 = s1_ref[...]

    def window_start(lo):
        return jnp.minimum(lo, npairs - win)

    def window_copy(lo, slot):
        rows = pl.ds(pl.multiple_of(window_start(lo) * nw, nw), win * nw)
        return pltpu.make_async_copy(g_ref.at[rows, :], gbuf.at[slot], sem.at[slot])

    def window_sum(lo, slot):
        pos = lax.broadcasted_iota(I32, (1, win), 1) + window_start(lo)
        own = (pos >= s0) & (pos < s1) & (pos >= lo)
        sel = jnp.where(own, 1.0, 0.0).astype(BF16)
        rows = _unpack_rows(lambda c: gbuf[slot, pl.ds(c, win, stride=nw), :], nw)
        return _dot(sel, rows)

    slot = t % 2

    @pl.when(t == 0)
    def _():
        window_copy(first, slot).start()

    window_copy(first, slot).wait()

    @pl.when(t + 1 < pl.num_programs(0))
    def _():
        window_copy(last, 1 - slot).start()

    def extra_window(wi, acc):
        lo = first + wi * win
        cp = window_copy(lo, 2)
        cp.start()
        cp.wait()
        return acc + window_sum(lo, 2)

    nwin = (last - first + win - 1) // win
    acc = lax.fori_loop(1, nwin, extra_window, window_sum(first, slot))
    o_ref[...] = x_ref[...] + gate_ref[...] * acc


def combine(x1, seg0, seg1, tile_bounds, gate2, g, seq):
    t, d = x1.shape
    tm = _pick_tile(seq, 256)
    nw = d // 2 // LANES
    npairs = g.shape[0] // nw
    win = min(COMBINE_WINDOW, npairs)
    row = lambda i, tb: (i, 0)
    grid_spec = pltpu.PrefetchScalarGridSpec(
        num_scalar_prefetch=1,
        grid=(t // tm,),
        in_specs=[pl.BlockSpec((tm, d), row), pl.BlockSpec((tm, 1), row), pl.BlockSpec((tm, 1), row),
                  pl.BlockSpec((None, 1, d), _mod_index(gate2.shape[0], seq // tm)),
                  pl.BlockSpec(memory_space=pl.ANY)],
        out_specs=pl.BlockSpec((tm, d), row),
        scratch_shapes=[pltpu.VMEM((3, win * nw, LANES), U32), pltpu.SemaphoreType.DMA((3,))],
    )
    return pl.pallas_call(
        functools.partial(_combine_body, win=win, d=d, npairs=npairs),
        grid_spec=grid_spec,
        out_shape=jax.ShapeDtypeStruct((t, d), F32),
        compiler_params=_cparams(1),
    )(tile_bounds, x1, seg0, seg1, gate2, g)


def _dft_channel_table():
    k = np.arange(FOURIER_CH)
    ang = 2.0 * np.pi * ((k[:, None] * k[None, :]) % FOURIER_CH) / FOURIER_CH
    return jnp.asarray(np.concatenate([np.cos(ang), np.sin(ang)], axis=1), BF16)


def _dft_position_tables(seq):
    r = 1 << ((seq.bit_length() - 1) // 2)
    n = jnp.arange(seq, dtype=I32)[None, :]
    k1 = jnp.arange(seq // r, dtype=I32)[:, None] * r
    k2 = jnp.arange(r, dtype=I32)[:, None]
    ang = lambda k: ((k * n) % seq).astype(F32) * (2.0 * math.pi / seq)
    a, b = ang(k1), ang(k2)
    ca, sa = jnp.cos(a)[:, None, :], jnp.sin(a)[:, None, :]
    cb, sb = jnp.cos(b)[None, :, :], jnp.sin(b)[None, :, :]
    scale = 1.0 / math.sqrt(seq * FOURIER_CH)
    cos = ((ca * cb - sa * sb) * scale).astype(BF16).reshape(seq, seq)
    sin = ((sa * cb + ca * sb) * scale).astype(BF16).reshape(seq, seq)
    return cos, sin


def _rope_pattern(seq, width):
    nf = width // 4
    pos = np.arange(seq)
    inv = ROPE_BASE ** (-np.arange(nf, dtype=np.float64) / nf)
    ar = (pos // GRID_W)[:, None] * inv
    ac = (pos % GRID_W)[:, None] * inv
    cos = np.concatenate([np.cos(ar), np.cos(ar), np.cos(ac), np.cos(ac)], axis=1)
    sin = np.concatenate([-np.sin(ar), np.sin(ar), -np.sin(ac), np.sin(ac)], axis=1)
    return cos, sin


def _rope_tables(seq):
    cd, sd = _rope_pattern(seq, DIFF_HD)
    diff = (np.tile(cd, (1, 2 * DIFF_HEADS)), np.tile(sd, (1, 2 * DIFF_HEADS)))
    cm, sm = _rope_pattern(seq, MLA_ROPE)
    pad_r = MLA_HEAD_PAD - MLA_QK_HD
    cm = np.concatenate([np.ones((seq, MLA_NOPE)), cm, np.ones((seq, pad_r))], axis=1)
    sm = np.concatenate([np.zeros((seq, MLA_NOPE)), sm, np.zeros((seq, pad_r))], axis=1)
    mla = (cm, sm)
    as_f32 = lambda pair: tuple(jnp.asarray(a, F32) for a in pair)
    return as_f32(diff), as_f32(mla)


def _pad_heads(a, width):
    lead = a.shape[:-1]
    a = a.reshape(lead + (MLA_HEADS, width))
    a = jnp.pad(a, [(0, 0)] * len(lead) + [(0, 0), (0, MLA_HEAD_PAD - width)])
    return a.reshape(lead + (MLA_PAD_W,))


def _layer_weights(p, l):
    w_in = p["w_in"][l]
    c0 = FOURIER_W
    c1 = c0 + 3 * DIFF_W
    c2 = c1 + MLA_Q_LORA + MLA_KV_LORA
    c3 = c2 + MLA_ROPE
    kr_cols = jnp.pad(w_in[:, c2:c3], ((0, 0), (MLA_NOPE, LANES - MLA_QK_HD)))
    w_kvb = p["mla_w_kvb"][l].reshape(MLA_KV_LORA, MLA_HEADS, MLA_NOPE + MLA_V)
    router = jnp.pad(p["moe_w_router"][l], ((0, 0), (0, LANES - N_EXPERTS)))
    router_hi = router.astype(BF16)
    tile = lambda v, reps: jnp.tile(v, reps)[None, :].astype(F32)
    pad_gain = lambda v: jnp.pad(v, (0, MLA_HEAD_PAD - MLA_QK_HD))[None, :].astype(F32)
    w_qb = _pad_heads(p["mla_w_qb"][l], MLA_QK_HD)
    lane = np.arange(LANES)
    rot = (lane >= MLA_NOPE) & (lane < MLA_QK_HD)
    off = MLA_ROPE // 4
    partner = np.where(rot, np.where((lane & off) == 0, lane + off, lane - off), lane)
    w_qb_partner = ((w_qb.reshape(MLA_Q_LORA, MLA_HEADS, LANES) * pad_gain(p["mla_qnorm_w"][l]))[:, :, partner]
                    * jnp.asarray(rot, F32)).reshape(MLA_Q_LORA, MLA_PAD_W)
    return dict(
        fourier=w_in[:, :c0].astype(BF16),
        diff=w_in[:, c0:c1].astype(BF16),
        mla_in=jnp.concatenate([w_in[:, c1:c2], kr_cols], axis=1).astype(BF16),
        gates=w_in[:, c3:].astype(BF16),
        diff_qn=tile(p["diff_qnorm_w"][l], 2 * DIFF_HEADS),
        diff_kn=tile(p["diff_knorm_w"][l], 2 * DIFF_HEADS),
        subln=p["diff_subln_w"][l][None, :].astype(F32),
        lamv=jnp.pad(jnp.stack([p["diff_lambda_q1"][l], p["diff_lambda_k1"][l],
                                p["diff_lambda_q2"][l], p["diff_lambda_k2"][l]]).astype(F32),
                     ((0, SUBLANES - 4), (0, LANES - DIFF_HD))),
        mla=dict(
            qa_norm=p["mla_qa_norm_w"][l][None, :].astype(F32),
            w_qb=w_qb.astype(BF16),
            w_qb_rope=jnp.concatenate([w_qb, w_qb_partner], axis=1).astype(BF16),
            q_norm=pad_gain(p["mla_qnorm_w"][l]),
            kva_norm=p["mla_kva_norm_w"][l][None, :].astype(F32),
            w_k=_pad_heads(w_kvb[:, :, :MLA_NOPE].reshape(MLA_KV_LORA, -1), MLA_NOPE).astype(BF16),
            w_v=_pad_heads(w_kvb[:, :, MLA_NOPE:].reshape(MLA_KV_LORA, -1), MLA_V).astype(BF16),
            k_norm=pad_gain(p["mla_knorm_w"][l]),
        ),
        merge=dict(
            br_f=p["w_br_fourier"][l].astype(BF16),
            br_d=p["w_br_diff"][l].astype(BF16),
            br_m=p["w_br_mla"][l].astype(BF16),
            out=p["w_out"][l].astype(BF16),
            router_hi=router_hi,
            router_hi_lo=jnp.concatenate([router_hi, (router - router_hi.astype(F32)).astype(BF16)], axis=1),
        ),
        moe_gate=p["moe_w_gate"][l].astype(BF16),
        moe_up=p["moe_w_up"][l].astype(BF16),
        moe_down=p["moe_w_down"][l].astype(BF16),
        norm1=p["norm1_w"][l][None, :].astype(F32),
        norm2=p["norm2_w"][l][None, :].astype(F32),
    )


def _trunk_layer(x, mods, w, lam_init, nbatch, seq, tabs, ctx):
    t, d = x.shape
    sh1, sc1, g1, sh2, sc2, g2 = mods
    h, gates = norm_gates_proj(x, w["norm1"], sc1, sh1, w["gates"], seq)
    ab = fourier_channel(h, w["fourier"], tabs["dft_ch"])
    y_f = fourier_position(ab, tabs["dft_cos"], tabs["dft_sin"], seq)
    new_ctx = None
    if ctx is None:
        q_d, k_d, v_d, k_d32, v_d32 = diff_qkv(h, w["diff"], w["diff_qn"], w["diff_kn"], seq, None)
        q_m, k_m, v_m, ckv32, krb32 = mla_proj(h, w["mla_in"], w["mla"], seq, None)
        new_ctx = (k_d32, v_d32, ckv32, krb32[:, MLA_NOPE:MLA_QK_HD])
        ctx_d = ctx_m = None
    else:
        q_d, k_d, v_d = diff_qkv(h, w["diff"], w["diff_qn"], w["diff_kn"], seq, tabs["rope_diff"])
        q_m, k_m, v_m = mla_proj(h, w["mla_in"], w["mla"], seq, tabs["rope_mla"])
        kd_c, vd_c, ckv_c, kr_c = ctx
        ctx_d = (kd_c.astype(BF16), vd_c.astype(BF16))
        krb_c = jnp.pad(kr_c, ((0, 0), (MLA_NOPE, LANES - MLA_QK_HD)))
        ctx_m = mla_ctx_keys(ckv_c, krb_c, w["mla"])
    o_d = attention(q_d, k_d, v_d, ctx_d, nbatch=nbatch, seq=seq, heads=DIFF_HEADS, nmaps=2,
                    lamv=w["lamv"], subln=w["subln"], lam_init=lam_init)
    o_m = attention(q_m, k_m, v_m, ctx_m, nbatch=nbatch, seq=seq, heads=MLA_HEADS, nmaps=1, compact=True)
    x1, h2p, aff_t = merge_out(x, y_f, o_d, o_m, gates, w["merge"], g1, w["norm2"], sc2, sh2, seq)
    idx, dst, gv, seg = route(aff_t, nbatch, seq)
    cap = idx.shape[1]
    rt = _pick_tile(nbatch * cap // 2, 512)
    by_expert = lambda a: jnp.transpose(a[:, :, :N_EXPERTS], (2, 0, 1))
    idx_e = by_expert(idx).reshape(-1, 1, rt)
    dst_e = by_expert(dst).reshape(-1, 1, rt)
    gv_e = by_expert(gv).reshape(-1, 1)
    g = moe_experts(idx_e, dst_e, gv_e, h2p, w["moe_gate"], w["moe_up"], w["moe_down"], d)
    seg0 = seg[:, 0, :].reshape(t, 1)
    seg1 = seg[:, 1, :].reshape(t, 1)
    tm = _pick_tile(seq, 256)
    npairs = nbatch * N_EXPERTS * cap
    tile_bounds = jnp.concatenate([seg0[::tm, 0], jnp.full((1,), npairs, I32)])
    x2 = combine(x1, seg0, seg1, tile_bounds, g2, g, seq)
    return x2, new_ctx


def kernel(x_prompt, x_sample, cache_diff_k, cache_diff_v, cache_mla_ckv, cache_mla_krope, c, c_ctx, w_ada, b_ada, norm1_w, norm2_w, w_in, diff_qnorm_w, diff_knorm_w, diff_lambda_q1, diff_lambda_k1, diff_lambda_q2, diff_lambda_k2, diff_subln_w, mla_qa_norm_w, mla_w_qb, mla_kva_norm_w, mla_w_kvb, mla_qnorm_w, mla_knorm_w, w_br_fourier, w_br_diff, w_br_mla, w_out, moe_w_router, moe_w_gate, moe_w_up, moe_w_down):
    params = dict(w_in=w_in, norm1_w=norm1_w, norm2_w=norm2_w,
                  diff_qnorm_w=diff_qnorm_w, diff_knorm_w=diff_knorm_w,
                  diff_lambda_q1=diff_lambda_q1, diff_lambda_k1=diff_lambda_k1,
                  diff_lambda_q2=diff_lambda_q2, diff_lambda_k2=diff_lambda_k2, diff_subln_w=diff_subln_w,
                  mla_qa_norm_w=mla_qa_norm_w, mla_w_qb=mla_w_qb, mla_kva_norm_w=mla_kva_norm_w,
                  mla_w_kvb=mla_w_kvb, mla_qnorm_w=mla_qnorm_w, mla_knorm_w=mla_knorm_w,
                  w_br_fourier=w_br_fourier, w_br_diff=w_br_diff, w_br_mla=w_br_mla, w_out=w_out,
                  moe_w_router=moe_w_router, moe_w_gate=moe_w_gate, moe_w_up=moe_w_up, moe_w_down=moe_w_down)
    bp, lp, d = x_prompt.shape
    bs, ls, _ = x_sample.shape
    depth = w_in.shape[0]
    past = cache_diff_k.shape[2]

    cond = jnp.concatenate([c, c_ctx[None, :], jnp.zeros((COND_ROWS - bs - 1, d), F32)], axis=0)
    mods = adaln(cond, w_ada, b_ada)

    dft_ch = _dft_channel_table()
    rope_diff, rope_mla = _rope_tables(ls)
    tabs_p = dict(dft_ch=dft_ch)
    tabs_p["dft_cos"], tabs_p["dft_sin"] = _dft_position_tables(lp)
    tabs_s = dict(dft_ch=dft_ch, rope_diff=rope_diff, rope_mla=rope_mla)
    tabs_s["dft_cos"], tabs_s["dft_sin"] = _dft_position_tables(ls)

    y_p = x_prompt.reshape(bp * lp, d)
    y_s = x_sample.reshape(bs * ls, d)
    new_ctx = []
    for l in range(depth):
        w = _layer_weights(params, l)
        lam_init = 0.8 - 0.6 * math.exp(-0.3 * l)
        m = mods[l].reshape(COND_ROWS, N_ADA, d)
        mods_s = [m[:bs, j][:, None, :] for j in range(N_ADA)]
        mods_p = [m[bs:bs + 1, j][:, None, :] for j in range(N_ADA)]
        y_p, ctx_l = _trunk_layer(y_p, mods_p, w, lam_init, bp, lp, tabs_p, None)
        new_ctx.append(ctx_l)
        ctx = (cache_diff_k[:, l].reshape(bs * past, DIFF_W), cache_diff_v[:, l].reshape(bs * past, DIFF_W),
               cache_mla_ckv[:, l].reshape(bs * past, MLA_KV_LORA), cache_mla_krope[:, l].reshape(bs * past, MLA_ROPE))
        y_s, _ = _trunk_layer(y_s, mods_s, w, lam_init, bs, ls, tabs_s, ctx)

    stack = lambda j, shape: jnp.stack([n[j].reshape((bp, lp) + shape) for n in new_ctx], axis=1)
    return (y_p.reshape(bp, lp, d), y_s.reshape(bs, ls, d),
            stack(0, (DIFF_HEADS, 2, DIFF_HD)), stack(1, (DIFF_HEADS, 2 * DIFF_HD)),
            stack(2, (MLA_KV_LORA,)), stack(3, (MLA_ROPE,)))
```

```python
import functools
import math

import jax
import jax.numpy as jnp
import numpy as np
from jax import lax
from jax.experimental import pallas as pl
from jax.experimental.pallas import tpu as pltpu

F32, BF16, I32, U32 = jnp.float32, jnp.bfloat16, jnp.int32, jnp.uint32

GRID_W = 64
ROPE_BASE = 10000.0
EPS = 1e-6
N_ADA = 6
FOURIER_GROUPS = 4
FOURIER_CH = 128
FOURIER_W = FOURIER_GROUPS * FOURIER_CH
DIFF_HEADS = 4
DIFF_HD = 64
DIFF_W = DIFF_HEADS * 2 * DIFF_HD
MLA_HEADS = 8
MLA_NOPE = 64
MLA_ROPE = 32
MLA_QK_HD = MLA_NOPE + MLA_ROPE
MLA_V = 64
MLA_Q_LORA = 384
MLA_KV_LORA = 256
N_EXPERTS = 16
EC_FACTOR = 2

LANES = 128
SUBLANES = 8
VMEM_LIMIT_BYTES = 56 * 1024 * 1024
LOG2E = math.log2(math.e)
MLA_HEAD_PAD = LANES
MLA_PAD_W = MLA_HEADS * MLA_HEAD_PAD
COND_ROWS = 16


def _cparams(n_axes, **kw):
    return pltpu.CompilerParams(dimension_semantics=("arbitrary",) * n_axes,
                                vmem_limit_bytes=VMEM_LIMIT_BYTES, **kw)


def _dot(a, b):
    return jnp.dot(a, b, preferred_element_type=F32)


def _dot_nt(a, b):
    return lax.dot_general(a, b, (((1,), (1,)), ((), ())), preferred_element_type=F32)


def _pick_tile(n, target):
    t = min(n, target)
    while n % t:
        t //= 2
    return t


def _pack_bf16_pairs(x):
    half = x.shape[1] // 2
    bits = pltpu.bitcast(x.astype(jnp.bfloat16).astype(F32), U32)
    return (bits[:, :half] >> 16) | bits[:, half:]


def _store_token_tiles(ref, packed):
    rows, w = packed.shape
    n = w // LANES
    for c in range(n):
        ref[pl.ds(c, rows, stride=n), :] = packed[:, c * LANES:(c + 1) * LANES]


def _unpack_rows(load_chunk, nchunk):
    lo, hi = [], []
    for c in range(nchunk):
        wds = load_chunk(c)
        lo.append(pltpu.bitcast(wds << 16, F32).astype(BF16))
        hi.append(pltpu.bitcast(wds & jnp.uint32(0xFFFF0000), F32).astype(BF16))
    return jnp.concatenate(lo + hi, axis=1)


def _adaln_body(c_ref, w_ref, b_ref, o_ref):
    c = c_ref[...]
    a = (c * jax.nn.sigmoid(c)).astype(BF16)
    o_ref[...] = _dot(a, w_ref[...].astype(BF16)) + b_ref[...]


def adaln(cond, w_ada, b_ada):
    depth, d, n = w_ada.shape
    tn = _pick_tile(n, 1024)
    return pl.pallas_call(
        _adaln_body,
        grid=(depth, n // tn),
        in_specs=[pl.BlockSpec((COND_ROWS, d), lambda l, j: (0, 0)),
                  pl.BlockSpec((None, d, tn), lambda l, j: (l, 0, j)),
                  pl.BlockSpec((None, 1, tn), lambda l, j: (l, 0, j))],
        out_specs=pl.BlockSpec((None, COND_ROWS, tn), lambda l, j: (l, 0, j)),
        out_shape=jax.ShapeDtypeStruct((depth, COND_ROWS, n), F32),
        compiler_params=_cparams(2),
    )(cond, w_ada, b_ada.reshape(depth, 1, n))


def _norm_mod(x, nw, sc, sh):
    r = lax.rsqrt(jnp.mean(x * x, axis=-1, keepdims=True) + EPS)
    return (x * r) * nw * (1.0 + sc) + sh


def _mod_index(nb, tiles_per_batch):
    if nb == 1:
        return lambda i, *_: (0, 0, 0)
    return lambda i, *_: (i // tiles_per_batch, 0, 0)


def _gates_body(x_ref, nw_ref, sc_ref, sh_ref, w_ref, h_ref, o_ref):
    @pl.when(pl.program_id(1) == 0)
    def _():
        h_ref[...] = _norm_mod(x_ref[...], nw_ref[...], sc_ref[...], sh_ref[...]).astype(BF16)

    o_ref[...] = jax.nn.sigmoid(_dot(h_ref[...], w_ref[...])).astype(BF16)


def norm_gates_proj(x, nw, sc, sh, w, seq):
    t, d = x.shape
    n = w.shape[1]
    nb = sc.shape[0]
    tm, tn = _pick_tile(seq if nb > 1 else t, 1024), _pick_tile(n, 1024)
    mod_spec = pl.BlockSpec((None, 1, d), _mod_index(nb, seq // tm))
    return pl.pallas_call(
        _gates_body,
        grid=(t // tm, n // tn),
        in_specs=[pl.BlockSpec((tm, d), lambda i, j: (i, 0)),
                  pl.BlockSpec((1, d), lambda i, j: (0, 0)),
                  mod_spec, mod_spec,
                  pl.BlockSpec((d, tn), lambda i, j: (0, j))],
        out_specs=[pl.BlockSpec((tm, d), lambda i, j: (i, 0)),
                   pl.BlockSpec((tm, tn), lambda i, j: (i, j))],
        out_shape=[jax.ShapeDtypeStruct((t, d), BF16), jax.ShapeDtypeStruct((t, n), BF16)],
        compiler_params=_cparams(2),
    )(x, nw, sc, sh, w)


def _fourier_ch_body(h_ref, w_ref, cs_ref, o_ref):
    u = _dot(h_ref[...], w_ref[...]).astype(BF16)
    for g in range(FOURIER_GROUPS):
        ab = _dot(u[:, g * FOURIER_CH:(g + 1) * FOURIER_CH], cs_ref[...])
        o_ref[:, g * FOURIER_CH:(g + 1) * FOURIER_CH] = ab[:, :FOURIER_CH].astype(BF16)
        o_ref[:, FOURIER_W + g * FOURIER_CH:FOURIER_W + (g + 1) * FOURIER_CH] = ab[:, FOURIER_CH:].astype(BF16)


def fourier_channel(h, w, cs):
    t, d = h.shape
    tm = _pick_tile(t, 1024)
    return pl.pallas_call(
        _fourier_ch_body,
        grid=(t // tm,),
        in_specs=[pl.BlockSpec((tm, d), lambda i: (i, 0)),
                  pl.BlockSpec((d, FOURIER_W), lambda i: (0, 0)),
                  pl.BlockSpec((FOURIER_CH, 2 * FOURIER_CH), lambda i: (0, 0))],
        out_specs=pl.BlockSpec((tm, 2 * FOURIER_W), lambda i: (i, 0)),
        out_shape=jax.ShapeDtypeStruct((t, 2 * FOURIER_W), BF16),
        compiler_params=_cparams(1),
    )(h, w, cs)


def _fourier_pos_body(c_ref, s_ref, ab_ref, o_ref):
    y = _dot(c_ref[...], ab_ref[:, :FOURIER_W]) - _dot(s_ref[...], ab_ref[:, FOURIER_W:])
    o_ref[...] = y.astype(BF16)


def fourier_position(ab, cpos, spos, seq):
    t = ab.shape[0]
    tr = _pick_tile(seq, 512)
    nr = seq // tr
    return pl.pallas_call(
        _fourier_pos_body,
        grid=(t // seq, nr),
        in_specs=[pl.BlockSpec((tr, seq), lambda b, r: (r, 0)),
                  pl.BlockSpec((tr, seq), lambda b, r: (r, 0)),
                  pl.BlockSpec((seq, 2 * FOURIER_W), lambda b, r: (b, 0))],
        out_specs=pl.BlockSpec((tr, FOURIER_W), lambda b, r: (b * nr + r, 0)),
        out_shape=jax.ShapeDtypeStruct((t, FOURIER_W), BF16),
        compiler_params=_cparams(2),
    )(cpos, spos, ab)


def _rope(x, cos, sin, off):
    w = x.shape[1]
    lane = lax.broadcasted_iota(I32, (1, w), 1)
    first = (lane & off) == 0
    partner = jnp.where(first, pltpu.roll(x, w - off, 1), pltpu.roll(x, off, 1))
    return x * cos + partner * sin


def _half_tile_norm(x):
    outs = []
    lane = lax.broadcasted_iota(I32, (1, LANES), 1)
    low = lane < DIFF_HD
    for j in range(x.shape[1] // LANES):
        seg = x[:, j * LANES:(j + 1) * LANES]
        sq = seg * seg
        s_lo = jnp.sum(jnp.where(low, sq, 0.0), axis=-1, keepdims=True)
        s_hi = jnp.sum(jnp.where(low, 0.0, sq), axis=-1, keepdims=True)
        ms = jnp.where(low, s_lo, s_hi) * (1.0 / DIFF_HD)
        outs.append(seg * lax.rsqrt(ms + EPS))
    return jnp.concatenate(outs, axis=1)


def _diff_qkv_body(*refs, rope):
    if rope:
        h_ref, w_ref, qn_ref, kn_ref, cos_ref, sin_ref, q_out, k_out, v_out = refs
    else:
        h_ref, w_ref, qn_ref, kn_ref, q_out, k_out, v_out, k32_out, v32_out = refs
    z = _dot(h_ref[...], w_ref[...])
    q = _half_tile_norm(z[:, :DIFF_W]) * qn_ref[...]
    k = _half_tile_norm(z[:, DIFF_W:2 * DIFF_W]) * kn_ref[...]
    v = z[:, 2 * DIFF_W:]
    if rope:
        q = _rope(q, cos_ref[...], sin_ref[...], DIFF_HD // 4)
        k = _rope(k, cos_ref[...], sin_ref[...], DIFF_HD // 4)
    else:
        k32_out[...] = k
        v32_out[...] = v
    q_out[...] = (q * (DIFF_HD ** -0.5 * LOG2E)).astype(BF16)
    k_out[...] = k.astype(BF16)
    v_out[...] = v.astype(BF16)


def diff_qkv(h, w, qn, kn, seq, rope_tabs):
    t, d = h.shape
    rope = rope_tabs is not None
    tm = _pick_tile(seq, 512)
    row = lambda i: (i, 0)
    const = lambda i: (0, 0)
    in_specs = [pl.BlockSpec((tm, d), row), pl.BlockSpec((d, 3 * DIFF_W), const),
                pl.BlockSpec((1, DIFF_W), const), pl.BlockSpec((1, DIFF_W), const)]
    args = [h, w, qn, kn]
    out_shape = [jax.ShapeDtypeStruct((t, DIFF_W), BF16)] * 3
    if rope:
        nt = seq // tm
        tab = pl.BlockSpec((tm, DIFF_W), lambda i: (i % nt, 0))
        in_specs += [tab, tab]
        args += list(rope_tabs)
    else:
        out_shape = out_shape + [jax.ShapeDtypeStruct((t, DIFF_W), F32)] * 2
    return pl.pallas_call(
        functools.partial(_diff_qkv_body, rope=rope),
        grid=(t // tm,),
        in_specs=in_specs,
        out_specs=[pl.BlockSpec((tm, DIFF_W), row)] * len(out_shape),
        out_shape=out_shape,
        compiler_params=_cparams(1),
    )(*args)


def _mla_keys(ckv_bf, krb, wk_ref, wv_ref, kn_ref, rope_tab):
    kn = _dot(ckv_bf, wk_ref[...])
    gain = kn_ref[...]
    kr = krb * gain
    if rope_tab is not None:
        kr = _rope(kr, rope_tab[0], rope_tab[1], MLA_ROPE // 4)
    kr_sq = jnp.sum(krb * krb, axis=-1, keepdims=True)
    outs = []
    for j in range(MLA_HEADS):
        seg = kn[:, j * LANES:(j + 1) * LANES]
        ms = (jnp.sum(seg * seg, axis=-1, keepdims=True) + kr_sq) * (1.0 / MLA_QK_HD)
        outs.append((seg * gain + kr) * lax.rsqrt(ms + EPS))
    return jnp.concatenate(outs, axis=1), _dot(ckv_bf, wv_ref[...])


def _mla_proj_body(*refs, rope):
    (h_ref, w_ref, qan_ref, wqb_ref, qn_ref, kvan_ref, wk_ref, wv_ref, kn_ref), refs = refs[:9], refs[9:]
    if rope:
        cos_ref, sin_ref, q_out, k_out, v_out = refs
        rope_tab = (cos_ref[...], sin_ref[...])
    else:
        q_out, k_out, v_out, ckv32_out, kr32_out = refs
        rope_tab = None
    z = _dot(h_ref[...], w_ref[...])
    q_a = z[:, :MLA_Q_LORA]
    kv_a = z[:, MLA_Q_LORA:MLA_Q_LORA + MLA_KV_LORA]
    krb = z[:, MLA_Q_LORA + MLA_KV_LORA:]
    c_q = q_a * lax.rsqrt(jnp.mean(q_a * q_a, axis=-1, keepdims=True) + EPS) * qan_ref[...]
    zq = _dot(c_q.astype(BF16), wqb_ref[...])
    gain = qn_ref[...]
    outs = []
    for j in range(MLA_HEADS):
        seg = zq[:, j * LANES:(j + 1) * LANES]
        ms = jnp.sum(seg * seg, axis=-1, keepdims=True) * (1.0 / MLA_QK_HD)
        val = seg * gain
        if rope:
            val = val * rope_tab[0] + zq[:, MLA_PAD_W + j * LANES:MLA_PAD_W + (j + 1) * LANES] * rope_tab[1]
        outs.append(val * (lax.rsqrt(ms + EPS) * (MLA_QK_HD ** -0.5 * LOG2E)))
    q_out[...] = jnp.concatenate(outs, axis=1).astype(BF16)
    c_kv = kv_a * lax.rsqrt(jnp.mean(kv_a * kv_a, axis=-1, keepdims=True) + EPS) * kvan_ref[...]
    k, v = _mla_keys(c_kv.astype(BF16), krb, wk_ref, wv_ref, kn_ref, rope_tab)
    if not rope:
        ckv32_out[...] = c_kv
        kr32_out[...] = krb
    k_out[...] = k.astype(BF16)
    v_out[...] = v.astype(BF16)


def mla_proj(h, w, p, seq, rope_tabs):
    t, d = h.shape
    rope = rope_tabs is not None
    tm = _pick_tile(seq, 512)
    row = lambda i: (i, 0)
    const = lambda i: (0, 0)
    wcols = MLA_Q_LORA + MLA_KV_LORA + LANES
    w_qb = p["w_qb_rope"] if rope else p["w_qb"]
    in_specs = [pl.BlockSpec((tm, d), row), pl.BlockSpec((d, wcols), const),
                pl.BlockSpec((1, MLA_Q_LORA), const), pl.BlockSpec(w_qb.shape, const),
                pl.BlockSpec((1, LANES), const), pl.BlockSpec((1, MLA_KV_LORA), const),
                pl.BlockSpec((MLA_KV_LORA, MLA_PAD_W), const), pl.BlockSpec((MLA_KV_LORA, MLA_PAD_W), const),
                pl.BlockSpec((1, LANES), const)]
    args = [h, w, p["qa_norm"], w_qb, p["q_norm"], p["kva_norm"], p["w_k"], p["w_v"], p["k_norm"]]
    out_shape = [jax.ShapeDtypeStruct((t, MLA_PAD_W), BF16)] * 3
    out_specs = [pl.BlockSpec((tm, MLA_PAD_W), row)] * 3
    if rope:
        nt = seq // tm
        tab = pl.BlockSpec((tm, LANES), lambda i: (i % nt, 0))
        in_specs += [tab, tab]
        args += list(rope_tabs)
    else:
        out_shape += [jax.ShapeDtypeStruct((t, MLA_KV_LORA), F32), jax.ShapeDtypeStruct((t, LANES), F32)]
        out_specs += [pl.BlockSpec((tm, MLA_KV_LORA), row), pl.BlockSpec((tm, LANES), row)]
    return pl.pallas_call(
        functools.partial(_mla_proj_body, rope=rope),
        grid=(t // tm,),
        in_specs=in_specs,
        out_specs=out_specs,
        out_shape=out_shape,
        compiler_params=_cparams(1),
    )(*args)


def _mla_ctx_body(ckv_ref, krb_ref, wk_ref, wv_ref, kn_ref, k_out, v_out):
    k, v = _mla_keys(ckv_ref[...].astype(BF16), krb_ref[...], wk_ref, wv_ref, kn_ref, None)
    k_out[...] = k.astype(BF16)
    v_out[...] = v.astype(BF16)


def mla_ctx_keys(ckv, krb, p):
    t = ckv.shape[0]
    tm = _pick_tile(t, 512)
    row = lambda i: (i, 0)
    const = lambda i: (0, 0)
    return pl.pallas_call(
        _mla_ctx_body,
        grid=(t // tm,),
        in_specs=[pl.BlockSpec((tm, MLA_KV_LORA), row), pl.BlockSpec((tm, LANES), row),
                  pl.BlockSpec((MLA_KV_LORA, MLA_PAD_W), const), pl.BlockSpec((MLA_KV_LORA, MLA_PAD_W), const),
                  pl.BlockSpec((1, LANES), const)],
        out_specs=[pl.BlockSpec((tm, MLA_PAD_W), row)] * 2,
        out_shape=[jax.ShapeDtypeStruct((t, MLA_PAD_W), BF16)] * 2,
        compiler_params=_cparams(1),
    )(ckv, krb, p["w_k"], p["w_v"], p["k_norm"])


SOFTMAX_SUM_FLOOR = 2.0 ** -100


def _attn_body(*refs, nmaps, hp, has_ctx, subln, compact, tk, lam_init):
    refs = list(refs)
    kmax_scr = refs.pop()
    lamv_ref = refs.pop(0) if nmaps == 2 else None
    q_ref, k_ref, v_ref = refs[:3]
    refs = refs[3:]
    segments = []
    if has_ctx:
        segments.append((refs[0], refs[1]))
        refs = refs[2:]
    segments.append((k_ref, v_ref))
    subln_ref = refs.pop(0) if subln else None
    o_ref = refs[0]
    tq = q_ref.shape[0]
    lane = lax.broadcasted_iota(I32, (1, LANES), 1)
    masks = [lane < DIFF_HD, lane >= DIFF_HD] if nmaps == 2 else [None]

    def cols(hh):
        return slice(hh * LANES, (hh + 1) * LANES)

    @pl.when(pl.program_id(2) == 0)
    def _():
        for hh in range(hp):
            for i, mk in enumerate(masks):
                best = jnp.zeros((1, 1), F32)
                for kr, _ in segments:
                    kf = kr[:, cols(hh)].astype(F32)
                    sq = kf * kf if mk is None else jnp.where(mk, kf * kf, 0.0)
                    best = jnp.maximum(best, jnp.max(jnp.sum(sq, axis=-1, keepdims=True), axis=0, keepdims=True))
                row = hh * nmaps + i
                kmax_scr[row:row + 1, :] = jnp.broadcast_to(best, (1, LANES))

    def queries(hh):
        q = q_ref[:, cols(hh)]
        return [q if mk is None else jnp.where(mk, q, jnp.zeros_like(q)) for mk in masks]

    def key_chunks(hh):
        for kr, vr in segments:
            n = kr.shape[0]
            for c0 in range(0, n, tk):
                c1 = min(n, c0 + tk)
                yield kr[c0:c1, cols(hh)], vr[c0:c1, cols(hh)]

    def shifted(hh):
        acc, l = [], []
        for i, qi in enumerate(queries(hh)):
            qf = qi.astype(F32)
            row = hh * nmaps + i
            shift = jnp.sqrt(jnp.sum(qf * qf, axis=-1, keepdims=True) * kmax_scr[row:row + 1, 0:1])
            a = jnp.zeros((tq, LANES), F32)
            li = jnp.zeros((tq, 1), F32)
            for kc, vc in key_chunks(hh):
                p = jnp.exp2(_dot_nt(qi, kc) - shift)
                li = li + jnp.sum(p, axis=-1, keepdims=True)
                a = a + _dot(p.astype(BF16), vc)
            acc.append(a)
            l.append(li)
        return acc, l

    def running_max(hh):
        qs = queries(hh)
        m = [jnp.full((tq, 1), -jnp.inf, F32) for _ in qs]
        l = [jnp.zeros((tq, 1), F32) for _ in qs]
        acc = [jnp.zeros((tq, LANES), F32) for _ in qs]
        for kc, vc in key_chunks(hh):
            for i, qi in enumerate(qs):
                s = _dot_nt(qi, kc)
                mn = jnp.maximum(m[i], jnp.max(s, axis=-1, keepdims=True))
                alpha = jnp.exp2(m[i] - mn)
                p = jnp.exp2(s - mn)
                l[i] = alpha * l[i] + jnp.sum(p, axis=-1, keepdims=True)
                acc[i] = alpha * acc[i] + _dot(p.astype(BF16), vc)
                m[i] = mn
        return acc, l

    def head_out(acc, l):
        o = acc[0] / l[0]
        if nmaps == 2:
            lv = lamv_ref[...]
            lam = (jnp.exp(jnp.sum(lv[0:1] * lv[1:2], axis=-1, keepdims=True))
                   - jnp.exp(jnp.sum(lv[2:3] * lv[3:4], axis=-1, keepdims=True)) + lam_init)
            o = o - lam * (acc[1] / l[1])
        if subln:
            o = o * lax.rsqrt(jnp.mean(o * o, axis=-1, keepdims=True) + EPS) * subln_ref[...] * (1.0 - lam_init)
        return o

    def write(outs):
        if compact:
            for j in range(hp // 2):
                pair = jnp.where(lane < LANES // 2, outs[2 * j], pltpu.roll(outs[2 * j + 1], LANES // 2, 1))
                o_ref[:, cols(j)] = pair.astype(BF16)
        else:
            for hh in range(hp):
                o_ref[:, cols(hh)] = outs[hh].astype(BF16)

    results = [shifted(hh) for hh in range(hp)]
    write([head_out(acc, l) for acc, l in results])

    lmin = functools.reduce(jnp.minimum, [jnp.min(li) for _, l in results for li in l])

    @pl.when(lmin < SOFTMAX_SUM_FLOOR)
    def _():
        write([head_out(*running_max(hh)) for hh in range(hp)])


def attention(q, k, v, ctx, *, nbatch, seq, heads, nmaps, compact=False, lamv=None, subln=None, lam_init=0.0):
    t, w = q.shape
    tq = _pick_tile(seq, 1024)
    nq = seq // tq
    hp = heads if seq <= 512 else (2 if compact else 1)
    assert hp * nmaps <= SUBLANES and heads % hp == 0
    wo = hp * LANES // 2 if compact else hp * LANES
    qspec = pl.BlockSpec((tq, hp * LANES), lambda b, h, i: (b * nq + i, h))
    kvspec = pl.BlockSpec((seq, hp * LANES), lambda b, h, i: (b, h))
    in_specs, args = [], []
    if nmaps == 2:
        in_specs.append(pl.BlockSpec((SUBLANES, LANES), lambda b, h, i: (0, 0)))
        args.append(lamv)
    in_specs += [qspec, kvspec, kvspec]
    args += [q, k, v]
    if ctx is not None:
        past = ctx[0].shape[0] // nbatch
        cspec = pl.BlockSpec((past, hp * LANES), lambda b, h, i: (b, h))
        in_specs += [cspec, cspec]
        args += list(ctx)
    if subln is not None:
        in_specs.append(pl.BlockSpec((1, LANES), lambda b, h, i: (0, 0)))
        args.append(subln)
    return pl.pallas_call(
        functools.partial(_attn_body, nmaps=nmaps, hp=hp, has_ctx=ctx is not None, subln=subln is not None,
                          compact=compact, tk=512, lam_init=lam_init),
        grid=(nbatch, heads // hp, nq),
        in_specs=in_specs,
        out_specs=pl.BlockSpec((tq, wo), lambda b, h, i: (b * nq + i, h)),
        out_shape=jax.ShapeDtypeStruct((t, w // 2 if compact else w), BF16),
        scratch_shapes=[pltpu.VMEM((SUBLANES, LANES), F32)],
        compiler_params=_cparams(3),
    )(*args)


def _merge_body(x_ref, f_ref, od_ref, om_ref, g0_ref, g1_ref, g2_ref, wf_ref, wd_ref, wm_ref, wo_ref,
                gate1_ref, nw_ref, sc_ref, sh_ref, wrh_ref, wrl_ref,
                x1_ref, h2p_ref, aff_ref):
    merged = g0_ref[...].astype(F32) * _dot(f_ref[...], wf_ref[...])
    merged = merged + g1_ref[...].astype(F32) * _dot(od_ref[...], wd_ref[...])
    merged = merged + g2_ref[...].astype(F32) * _dot(om_ref[...], wm_ref[...])
    x1 = x_ref[...] + gate1_ref[...] * _dot(merged.astype(BF16), wo_ref[...])
    x1_ref[...] = x1
    h2 = _norm_mod(x1, nw_ref[...], sc_ref[...], sh_ref[...])
    _store_token_tiles(h2p_ref, _pack_bf16_pairs(h2))
    h_hi = h2.astype(jnp.bfloat16)
    h_lo = (h2 - h_hi.astype(F32)).astype(BF16)
    hi_terms = _dot(h_hi, wrl_ref[...])
    logits = hi_terms[:, :LANES] + hi_terms[:, LANES:] + _dot(h_lo, wrh_ref[...])
    lane = lax.broadcasted_iota(I32, (1, LANES), 1)
    logits = jnp.where(lane < N_EXPERTS, logits, -1e30)
    e = jnp.exp(logits - jnp.max(logits, axis=-1, keepdims=True))
    aff = e / jnp.sum(e, axis=-1, keepdims=True)
    aff_ref[...] = aff.T[:N_EXPERTS, :]


def merge_out(x, f, od, om, gates, w, gate1, nw, sc, sh, seq):
    t, d = x.shape
    tm = _pick_tile(seq, 256)
    nchunk = d // 2 // LANES
    row = lambda i: (i, 0)
    const = lambda i: (0, 0)
    once = pl.Buffered(1)
    mod_spec = pl.BlockSpec((None, 1, d), _mod_index(sc.shape[0], seq // tm))
    wspec = lambda a: pl.BlockSpec(a.shape, const, pipeline_mode=once)
    in_specs = [pl.BlockSpec((tm, d), row),
                pl.BlockSpec((tm, f.shape[1]), row), pl.BlockSpec((tm, od.shape[1]), row),
                pl.BlockSpec((tm, om.shape[1]), row),
                pl.BlockSpec((tm, d), lambda i: (i, 0)), pl.BlockSpec((tm, d), lambda i: (i, 1)),
                pl.BlockSpec((tm, d), lambda i: (i, 2)),
                wspec(w["br_f"]), wspec(w["br_d"]), wspec(w["br_m"]), wspec(w["out"]),
                mod_spec, pl.BlockSpec((1, d), const), mod_spec, mod_spec,
                wspec(w["router_hi"]), wspec(w["router_hi_lo"])]
    return pl.pallas_call(
        _merge_body,
        grid=(t // tm,),
        in_specs=in_specs,
        out_specs=[pl.BlockSpec((tm, d), row),
                   pl.BlockSpec((tm * nchunk, LANES), row),
                   pl.BlockSpec((N_EXPERTS, tm), lambda i: (0, i))],
        out_shape=[jax.ShapeDtypeStruct((t, d), F32),
                   jax.ShapeDtypeStruct((t * nchunk, LANES), U32),
                   jax.ShapeDtypeStruct((N_EXPERTS, t), F32)],
        compiler_params=_cparams(1),
    )(x, f, od, om, gates, gates, gates, w["br_f"], w["br_d"], w["br_m"], w["out"],
      gate1, nw, sc, sh, w["router_hi"], w["router_hi_lo"])


def _cumsum_lanes(x):
    n = x.shape[1]
    lane = lax.broadcasted_iota(I32, (1, n), 1)
    s = 1
    while s < n:
        x = x + jnp.where(lane >= s, pltpu.roll(x, s, 1), 0)
        s *= 2
    return x


ROUTE_MATMUL_MIN_TOKENS = 1024


def _route_body(aff_ref, idx_ref, dst_ref, gv_ref, seg_ref, key_scr, dst_scr, *, cap, row_chunk):
    b = pl.program_id(0)
    aff = aff_ref[...]
    ne, n = aff.shape
    bits = pltpu.bitcast(aff, I32)

    def search(i, thr):
        cand = thr | (1 << (30 - i))
        cnt = jnp.sum(jnp.where(bits >= cand, 1.0, 0.0), axis=-1, keepdims=True)
        return jnp.where(cnt >= cap, cand, thr)

    thr = lax.fori_loop(0, 31, search, jnp.zeros((ne, 1), I32))
    gt = bits > thr
    eq = (bits == thr).astype(I32)
    need = cap - jnp.sum(jnp.where(gt, 1.0, 0.0), axis=-1, keepdims=True).astype(I32)
    eq_before = _cumsum_lanes(eq) - eq
    sel = jnp.where(gt | ((eq > 0) & (eq_before < need)), 1, 0)
    cum = _cumsum_lanes(sel)
    key_scr[...] = sel * cum

    before = jnp.zeros((1, n), I32)
    for e in range(ne):
        dst_scr[e:e + 1, :] = before
        before = before + sel[e:e + 1, :]
    k_tok = before
    start = _cumsum_lanes(k_tok) - k_tok + b * (ne * cap)
    dst_scr[...] = dst_scr[...] + start
    seg_ref[0:1, :] = start
    seg_ref[1:2, :] = start + k_tok
    seg_ref[2:SUBLANES, :] = jnp.zeros((SUBLANES - 2, n), I32)

    tok = lax.broadcasted_iota(I32, (1, n), 1) + b * n
    lane = lax.broadcasted_iota(I32, (1, LANES), 1)
    idx_ref[...] = jnp.zeros(idx_ref.shape, I32)
    dst_ref[...] = jnp.zeros(dst_ref.shape, I32)
    gv_ref[...] = jnp.zeros(gv_ref.shape, F32)
    byte = lambda v, shift: ((v >> shift) & 255).astype(F32)

    def per_expert(e, carry):
        key = key_scr[pl.ds(e, 1), :]
        dst = dst_scr[pl.ds(e, 1), :]
        af = aff_ref[pl.ds(e, 1), :]
        a_hi = af.astype(jnp.bfloat16).astype(F32)
        a_mid = (af - a_hi).astype(jnp.bfloat16).astype(F32)
        a_lo = af - a_hi - a_mid
        vals = jnp.concatenate([byte(tok, 16), byte(tok, 8), byte(tok, 0),
                                byte(dst, 16), byte(dst, 8), byte(dst, 0),
                                a_hi, a_mid, a_lo] + [jnp.zeros((1, n), F32)] * 7, axis=0).astype(BF16)
        for c0 in range(0, cap, row_chunk):
            slot = lax.broadcasted_iota(I32, (row_chunk, 1), 0) + (c0 + 1)
            hit = key == slot
            if n >= ROUTE_MATMUL_MIN_TOKENS:
                got = _dot_nt(jnp.where(hit, 1.0, 0.0).astype(BF16), vals)
                word = lambda j: (got[:, j:j + 1].astype(I32) * 65536 + got[:, j + 1:j + 2].astype(I32) * 256
                                  + got[:, j + 2:j + 3].astype(I32))
                i_col, d_col = word(0), word(3)
                g_col = got[:, 6:7] + got[:, 7:8] + got[:, 8:9]
            else:
                pick = lambda v: jnp.sum(jnp.where(hit, v, 0.0), axis=-1, keepdims=True)
                i_col, d_col = pick(tok.astype(F32)).astype(I32), pick(dst.astype(F32)).astype(I32)
                g_col = pick(af)
            rows = pl.ds(c0, row_chunk)
            idx_ref[rows, :] = jnp.where(lane == e, i_col, idx_ref[rows, :])
            dst_ref[rows, :] = jnp.where(lane == e, d_col, dst_ref[rows, :])
            gv_ref[rows, :] = jnp.where(lane == e, g_col, gv_ref[rows, :])
        return carry

    lax.fori_loop(0, ne, per_expert, 0)


def route(aff_t, nbatch, seq):
    ne = aff_t.shape[0]
    cap = EC_FACTOR * seq // ne
    row_chunk = min(cap, 256)
    tab = lambda dt: jax.ShapeDtypeStruct((nbatch, cap, LANES), dt)
    tspec = pl.BlockSpec((None, cap, LANES), lambda b: (b, 0, 0))
    return pl.pallas_call(
        functools.partial(_route_body, cap=cap, row_chunk=row_chunk),
        grid=(nbatch,),
        in_specs=[pl.BlockSpec((ne, seq), lambda b: (0, b))],
        out_specs=[tspec, tspec, tspec, pl.BlockSpec((None, SUBLANES, seq), lambda b: (b, 0, 0))],
        out_shape=[tab(I32), tab(I32), tab(F32), jax.ShapeDtypeStruct((nbatch, SUBLANES, seq), I32)],
        scratch_shapes=[pltpu.VMEM((ne, seq), I32), pltpu.VMEM((ne, seq), I32)],
        compiler_params=_cparams(1),
    )(aff_t)


DMA_UNROLL = 8


def _moe_body(idx_a, idx_b, idx_a_next, dst_a, dst_b, gv_a, gv_b, h2p_ref, wg_ref, wu_ref, wd_ref, g_ref,
              xa, xb, ya, yb, sem, *, rt, d):
    nw = d // 2 // LANES
    step = pl.program_id(0) * pl.num_programs(1) + pl.program_id(1)
    last = pl.num_programs(0) * pl.num_programs(1) - 1
    in_a, in_b, out_a, out_b = (sem.at[i] for i in range(4))

    def row_tile(r):
        return pl.ds(r * nw if isinstance(r, int) else pl.multiple_of(r * nw, nw), nw)

    def looped(fn):
        def body(i, carry):
            for u in range(DMA_UNROLL):
                fn(i * DMA_UNROLL + u)
            return carry
        lax.fori_loop(0, rt // DMA_UNROLL, body, 0)

    def inline(fn):
        for r in range(rt):
            fn(r)

    def gather(idx_ref, r, xbuf, s):
        src = 0 if idx_ref is None else idx_ref[0, r]
        return pltpu.make_async_copy(h2p_ref.at[row_tile(src), :], xbuf.at[row_tile(r), :], s)

    def scatter(ybuf, r, dst_ref, s):
        dst = 0 if dst_ref is None else dst_ref[0, r]
        return pltpu.make_async_copy(ybuf.at[row_tile(r), :], g_ref.at[row_tile(dst), :], s)

    def drain_gather(xbuf, s):
        looped(lambda r: gather(None, r, xbuf, s).wait())

    def drain_scatter(ybuf, s):
        looped(lambda r: scatter(ybuf, r, None, s).wait())

    def ffn(xbuf, gv_ref):
        x = _unpack_rows(lambda c: xbuf[pl.ds(c, rt, stride=nw), :], nw)
        a = _dot(x, wg_ref[...])
        u = _dot(x, wu_ref[...])
        mid = (a * jax.nn.sigmoid(a) * u).astype(BF16)
        return _pack_bf16_pairs(_dot(mid, wd_ref[...]) * gv_ref[...])

    @pl.when(step == 0)
    def _():
        looped(lambda r: gather(idx_a, r, xa, in_a).start())

    drain_gather(xa, in_a)
    inline(lambda r: gather(idx_b, r, xb, in_b).start())
    y = ffn(xa, gv_a)

    @pl.when(step > 0)
    def _():
        drain_scatter(ya, out_a)

    _store_token_tiles(ya, y)

    drain_gather(xb, in_b)
    inline(lambda r: gather(idx_a_next, r, xa, in_a).start())
    inline(lambda r: scatter(ya, r, dst_a, out_a).start())
    y = ffn(xb, gv_b)

    @pl.when(step > 0)
    def _():
        drain_scatter(yb, out_b)

    _store_token_tiles(yb, y)
    looped(lambda r: scatter(yb, r, dst_b, out_b).start())

    @pl.when(step == last)
    def _():
        drain_gather(xa, in_a)
        drain_scatter(ya, out_a)
        drain_scatter(yb, out_b)


def moe_experts(idx, dst, gv, h2p, wg, wu, wd, layer, d):
    _, ne, _, ff = wg.shape
    rt = idx.shape[2]
    ntiles = idx.shape[0]
    pairs = ntiles // ne // 2
    nw = d // 2 // LANES
    tile_a = lambda e, j: 2 * (e * pairs + j)
    smem = lambda tile: pl.BlockSpec((None, 1, rt), lambda e, j: (tile(e, j), 0, 0), memory_space=pltpu.SMEM)
    vmem_col = lambda tile: pl.BlockSpec((rt, 1), lambda e, j: (tile(e, j), 0))
    tile_b = lambda e, j: tile_a(e, j) + 1
    tile_a_next = lambda e, j: jnp.minimum(tile_a(e, j) + 2, ntiles - 2)
    wspec = lambda a: pl.BlockSpec((None, None) + a.shape[2:], lambda e, j: (layer, e, 0, 0))
    row_buf = pltpu.VMEM((rt * nw, LANES), U32)
    return pl.pallas_call(
        functools.partial(_moe_body, rt=rt, d=d),
        grid=(ne, pairs),
        in_specs=[smem(tile_a), smem(tile_b), smem(tile_a_next), smem(tile_a), smem(tile_b),
                  vmem_col(tile_a), vmem_col(tile_b),
                  pl.BlockSpec(memory_space=pl.ANY), wspec(wg), wspec(wu), wspec(wd)],
        out_specs=pl.BlockSpec(memory_space=pl.ANY),
        out_shape=jax.ShapeDtypeStruct((ntiles * rt * nw, LANES), U32),
        scratch_shapes=[row_buf, row_buf, row_buf, row_buf, pltpu.SemaphoreType.DMA((4,))],
        compiler_params=_cparams(2, has_side_effects=True),
    )(idx, idx, idx, dst, dst, gv, gv, h2p, wg, wu, wd)


COMBINE_WINDOW = 1024


def _combine_body(tb_ref, x_ref, s0_ref, s1_ref, gate_ref, g_ref, o_ref, gbuf, sem, *, win, d, npairs):
    t = pl.program_id(0)
    nw = d // 2 // LANES
    first = tb_ref[t]
    last = tb_ref[t + 1]
    s0 = s0_ref[...]
    s1 = s1_ref[...]

    def window_start(lo):
        return jnp.minimum(lo, npairs - win)

    def window_copy(lo, slot):
        rows = pl.ds(pl.multiple_of(window_start(lo) * nw, nw), win * nw)
        return pltpu.make_async_copy(g_ref.at[rows, :], gbuf.at[slot], sem.at[slot])

    def window_sum(lo, slot):
        pos = lax.broadcasted_iota(I32, (1, win), 1) + window_start(lo)
        own = (pos >= s0) & (pos < s1) & (pos >= lo)
        sel = jnp.where(own, 1.0, 0.0).astype(BF16)
        rows = _unpack_rows(lambda c: gbuf[slot, pl.ds(c, win, stride=nw), :], nw)
        return _dot(sel, rows)

    slot = t % 2

    @pl.when(t == 0)
    def _():
        window_copy(first, slot).start()

    window_copy(first, slot).wait()

    @pl.when(t + 1 < pl.num_programs(0))
    def _():
        window_copy(last, 1 - slot).start()

    def extra_window(wi, acc):
        lo = first + wi * win
        cp = window_copy(lo, 2)
        cp.start()
        cp.wait()
        return acc + window_sum(lo, 2)

    nwin = (last - first + win - 1) // win
    acc = lax.fori_loop(1, nwin, extra_window, window_sum(first, slot))
    o_ref[...] = x_ref[...] + gate_ref[...] * acc


def combine(x1, seg0, seg1, tile_bounds, gate2, g, seq):
    t, d = x1.shape
    tm = _pick_tile(seq, 256)
    nw = d // 2 // LANES
    npairs = g.shape[0] // nw
    win = min(COMBINE_WINDOW, npairs)
    row = lambda i, tb: (i, 0)
    grid_spec = pltpu.PrefetchScalarGridSpec(
        num_scalar_prefetch=1,
        grid=(t // tm,),
        in_specs=[pl.BlockSpec((tm, d), row), pl.BlockSpec((tm, 1), row), pl.BlockSpec((tm, 1), row),
                  pl.BlockSpec((None, 1, d), _mod_index(gate2.shape[0], seq // tm)),
                  pl.BlockSpec(memory_space=pl.ANY)],
        out_specs=pl.BlockSpec((tm, d), row),
        scratch_shapes=[pltpu.VMEM((3, win * nw, LANES), U32), pltpu.SemaphoreType.DMA((3,))],
    )
    return pl.pallas_call(
        functools.partial(_combine_body, win=win, d=d, npairs=npairs),
        grid_spec=grid_spec,
        out_shape=jax.ShapeDtypeStruct((t, d), F32),
        compiler_params=_cparams(1),
    )(tile_bounds, x1, seg0, seg1, gate2, g)


def _dft_channel_table():
    k = np.arange(FOURIER_CH)
    ang = 2.0 * np.pi * ((k[:, None] * k[None, :]) % FOURIER_CH) / FOURIER_CH
    return jnp.asarray(np.concatenate([np.cos(ang), np.sin(ang)], axis=1), BF16)


def _dft_position_tables(seq):
    r = 1 << ((seq.bit_length() - 1) // 2)
    n = jnp.arange(seq, dtype=I32)[None, :]
    k1 = jnp.arange(seq // r, dtype=I32)[:, None] * r
    k2 = jnp.arange(r, dtype=I32)[:, None]
    ang = lambda k: ((k * n) % seq).astype(F32) * (2.0 * math.pi / seq)
    a, b = ang(k1), ang(k2)
    ca, sa = jnp.cos(a)[:, None, :], jnp.sin(a)[:, None, :]
    cb, sb = jnp.cos(b)[None, :, :], jnp.sin(b)[None, :, :]
    scale = 1.0 / math.sqrt(seq * FOURIER_CH)
    cos = ((ca * cb - sa * sb) * scale).astype(BF16).reshape(seq, seq)
    sin = ((sa * cb + ca * sb) * scale).astype(BF16).reshape(seq, seq)
    return cos, sin


def _rope_pattern(seq, width):
    nf = width // 4
    pos = np.arange(seq)
    inv = ROPE_BASE ** (-np.arange(nf, dtype=np.float64) / nf)
    ar = (pos // GRID_W)[:, None] * inv
    ac = (pos % GRID_W)[:, None] * inv
    cos = np.concatenate([np.cos(ar), np.cos(ar), np.cos(ac), np.cos(ac)], axis=1)
    sin = np.concatenate([-np.sin(ar), np.sin(ar), -np.sin(ac), np.sin(ac)], axis=1)
    return cos, sin


def _rope_tables(seq):
    cd, sd = _rope_pattern(seq, DIFF_HD)
    diff = (np.tile(cd, (1, 2 * DIFF_HEADS)), np.tile(sd, (1, 2 * DIFF_HEADS)))
    cm, sm = _rope_pattern(seq, MLA_ROPE)
    pad_r = MLA_HEAD_PAD - MLA_QK_HD
    cm = np.concatenate([np.ones((seq, MLA_NOPE)), cm, np.ones((seq, pad_r))], axis=1)
    sm = np.concatenate([np.zeros((seq, MLA_NOPE)), sm, np.zeros((seq, pad_r))], axis=1)
    mla = (cm, sm)
    as_f32 = lambda pair: tuple(jnp.asarray(a, F32) for a in pair)
    return as_f32(diff), as_f32(mla)


def _pad_heads(a, width):
    lead = a.shape[:-1]
    a = a.reshape(lead + (MLA_HEADS, width))
    a = jnp.pad(a, [(0, 0)] * len(lead) + [(0, 0), (0, MLA_HEAD_PAD - width)])
    return a.reshape(lead + (MLA_PAD_W,))


def _layer_weights(p, l):
    w_in = p["w_in"][l]
    c0 = FOURIER_W
    c1 = c0 + 3 * DIFF_W
    c2 = c1 + MLA_Q_LORA + MLA_KV_LORA
    c3 = c2 + MLA_ROPE
    kr_cols = jnp.pad(w_in[:, c2:c3], ((0, 0), (MLA_NOPE, LANES - MLA_QK_HD)))
    w_kvb = p["mla_w_kvb"][l].reshape(MLA_KV_LORA, MLA_HEADS, MLA_NOPE + MLA_V)
    router = jnp.pad(p["moe_w_router"][l], ((0, 0), (0, LANES - N_EXPERTS)))
    router_hi = router.astype(BF16)
    tile = lambda v, reps: jnp.tile(v, reps)[None, :].astype(F32)
    pad_gain = lambda v: jnp.pad(v, (0, MLA_HEAD_PAD - MLA_QK_HD))[None, :].astype(F32)
    w_qb = _pad_heads(p["mla_w_qb"][l], MLA_QK_HD)
    lane = np.arange(LANES)
    rot = (lane >= MLA_NOPE) & (lane < MLA_QK_HD)
    off = MLA_ROPE // 4
    partner = np.where(rot, np.where((lane & off) == 0, lane + off, lane - off), lane)
    w_qb_partner = ((w_qb.reshape(MLA_Q_LORA, MLA_HEADS, LANES) * pad_gain(p["mla_qnorm_w"][l]))[:, :, partner]
                    * jnp.asarray(rot, F32)).reshape(MLA_Q_LORA, MLA_PAD_W)
    return dict(
        fourier=w_in[:, :c0].astype(BF16),
        diff=w_in[:, c0:c1].astype(BF16),
        mla_in=jnp.concatenate([w_in[:, c1:c2], kr_cols], axis=1).astype(BF16),
        gates=w_in[:, c3:].astype(BF16),
        diff_qn=tile(p["diff_qnorm_w"][l], 2 * DIFF_HEADS),
        diff_kn=tile(p["diff_knorm_w"][l], 2 * DIFF_HEADS),
        subln=p["diff_subln_w"][l][None, :].astype(F32),
        lamv=jnp.pad(jnp.stack([p["diff_lambda_q1"][l], p["diff_lambda_k1"][l],
                                p["diff_lambda_q2"][l], p["diff_lambda_k2"][l]]).astype(F32),
                     ((0, SUBLANES - 4), (0, LANES - DIFF_HD))),
        mla=dict(
            qa_norm=p["mla_qa_norm_w"][l][None, :].astype(F32),
            w_qb=w_qb.astype(BF16),
            w_qb_rope=jnp.concatenate([w_qb, w_qb_partner], axis=1).astype(BF16),
            q_norm=pad_gain(p["mla_qnorm_w"][l]),
            kva_norm=p["mla_kva_norm_w"][l][None, :].astype(F32),
            w_k=_pad_heads(w_kvb[:, :, :MLA_NOPE].reshape(MLA_KV_LORA, -1), MLA_NOPE).astype(BF16),
            w_v=_pad_heads(w_kvb[:, :, MLA_NOPE:].reshape(MLA_KV_LORA, -1), MLA_V).astype(BF16),
            k_norm=pad_gain(p["mla_knorm_w"][l]),
        ),
        merge=dict(
            br_f=p["w_br_fourier"][l].astype(BF16),
            br_d=p["w_br_diff"][l].astype(BF16),
            br_m=p["w_br_mla"][l].astype(BF16),
            out=p["w_out"][l].astype(BF16),
            router_hi=router_hi,
            router_hi_lo=jnp.concatenate([router_hi, (router - router_hi.astype(F32)).astype(BF16)], axis=1),
        ),
        layer=l,
        norm1=p["norm1_w"][l][None, :].astype(F32),
        norm2=p["norm2_w"][l][None, :].astype(F32),
    )


def _trunk_layer(x, mods, w, lam_init, nbatch, seq, tabs, ctx):
    t, d = x.shape
    sh1, sc1, g1, sh2, sc2, g2 = mods
    h, gates = norm_gates_proj(x, w["norm1"], sc1, sh1, w["gates"], seq)
    ab = fourier_channel(h, w["fourier"], tabs["dft_ch"])
    y_f = fourier_position(ab, tabs["dft_cos"], tabs["dft_sin"], seq)
    new_ctx = None
    if ctx is None:
        q_d, k_d, v_d, k_d32, v_d32 = diff_qkv(h, w["diff"], w["diff_qn"], w["diff_kn"], seq, None)
        q_m, k_m, v_m, ckv32, krb32 = mla_proj(h, w["mla_in"], w["mla"], seq, None)
        new_ctx = (k_d32, v_d32, ckv32, krb32[:, MLA_NOPE:MLA_QK_HD])
        ctx_d = ctx_m = None
    else:
        q_d, k_d, v_d = diff_qkv(h, w["diff"], w["diff_qn"], w["diff_kn"], seq, tabs["rope_diff"])
        q_m, k_m, v_m = mla_proj(h, w["mla_in"], w["mla"], seq, tabs["rope_mla"])
        kd_c, vd_c, ckv_c, kr_c = ctx
        ctx_d = (kd_c.astype(BF16), vd_c.astype(BF16))
        krb_c = jnp.pad(kr_c, ((0, 0), (MLA_NOPE, LANES - MLA_QK_HD)))
        ctx_m = mla_ctx_keys(ckv_c, krb_c, w["mla"])
    o_d = attention(q_d, k_d, v_d, ctx_d, nbatch=nbatch, seq=seq, heads=DIFF_HEADS, nmaps=2,
                    lamv=w["lamv"], subln=w["subln"], lam_init=lam_init)
    o_m = attention(q_m, k_m, v_m, ctx_m, nbatch=nbatch, seq=seq, heads=MLA_HEADS, nmaps=1, compact=True)
    x1, h2p, aff_t = merge_out(x, y_f, o_d, o_m, gates, w["merge"], g1, w["norm2"], sc2, sh2, seq)
    idx, dst, gv, seg = route(aff_t, nbatch, seq)
    cap = idx.shape[1]
    rt = _pick_tile(nbatch * cap // 2, 512)
    by_expert = lambda a: jnp.transpose(a[:, :, :N_EXPERTS], (2, 0, 1))
    idx_e = by_expert(idx).reshape(-1, 1, rt)
    dst_e = by_expert(dst).reshape(-1, 1, rt)
    gv_e = by_expert(gv).reshape(-1, 1)
    g = moe_experts(idx_e, dst_e, gv_e, h2p, *w["moe"], w["layer"], d)
    seg0 = seg[:, 0, :].reshape(t, 1)
    seg1 = seg[:, 1, :].reshape(t, 1)
    tm = _pick_tile(seq, 256)
    npairs = nbatch * N_EXPERTS * cap
    tile_bounds = jnp.concatenate([seg0[::tm, 0], jnp.full((1,), npairs, I32)])
    x2 = combine(x1, seg0, seg1, tile_bounds, g2, g, seq)
    return x2, new_ctx


def kernel(x_prompt, x_sample, cache_diff_k, cache_diff_v, cache_mla_ckv, cache_mla_krope, c, c_ctx, w_ada, b_ada, norm1_w, norm2_w, w_in, diff_qnorm_w, diff_knorm_w, diff_lambda_q1, diff_lambda_k1, diff_lambda_q2, diff_lambda_k2, diff_subln_w, mla_qa_norm_w, mla_w_qb, mla_kva_norm_w, mla_w_kvb, mla_qnorm_w, mla_knorm_w, w_br_fourier, w_br_diff, w_br_mla, w_out, moe_w_router, moe_w_gate, moe_w_up, moe_w_down):
    params = dict(w_in=w_in, norm1_w=norm1_w, norm2_w=norm2_w,
                  diff_qnorm_w=diff_qnorm_w, diff_knorm_w=diff_knorm_w,
                  diff_lambda_q1=diff_lambda_q1, diff_lambda_k1=diff_lambda_k1,
                  diff_lambda_q2=diff_lambda_q2, diff_lambda_k2=diff_lambda_k2, diff_subln_w=diff_subln_w,
                  mla_qa_norm_w=mla_qa_norm_w, mla_w_qb=mla_w_qb, mla_kva_norm_w=mla_kva_norm_w,
                  mla_w_kvb=mla_w_kvb, mla_qnorm_w=mla_qnorm_w, mla_knorm_w=mla_knorm_w,
                  w_br_fourier=w_br_fourier, w_br_diff=w_br_diff, w_br_mla=w_br_mla, w_out=w_out,
                  moe_w_router=moe_w_router, moe_w_gate=moe_w_gate, moe_w_up=moe_w_up, moe_w_down=moe_w_down)
    bp, lp, d = x_prompt.shape
    bs, ls, _ = x_sample.shape
    depth = w_in.shape[0]
    past = cache_diff_k.shape[2]

    cond = jnp.concatenate([c, c_ctx[None, :], jnp.zeros((COND_ROWS - bs - 1, d), F32)], axis=0)
    mods = adaln(cond, w_ada, b_ada)

    dft_ch = _dft_channel_table()
    rope_diff, rope_mla = _rope_tables(ls)
    tabs_p = dict(dft_ch=dft_ch)
    tabs_p["dft_cos"], tabs_p["dft_sin"] = _dft_position_tables(lp)
    tabs_s = dict(dft_ch=dft_ch, rope_diff=rope_diff, rope_mla=rope_mla)
    tabs_s["dft_cos"], tabs_s["dft_sin"] = _dft_position_tables(ls)

    y_p = x_prompt.reshape(bp * lp, d)
    y_s = x_sample.reshape(bs * ls, d)
    new_ctx = []
    moe_weights = (moe_w_gate.astype(BF16), moe_w_up.astype(BF16), moe_w_down.astype(BF16))
    for l in range(depth):
        w = _layer_weights(params, l)
        w["moe"] = moe_weights
        lam_init = 0.8 - 0.6 * math.exp(-0.3 * l)
        m = mods[l].reshape(COND_ROWS, N_ADA, d)
        mods_s = [m[:bs, j][:, None, :] for j in range(N_ADA)]
        mods_p = [m[bs:bs + 1, j][:, None, :] for j in range(N_ADA)]
        y_p, ctx_l = _trunk_layer(y_p, mods_p, w, lam_init, bp, lp, tabs_p, None)
        new_ctx.append(ctx_l)
        ctx = (cache_diff_k[:, l].reshape(bs * past, DIFF_W), cache_diff_v[:, l].reshape(bs * past, DIFF_W),
               cache_mla_ckv[:, l].reshape(bs * past, MLA_KV_LORA), cache_mla_krope[:, l].reshape(bs * past, MLA_ROPE))
        y_s, _ = _trunk_layer(y_s, mods_s, w, lam_init, bs, ls, tabs_s, ctx)

    stack = lambda j, shape: jnp.stack([n[j].reshape((bp, lp) + shape) for n in new_ctx], axis=1)
    return (y_p.reshape(bp, lp, d), y_s.reshape(bs, ls, d),
            stack(0, (DIFF_HEADS, 2, DIFF_HD)), stack(1, (DIFF_HEADS, 2 * DIFF_HD)),
            stack(2, (MLA_KV_LORA,)), stack(3, (MLA_ROPE,)))
```

```python
import functools
import math

import jax
import jax.numpy as jnp
import numpy as np
from jax import lax
from jax.experimental import pallas as pl
from jax.experimental.pallas import tpu as pltpu

F32, BF16, I32, U32 = jnp.float32, jnp.bfloat16, jnp.int32, jnp.uint32

GRID_W = 64
ROPE_BASE = 10000.0
EPS = 1e-6
N_ADA = 6
FOURIER_GROUPS = 4
FOURIER_CH = 128
FOURIER_W = FOURIER_GROUPS * FOURIER_CH
DIFF_HEADS = 4
DIFF_HD = 64
DIFF_W = DIFF_HEADS * 2 * DIFF_HD
MLA_HEADS = 8
MLA_NOPE = 64
MLA_ROPE = 32
MLA_QK_HD = MLA_NOPE + MLA_ROPE
MLA_V = 64
MLA_Q_LORA = 384
MLA_KV_LORA = 256
N_EXPERTS = 16
EC_FACTOR = 2

LANES = 128
SUBLANES = 8
VMEM_LIMIT_BYTES = 56 * 1024 * 1024
LOG2E = math.log2(math.e)
MLA_HEAD_PAD = LANES
MLA_PAD_W = MLA_HEADS * MLA_HEAD_PAD
COND_ROWS = 16


def _cparams(n_axes, **kw):
    return pltpu.CompilerParams(dimension_semantics=("arbitrary",) * n_axes,
                                vmem_limit_bytes=VMEM_LIMIT_BYTES, **kw)


def _dot(a, b):
    return jnp.dot(a, b, preferred_element_type=F32)


def _dot_nt(a, b):
    return lax.dot_general(a, b, (((1,), (1,)), ((), ())), preferred_element_type=F32)


def _pick_tile(n, target):
    t = min(n, target)
    while n % t:
        t //= 2
    return t


def _pack_bf16_pairs(x):
    half = x.shape[1] // 2
    bits = pltpu.bitcast(x.astype(jnp.bfloat16).astype(F32), U32)
    return (bits[:, :half] >> 16) | bits[:, half:]


def _store_token_tiles(ref, packed):
    rows, w = packed.shape
    n = w // LANES
    for c in range(n):
        ref[pl.ds(c, rows, stride=n), :] = packed[:, c * LANES:(c + 1) * LANES]


def _unpack_rows(load_chunk, nchunk):
    lo, hi = [], []
    for c in range(nchunk):
        wds = load_chunk(c)
        lo.append(pltpu.bitcast(wds << 16, F32).astype(BF16))
        hi.append(pltpu.bitcast(wds & jnp.uint32(0xFFFF0000), F32).astype(BF16))
    return jnp.concatenate(lo + hi, axis=1)


def _adaln_body(c_ref, w_ref, b_ref, o_ref):
    c = c_ref[...]
    a = (c * jax.nn.sigmoid(c)).astype(BF16)
    o_ref[...] = _dot(a, w_ref[...].astype(BF16)) + b_ref[...]


def adaln(cond, w_ada, b_ada):
    depth, d, n = w_ada.shape
    tn = _pick_tile(n, 1024)
    return pl.pallas_call(
        _adaln_body,
        grid=(depth, n // tn),
        in_specs=[pl.BlockSpec((COND_ROWS, d), lambda l, j: (0, 0)),
                  pl.BlockSpec((None, d, tn), lambda l, j: (l, 0, j)),
                  pl.BlockSpec((None, 1, tn), lambda l, j: (l, 0, j))],
        out_specs=pl.BlockSpec((None, COND_ROWS, tn), lambda l, j: (l, 0, j)),
        out_shape=jax.ShapeDtypeStruct((depth, COND_ROWS, n), F32),
        compiler_params=_cparams(2),
    )(cond, w_ada, b_ada.reshape(depth, 1, n))


def _norm_mod(x, nw, sc, sh):
    r = lax.rsqrt(jnp.mean(x * x, axis=-1, keepdims=True) + EPS)
    return (x * r) * nw * (1.0 + sc) + sh


def _mod_index(nb, tiles_per_batch):
    if nb == 1:
        return lambda i, *_: (0, 0, 0)
    return lambda i, *_: (i // tiles_per_batch, 0, 0)


def _gates_body(x_ref, nw_ref, sc_ref, sh_ref, w_ref, h_ref, o_ref):
    @pl.when(pl.program_id(1) == 0)
    def _():
        h_ref[...] = _norm_mod(x_ref[...], nw_ref[...], sc_ref[...], sh_ref[...]).astype(BF16)

    o_ref[...] = jax.nn.sigmoid(_dot(h_ref[...], w_ref[...])).astype(BF16)


def norm_gates_proj(x, nw, sc, sh, w, seq):
    t, d = x.shape
    n = w.shape[1]
    nb = sc.shape[0]
    tm, tn = _pick_tile(seq if nb > 1 else t, 1024), _pick_tile(n, 1024)
    mod_spec = pl.BlockSpec((None, 1, d), _mod_index(nb, seq // tm))
    return pl.pallas_call(
        _gates_body,
        grid=(t // tm, n // tn),
        in_specs=[pl.BlockSpec((tm, d), lambda i, j: (i, 0)),
                  pl.BlockSpec((1, d), lambda i, j: (0, 0)),
                  mod_spec, mod_spec,
                  pl.BlockSpec((d, tn), lambda i, j: (0, j))],
        out_specs=[pl.BlockSpec((tm, d), lambda i, j: (i, 0)),
                   pl.BlockSpec((tm, tn), lambda i, j: (i, j))],
        out_shape=[jax.ShapeDtypeStruct((t, d), BF16), jax.ShapeDtypeStruct((t, n), BF16)],
        compiler_params=_cparams(2),
    )(x, nw, sc, sh, w)


def _fourier_ch_body(h_ref, w_ref, cs_ref, o_ref):
    u = _dot(h_ref[...], w_ref[...]).astype(BF16)
    for g in range(FOURIER_GROUPS):
        ab = _dot(u[:, g * FOURIER_CH:(g + 1) * FOURIER_CH], cs_ref[...])
        o_ref[:, g * FOURIER_CH:(g + 1) * FOURIER_CH] = ab[:, :FOURIER_CH].astype(BF16)
        o_ref[:, FOURIER_W + g * FOURIER_CH:FOURIER_W + (g + 1) * FOURIER_CH] = ab[:, FOURIER_CH:].astype(BF16)


def fourier_channel(h, w, cs):
    t, d = h.shape
    tm = _pick_tile(t, 1024)
    return pl.pallas_call(
        _fourier_ch_body,
        grid=(t // tm,),
        in_specs=[pl.BlockSpec((tm, d), lambda i: (i, 0)),
                  pl.BlockSpec((d, FOURIER_W), lambda i: (0, 0)),
                  pl.BlockSpec((FOURIER_CH, 2 * FOURIER_CH), lambda i: (0, 0))],
        out_specs=pl.BlockSpec((tm, 2 * FOURIER_W), lambda i: (i, 0)),
        out_shape=jax.ShapeDtypeStruct((t, 2 * FOURIER_W), BF16),
        compiler_params=_cparams(1),
    )(h, w, cs)


def _fourier_pos_body(c_ref, s_ref, ab_ref, o_ref):
    y = _dot(c_ref[...], ab_ref[:, :FOURIER_W]) - _dot(s_ref[...], ab_ref[:, FOURIER_W:])
    o_ref[...] = y.astype(BF16)


def fourier_position(ab, cpos, spos, seq):
    t = ab.shape[0]
    tr = _pick_tile(seq, 512)
    nr = seq // tr
    return pl.pallas_call(
        _fourier_pos_body,
        grid=(t // seq, nr),
        in_specs=[pl.BlockSpec((tr, seq), lambda b, r: (r, 0)),
                  pl.BlockSpec((tr, seq), lambda b, r: (r, 0)),
                  pl.BlockSpec((seq, 2 * FOURIER_W), lambda b, r: (b, 0))],
        out_specs=pl.BlockSpec((tr, FOURIER_W), lambda b, r: (b * nr + r, 0)),
        out_shape=jax.ShapeDtypeStruct((t, FOURIER_W), BF16),
        compiler_params=_cparams(2),
    )(cpos, spos, ab)


def _rope(x, cos, sin, off):
    w = x.shape[1]
    lane = lax.broadcasted_iota(I32, (1, w), 1)
    first = (lane & off) == 0
    partner = jnp.where(first, pltpu.roll(x, w - off, 1), pltpu.roll(x, off, 1))
    return x * cos + partner * sin


def _half_tile_norm(x):
    outs = []
    lane = lax.broadcasted_iota(I32, (1, LANES), 1)
    low = lane < DIFF_HD
    for j in range(x.shape[1] // LANES):
        seg = x[:, j * LANES:(j + 1) * LANES]
        sq = seg * seg
        s_lo = jnp.sum(jnp.where(low, sq, 0.0), axis=-1, keepdims=True)
        s_hi = jnp.sum(jnp.where(low, 0.0, sq), axis=-1, keepdims=True)
        ms = jnp.where(low, s_lo, s_hi) * (1.0 / DIFF_HD)
        outs.append(seg * lax.rsqrt(ms + EPS))
    return jnp.concatenate(outs, axis=1)


def _diff_qkv_body(*refs, rope):
    if rope:
        h_ref, w_ref, qn_ref, kn_ref, cos_ref, sin_ref, q_out, k_out, v_out = refs
    else:
        h_ref, w_ref, qn_ref, kn_ref, q_out, k_out, v_out, k32_out, v32_out = refs
    z = _dot(h_ref[...], w_ref[...])
    q = _half_tile_norm(z[:, :DIFF_W]) * qn_ref[...]
    k = _half_tile_norm(z[:, DIFF_W:2 * DIFF_W]) * kn_ref[...]
    v = z[:, 2 * DIFF_W:]
    if rope:
        q = _rope(q, cos_ref[...], sin_ref[...], DIFF_HD // 4)
        k = _rope(k, cos_ref[...], sin_ref[...], DIFF_HD // 4)
    else:
        k32_out[...] = k
        v32_out[...] = v
    q_out[...] = (q * (DIFF_HD ** -0.5 * LOG2E)).astype(BF16)
    k_out[...] = k.astype(BF16)
    v_out[...] = v.astype(BF16)


def diff_qkv(h, w, qn, kn, seq, rope_tabs):
    t, d = h.shape
    rope = rope_tabs is not None
    tm = _pick_tile(seq, 512)
    row = lambda i: (i, 0)
    const = lambda i: (0, 0)
    in_specs = [pl.BlockSpec((tm, d), row), pl.BlockSpec((d, 3 * DIFF_W), const),
                pl.BlockSpec((1, DIFF_W), const), pl.BlockSpec((1, DIFF_W), const)]
    args = [h, w, qn, kn]
    out_shape = [jax.ShapeDtypeStruct((t, DIFF_W), BF16)] * 3
    if rope:
        nt = seq // tm
        tab = pl.BlockSpec((tm, DIFF_W), lambda i: (i % nt, 0))
        in_specs += [tab, tab]
        args += list(rope_tabs)
    else:
        out_shape = out_shape + [jax.ShapeDtypeStruct((t, DIFF_W), F32)] * 2
    return pl.pallas_call(
        functools.partial(_diff_qkv_body, rope=rope),
        grid=(t // tm,),
        in_specs=in_specs,
        out_specs=[pl.BlockSpec((tm, DIFF_W), row)] * len(out_shape),
        out_shape=out_shape,
        compiler_params=_cparams(1),
    )(*args)


def _mla_keys(ckv_bf, krb, wk_ref, wv_ref, kn_ref, rope_tab):
    kn = _dot(ckv_bf, wk_ref[...])
    gain = kn_ref[...]
    kr = krb * gain
    if rope_tab is not None:
        kr = _rope(kr, rope_tab[0], rope_tab[1], MLA_ROPE // 4)
    kr_sq = jnp.sum(krb * krb, axis=-1, keepdims=True)
    outs = []
    for j in range(MLA_HEADS):
        seg = kn[:, j * LANES:(j + 1) * LANES]
        ms = (jnp.sum(seg * seg, axis=-1, keepdims=True) + kr_sq) * (1.0 / MLA_QK_HD)
        outs.append((seg * gain + kr) * lax.rsqrt(ms + EPS))
    return jnp.concatenate(outs, axis=1), _dot(ckv_bf, wv_ref[...])


def _mla_proj_body(*refs, rope):
    (h_ref, w_ref, qan_ref, wqb_ref, qn_ref, kvan_ref, wk_ref, wv_ref, kn_ref), refs = refs[:9], refs[9:]
    if rope:
        cos_ref, sin_ref, q_out, k_out, v_out = refs
        rope_tab = (cos_ref[...], sin_ref[...])
    else:
        q_out, k_out, v_out, ckv32_out, kr32_out = refs
        rope_tab = None
    z = _dot(h_ref[...], w_ref[...])
    q_a = z[:, :MLA_Q_LORA]
    kv_a = z[:, MLA_Q_LORA:MLA_Q_LORA + MLA_KV_LORA]
    krb = z[:, MLA_Q_LORA + MLA_KV_LORA:]
    c_q = q_a * lax.rsqrt(jnp.mean(q_a * q_a, axis=-1, keepdims=True) + EPS) * qan_ref[...]
    zq = _dot(c_q.astype(BF16), wqb_ref[...])
    gain = qn_ref[...]
    outs = []
    for j in range(MLA_HEADS):
        seg = zq[:, j * LANES:(j + 1) * LANES]
        ms = jnp.sum(seg * seg, axis=-1, keepdims=True) * (1.0 / MLA_QK_HD)
        val = seg * gain
        if rope:
            val = val * rope_tab[0] + zq[:, MLA_PAD_W + j * LANES:MLA_PAD_W + (j + 1) * LANES] * rope_tab[1]
        outs.append(val * (lax.rsqrt(ms + EPS) * (MLA_QK_HD ** -0.5 * LOG2E)))
    q_out[...] = jnp.concatenate(outs, axis=1).astype(BF16)
    c_kv = kv_a * lax.rsqrt(jnp.mean(kv_a * kv_a, axis=-1, keepdims=True) + EPS) * kvan_ref[...]
    k, v = _mla_keys(c_kv.astype(BF16), krb, wk_ref, wv_ref, kn_ref, rope_tab)
    if not rope:
        ckv32_out[...] = c_kv
        kr32_out[...] = krb
    k_out[...] = k.astype(BF16)
    v_out[...] = v.astype(BF16)


def mla_proj(h, w, p, seq, rope_tabs):
    t, d = h.shape
    rope = rope_tabs is not None
    tm = _pick_tile(seq, 512)
    row = lambda i: (i, 0)
    const = lambda i: (0, 0)
    wcols = MLA_Q_LORA + MLA_KV_LORA + LANES
    w_qb = p["w_qb_rope"] if rope else p["w_qb"]
    in_specs = [pl.BlockSpec((tm, d), row), pl.BlockSpec((d, wcols), const),
                pl.BlockSpec((1, MLA_Q_LORA), const), pl.BlockSpec(w_qb.shape, const),
                pl.BlockSpec((1, LANES), const), pl.BlockSpec((1, MLA_KV_LORA), const),
                pl.BlockSpec((MLA_KV_LORA, MLA_PAD_W), const), pl.BlockSpec((MLA_KV_LORA, MLA_PAD_W), const),
                pl.BlockSpec((1, LANES), const)]
    args = [h, w, p["qa_norm"], w_qb, p["q_norm"], p["kva_norm"], p["w_k"], p["w_v"], p["k_norm"]]
    out_shape = [jax.ShapeDtypeStruct((t, MLA_PAD_W), BF16)] * 3
    out_specs = [pl.BlockSpec((tm, MLA_PAD_W), row)] * 3
    if rope:
        nt = seq // tm
        tab = pl.BlockSpec((tm, LANES), lambda i: (i % nt, 0))
        in_specs += [tab, tab]
        args += list(rope_tabs)
    else:
        out_shape += [jax.ShapeDtypeStruct((t, MLA_KV_LORA), F32), jax.ShapeDtypeStruct((t, LANES), F32)]
        out_specs += [pl.BlockSpec((tm, MLA_KV_LORA), row), pl.BlockSpec((tm, LANES), row)]
    return pl.pallas_call(
        functools.partial(_mla_proj_body, rope=rope),
        grid=(t // tm,),
        in_specs=in_specs,
        out_specs=out_specs,
        out_shape=out_shape,
        compiler_params=_cparams(1),
    )(*args)


def _mla_ctx_body(ckv_ref, krb_ref, wk_ref, wv_ref, kn_ref, k_out, v_out):
    k, v = _mla_keys(ckv_ref[...].astype(BF16), krb_ref[...], wk_ref, wv_ref, kn_ref, None)
    k_out[...] = k.astype(BF16)
    v_out[...] = v.astype(BF16)


def mla_ctx_keys(ckv, krb, p):
    t = ckv.shape[0]
    tm = _pick_tile(t, 512)
    row = lambda i: (i, 0)
    const = lambda i: (0, 0)
    return pl.pallas_call(
        _mla_ctx_body,
        grid=(t // tm,),
        in_specs=[pl.BlockSpec((tm, MLA_KV_LORA), row), pl.BlockSpec((tm, LANES), row),
                  pl.BlockSpec((MLA_KV_LORA, MLA_PAD_W), const), pl.BlockSpec((MLA_KV_LORA, MLA_PAD_W), const),
                  pl.BlockSpec((1, LANES), const)],
        out_specs=[pl.BlockSpec((tm, MLA_PAD_W), row)] * 2,
        out_shape=[jax.ShapeDtypeStruct((t, MLA_PAD_W), BF16)] * 2,
        compiler_params=_cparams(1),
    )(ckv, krb, p["w_k"], p["w_v"], p["k_norm"])


SOFTMAX_SUM_FLOOR = 2.0 ** -100


def _attn_body(*refs, nmaps, hp, has_ctx, subln, compact, tk, lam_init):
    refs = list(refs)
    kmax_scr = refs.pop()
    lamv_ref = refs.pop(0) if nmaps == 2 else None
    q_ref, k_ref, v_ref = refs[:3]
    refs = refs[3:]
    segments = []
    if has_ctx:
        segments.append((refs[0], refs[1]))
        refs = refs[2:]
    segments.append((k_ref, v_ref))
    subln_ref = refs.pop(0) if subln else None
    o_ref = refs[0]
    tq = q_ref.shape[0]
    lane = lax.broadcasted_iota(I32, (1, LANES), 1)
    masks = [lane < DIFF_HD, lane >= DIFF_HD] if nmaps == 2 else [None]

    def cols(hh):
        return slice(hh * LANES, (hh + 1) * LANES)

    @pl.when(pl.program_id(2) == 0)
    def _():
        for hh in range(hp):
            for i, mk in enumerate(masks):
                best = jnp.zeros((1, 1), F32)
                for kr, _ in segments:
                    kf = kr[:, cols(hh)].astype(F32)
                    sq = kf * kf if mk is None else jnp.where(mk, kf * kf, 0.0)
                    best = jnp.maximum(best, jnp.max(jnp.sum(sq, axis=-1, keepdims=True), axis=0, keepdims=True))
                row = hh * nmaps + i
                kmax_scr[row:row + 1, :] = jnp.broadcast_to(best, (1, LANES))

    def queries(hh):
        q = q_ref[:, cols(hh)]
        return [q if mk is None else jnp.where(mk, q, jnp.zeros_like(q)) for mk in masks]

    def key_chunks(hh):
        for kr, vr in segments:
            n = kr.shape[0]
            for c0 in range(0, n, tk):
                c1 = min(n, c0 + tk)
                yield kr[c0:c1, cols(hh)], vr[c0:c1, cols(hh)]

    def shifted(hh):
        acc, l = [], []
        for i, qi in enumerate(queries(hh)):
            qf = qi.astype(F32)
            row = hh * nmaps + i
            shift = jnp.sqrt(jnp.sum(qf * qf, axis=-1, keepdims=True) * kmax_scr[row:row + 1, 0:1])
            a = jnp.zeros((tq, LANES), F32)
            li = jnp.zeros((tq, 1), F32)
            for kc, vc in key_chunks(hh):
                p = jnp.exp2(_dot_nt(qi, kc) - shift)
                li = li + jnp.sum(p, axis=-1, keepdims=True)
                a = a + _dot(p.astype(BF16), vc)
            acc.append(a)
            l.append(li)
        return acc, l

    def running_max(hh):
        qs = queries(hh)
        m = [jnp.full((tq, 1), -jnp.inf, F32) for _ in qs]
        l = [jnp.zeros((tq, 1), F32) for _ in qs]
        acc = [jnp.zeros((tq, LANES), F32) for _ in qs]
        for kc, vc in key_chunks(hh):
            for i, qi in enumerate(qs):
                s = _dot_nt(qi, kc)
                mn = jnp.maximum(m[i], jnp.max(s, axis=-1, keepdims=True))
                alpha = jnp.exp2(m[i] - mn)
                p = jnp.exp2(s - mn)
                l[i] = alpha * l[i] + jnp.sum(p, axis=-1, keepdims=True)
                acc[i] = alpha * acc[i] + _dot(p.astype(BF16), vc)
                m[i] = mn
        return acc, l

    def head_out(acc, l):
        o = acc[0] / l[0]
        if nmaps == 2:
            lv = lamv_ref[...]
            lam = (jnp.exp(jnp.sum(lv[0:1] * lv[1:2], axis=-1, keepdims=True))
                   - jnp.exp(jnp.sum(lv[2:3] * lv[3:4], axis=-1, keepdims=True)) + lam_init)
            o = o - lam * (acc[1] / l[1])
        if subln:
            o = o * lax.rsqrt(jnp.mean(o * o, axis=-1, keepdims=True) + EPS) * subln_ref[...] * (1.0 - lam_init)
        return o

    def write(outs):
        if compact:
            for j in range(hp // 2):
                pair = jnp.where(lane < LANES // 2, outs[2 * j], pltpu.roll(outs[2 * j + 1], LANES // 2, 1))
                o_ref[:, cols(j)] = pair.astype(BF16)
        else:
            for hh in range(hp):
                o_ref[:, cols(hh)] = outs[hh].astype(BF16)

    results = [shifted(hh) for hh in range(hp)]
    write([head_out(acc, l) for acc, l in results])

    lmin = functools.reduce(jnp.minimum, [jnp.min(li) for _, l in results for li in l])

    @pl.when(lmin < SOFTMAX_SUM_FLOOR)
    def _():
        write([head_out(*running_max(hh)) for hh in range(hp)])


def attention(q, k, v, ctx, *, nbatch, seq, heads, nmaps, compact=False, lamv=None, subln=None, lam_init=0.0):
    t, w = q.shape
    tq = _pick_tile(seq, 1024)
    nq = seq // tq
    hp = heads if seq <= 512 else (2 if compact else 1)
    assert hp * nmaps <= SUBLANES and heads % hp == 0
    wo = hp * LANES // 2 if compact else hp * LANES
    qspec = pl.BlockSpec((tq, hp * LANES), lambda b, h, i: (b * nq + i, h))
    kvspec = pl.BlockSpec((seq, hp * LANES), lambda b, h, i: (b, h))
    in_specs, args = [], []
    if nmaps == 2:
        in_specs.append(pl.BlockSpec((SUBLANES, LANES), lambda b, h, i: (0, 0)))
        args.append(lamv)
    in_specs += [qspec, kvspec, kvspec]
    args += [q, k, v]
    if ctx is not None:
        past = ctx[0].shape[0] // nbatch
        cspec = pl.BlockSpec((past, hp * LANES), lambda b, h, i: (b, h))
        in_specs += [cspec, cspec]
        args += list(ctx)
    if subln is not None:
        in_specs.append(pl.BlockSpec((1, LANES), lambda b, h, i: (0, 0)))
        args.append(subln)
    return pl.pallas_call(
        functools.partial(_attn_body, nmaps=nmaps, hp=hp, has_ctx=ctx is not None, subln=subln is not None,
                          compact=compact, tk=512, lam_init=lam_init),
        grid=(nbatch, heads // hp, nq),
        in_specs=in_specs,
        out_specs=pl.BlockSpec((tq, wo), lambda b, h, i: (b * nq + i, h)),
        out_shape=jax.ShapeDtypeStruct((t, w // 2 if compact else w), BF16),
        scratch_shapes=[pltpu.VMEM((SUBLANES, LANES), F32)],
        compiler_params=_cparams(3),
    )(*args)


def _merge_body(x_ref, f_ref, od_ref, om_ref, g0_ref, g1_ref, g2_ref, wf_ref, wd_ref, wm_ref, wo_ref,
                gate1_ref, nw_ref, sc_ref, sh_ref, wrh_ref, wrl_ref,
                x1_ref, h2p_ref, aff_ref):
    merged = g0_ref[...].astype(F32) * _dot(f_ref[...], wf_ref[...])
    merged = merged + g1_ref[...].astype(F32) * _dot(od_ref[...], wd_ref[...])
    merged = merged + g2_ref[...].astype(F32) * _dot(om_ref[...], wm_ref[...])
    x1 = x_ref[...] + gate1_ref[...] * _dot(merged.astype(BF16), wo_ref[...])
    x1_ref[...] = x1
    h2 = _norm_mod(x1, nw_ref[...], sc_ref[...], sh_ref[...])
    _store_token_tiles(h2p_ref, _pack_bf16_pairs(h2))
    h_hi = h2.astype(jnp.bfloat16)
    h_lo = (h2 - h_hi.astype(F32)).astype(BF16)
    hi_terms = _dot(h_hi, wrl_ref[...])
    logits = hi_terms[:, :LANES] + hi_terms[:, LANES:] + _dot(h_lo, wrh_ref[...])
    lane = lax.broadcasted_iota(I32, (1, LANES), 1)
    logits = jnp.where(lane < N_EXPERTS, logits, -1e30)
    e = jnp.exp(logits - jnp.max(logits, axis=-1, keepdims=True))
    aff = e / jnp.sum(e, axis=-1, keepdims=True)
    aff_ref[...] = aff.T[:N_EXPERTS, :]


def merge_out(x, f, od, om, gates, w, gate1, nw, sc, sh, seq):
    t, d = x.shape
    tm = _pick_tile(seq, 256)
    nchunk = d // 2 // LANES
    row = lambda i: (i, 0)
    const = lambda i: (0, 0)
    once = pl.Buffered(1)
    mod_spec = pl.BlockSpec((None, 1, d), _mod_index(sc.shape[0], seq // tm))
    wspec = lambda a: pl.BlockSpec(a.shape, const, pipeline_mode=once)
    in_specs = [pl.BlockSpec((tm, d), row),
                pl.BlockSpec((tm, f.shape[1]), row), pl.BlockSpec((tm, od.shape[1]), row),
                pl.BlockSpec((tm, om.shape[1]), row),
                pl.BlockSpec((tm, d), lambda i: (i, 0)), pl.BlockSpec((tm, d), lambda i: (i, 1)),
                pl.BlockSpec((tm, d), lambda i: (i, 2)),
                wspec(w["br_f"]), wspec(w["br_d"]), wspec(w["br_m"]), wspec(w["out"]),
                mod_spec, pl.BlockSpec((1, d), const), mod_spec, mod_spec,
                wspec(w["router_hi"]), wspec(w["router_hi_lo"])]
    return pl.pallas_call(
        _merge_body,
        grid=(t // tm,),
        in_specs=in_specs,
        out_specs=[pl.BlockSpec((tm, d), row),
                   pl.BlockSpec((tm * nchunk, LANES), row),
                   pl.BlockSpec((N_EXPERTS, tm), lambda i: (0, i))],
        out_shape=[jax.ShapeDtypeStruct((t, d), F32),
                   jax.ShapeDtypeStruct((t * nchunk, LANES), U32),
                   jax.ShapeDtypeStruct((N_EXPERTS, t), F32)],
        compiler_params=_cparams(1),
    )(x, f, od, om, gates, gates, gates, w["br_f"], w["br_d"], w["br_m"], w["out"],
      gate1, nw, sc, sh, w["router_hi"], w["router_hi_lo"])


def _cumsum_lanes(x):
    n = x.shape[1]
    lane = lax.broadcasted_iota(I32, (1, n), 1)
    s = 1
    while s < n:
        x = x + jnp.where(lane >= s, pltpu.roll(x, s, 1), 0)
        s *= 2
    return x


ROUTE_MATMUL_MIN_TOKENS = 1024


def _route_body(aff_ref, idx_ref, dst_ref, gv_ref, seg_ref, key_scr, dst_scr, *, cap, row_chunk):
    b = pl.program_id(0)
    aff = aff_ref[...]
    ne, n = aff.shape
    bits = pltpu.bitcast(aff, I32)

    def search(i, thr):
        cand = thr | (1 << (30 - i))
        cnt = jnp.sum(jnp.where(bits >= cand, 1.0, 0.0), axis=-1, keepdims=True)
        return jnp.where(cnt >= cap, cand, thr)

    thr = lax.fori_loop(0, 31, search, jnp.zeros((ne, 1), I32))
    gt = bits > thr
    eq = (bits == thr).astype(I32)
    need = cap - jnp.sum(jnp.where(gt, 1.0, 0.0), axis=-1, keepdims=True).astype(I32)
    eq_before = _cumsum_lanes(eq) - eq
    sel = jnp.where(gt | ((eq > 0) & (eq_before < need)), 1, 0)
    cum = _cumsum_lanes(sel)
    key_scr[...] = sel * cum

    before = jnp.zeros((1, n), I32)
    for e in range(ne):
        dst_scr[e:e + 1, :] = before
        before = before + sel[e:e + 1, :]
    k_tok = before
    start = _cumsum_lanes(k_tok) - k_tok + b * (ne * cap)
    dst_scr[...] = dst_scr[...] + start
    seg_ref[0:1, :] = start
    seg_ref[1:2, :] = start + k_tok
    seg_ref[2:SUBLANES, :] = jnp.zeros((SUBLANES - 2, n), I32)

    tok = lax.broadcasted_iota(I32, (1, n), 1) + b * n
    lane = lax.broadcasted_iota(I32, (1, LANES), 1)
    idx_ref[...] = jnp.zeros(idx_ref.shape, I32)
    dst_ref[...] = jnp.zeros(dst_ref.shape, I32)
    gv_ref[...] = jnp.zeros(gv_ref.shape, F32)
    byte = lambda v, shift: ((v >> shift) & 255).astype(F32)

    def per_expert(e, carry):
        key = key_scr[pl.ds(e, 1), :]
        dst = dst_scr[pl.ds(e, 1), :]
        af = aff_ref[pl.ds(e, 1), :]
        a_hi = af.astype(jnp.bfloat16).astype(F32)
        a_mid = (af - a_hi).astype(jnp.bfloat16).astype(F32)
        a_lo = af - a_hi - a_mid
        vals = jnp.concatenate([byte(tok, 16), byte(tok, 8), byte(tok, 0),
                                byte(dst, 16), byte(dst, 8), byte(dst, 0),
                                a_hi, a_mid, a_lo] + [jnp.zeros((1, n), F32)] * 7, axis=0).astype(BF16)
        for c0 in range(0, cap, row_chunk):
            slot = lax.broadcasted_iota(I32, (row_chunk, 1), 0) + (c0 + 1)
            hit = key == slot
            if n >= ROUTE_MATMUL_MIN_TOKENS:
                got = _dot_nt(jnp.where(hit, 1.0, 0.0).astype(BF16), vals)
                word = lambda j: (got[:, j:j + 1].astype(I32) * 65536 + got[:, j + 1:j + 2].astype(I32) * 256
                                  + got[:, j + 2:j + 3].astype(I32))
                i_col, d_col = word(0), word(3)
                g_col = got[:, 6:7] + got[:, 7:8] + got[:, 8:9]
            else:
                pick = lambda v: jnp.sum(jnp.where(hit, v, 0.0), axis=-1, keepdims=True)
                i_col, d_col = pick(tok.astype(F32)).astype(I32), pick(dst.astype(F32)).astype(I32)
                g_col = pick(af)
            rows = pl.ds(c0, row_chunk)
            idx_ref[rows, :] = jnp.where(lane == e, i_col, idx_ref[rows, :])
            dst_ref[rows, :] = jnp.where(lane == e, d_col, dst_ref[rows, :])
            gv_ref[rows, :] = jnp.where(lane == e, g_col, gv_ref[rows, :])
        return carry

    lax.fori_loop(0, ne, per_expert, 0)


def route(aff_t, nbatch, seq):
    ne = aff_t.shape[0]
    cap = EC_FACTOR * seq // ne
    row_chunk = min(cap, 256)
    tab = lambda dt: jax.ShapeDtypeStruct((nbatch, cap, LANES), dt)
    tspec = pl.BlockSpec((None, cap, LANES), lambda b: (b, 0, 0))
    return pl.pallas_call(
        functools.partial(_route_body, cap=cap, row_chunk=row_chunk),
        grid=(nbatch,),
        in_specs=[pl.BlockSpec((ne, seq), lambda b: (0, b))],
        out_specs=[tspec, tspec, tspec, pl.BlockSpec((None, SUBLANES, seq), lambda b: (b, 0, 0))],
        out_shape=[tab(I32), tab(I32), tab(F32), jax.ShapeDtypeStruct((nbatch, SUBLANES, seq), I32)],
        scratch_shapes=[pltpu.VMEM((ne, seq), I32), pltpu.VMEM((ne, seq), I32)],
        compiler_params=_cparams(1),
    )(aff_t)


DMA_UNROLL = 8


def _moe_body(idx_a, idx_b, idx_a_next, dst_a, dst_b, dst_b_prev, gv_a, gv_b, h2p_ref, wg_ref, wu_ref, wd_ref, g_ref,
              xa, xb, ya, yb, sem, *, rt, d):
    nw = d // 2 // LANES
    step = pl.program_id(0) * pl.num_programs(1) + pl.program_id(1)
    last = pl.num_programs(0) * pl.num_programs(1) - 1
    in_a, in_b, out_a, out_b = (sem.at[i] for i in range(4))

    def row_tile(r):
        return pl.ds(r * nw if isinstance(r, int) else pl.multiple_of(r * nw, nw), nw)

    def looped(fn):
        def body(i, carry):
            for u in range(DMA_UNROLL):
                fn(i * DMA_UNROLL + u)
            return carry
        lax.fori_loop(0, rt // DMA_UNROLL, body, 0)

    def inline(fn):
        for r in range(rt):
            fn(r)

    def gather(idx_ref, r, xbuf, s):
        src = 0 if idx_ref is None else idx_ref[0, r]
        return pltpu.make_async_copy(h2p_ref.at[row_tile(src), :], xbuf.at[row_tile(r), :], s)

    def scatter(ybuf, r, dst_ref, s):
        dst = 0 if dst_ref is None else dst_ref[0, r]
        return pltpu.make_async_copy(ybuf.at[row_tile(r), :], g_ref.at[row_tile(dst), :], s)

    def drain_gather(xbuf, s):
        looped(lambda r: gather(None, r, xbuf, s).wait())

    def drain_scatter(ybuf, s):
        looped(lambda r: scatter(ybuf, r, None, s).wait())

    def ffn(xbuf, gv_ref):
        x = _unpack_rows(lambda c: xbuf[pl.ds(c, rt, stride=nw), :], nw)
        a = _dot(x, wg_ref[...])
        u = _dot(x, wu_ref[...])
        mid = (a * jax.nn.sigmoid(a) * u).astype(BF16)
        return _pack_bf16_pairs(_dot(mid, wd_ref[...]) * gv_ref[...])

    @pl.when(step == 0)
    def _():
        looped(lambda r: gather(idx_a, r, xa, in_a).start())
        yb[...] = jnp.zeros(yb.shape, U32)

    drain_gather(xa, in_a)
    inline(lambda r: gather(idx_b, r, xb, in_b).start())
    inline(lambda r: scatter(yb, r, dst_b_prev, out_b).start())
    y = ffn(xa, gv_a)

    @pl.when(step > 0)
    def _():
        drain_scatter(ya, out_a)

    _store_token_tiles(ya, y)

    drain_gather(xb, in_b)
    inline(lambda r: gather(idx_a_next, r, xa, in_a).start())
    inline(lambda r: scatter(ya, r, dst_a, out_a).start())
    y = ffn(xb, gv_b)
    drain_scatter(yb, out_b)
    _store_token_tiles(yb, y)

    @pl.when(step == last)
    def _():
        looped(lambda r: scatter(yb, r, dst_b, out_b).start())
        drain_gather(xa, in_a)
        drain_scatter(ya, out_a)
        drain_scatter(yb, out_b)


def moe_experts(idx, dst, gv, h2p, wg, wu, wd, layer, d):
    _, ne, _, ff = wg.shape
    rt = idx.shape[2]
    ntiles = idx.shape[0]
    pairs = ntiles // ne // 2
    nw = d // 2 // LANES
    tile_a = lambda e, j: 2 * (e * pairs + j)
    smem = lambda tile: pl.BlockSpec((None, 1, rt), lambda e, j: (tile(e, j), 0, 0), memory_space=pltpu.SMEM)
    vmem_col = lambda tile: pl.BlockSpec((rt, 1), lambda e, j: (tile(e, j), 0))
    tile_b = lambda e, j: tile_a(e, j) + 1
    tile_a_next = lambda e, j: jnp.minimum(tile_a(e, j) + 2, ntiles - 2)
    tile_b_prev = lambda e, j: jnp.maximum(tile_a(e, j) - 1, 1)
    wspec = lambda a: pl.BlockSpec((None, None) + a.shape[2:], lambda e, j: (layer, e, 0, 0))
    row_buf = pltpu.VMEM((rt * nw, LANES), U32)
    return pl.pallas_call(
        functools.partial(_moe_body, rt=rt, d=d),
        grid=(ne, pairs),
        in_specs=[smem(tile_a), smem(tile_b), smem(tile_a_next), smem(tile_a), smem(tile_b), smem(tile_b_prev),
                  vmem_col(tile_a), vmem_col(tile_b),
                  pl.BlockSpec(memory_space=pl.ANY), wspec(wg), wspec(wu), wspec(wd)],
        out_specs=pl.BlockSpec(memory_space=pl.ANY),
        out_shape=jax.ShapeDtypeStruct((ntiles * rt * nw, LANES), U32),
        scratch_shapes=[row_buf, row_buf, row_buf, row_buf, pltpu.SemaphoreType.DMA((4,))],
        compiler_params=_cparams(2, has_side_effects=True),
    )(idx, idx, idx, dst, dst, dst, gv, gv, h2p, wg, wu, wd)


COMBINE_WINDOW = 768


def _combine_body(tb_ref, x_ref, s0_ref, s1_ref, gate_ref, g_ref, o_ref, gbuf, sem, *, win, d, npairs):
    t = pl.program_id(0)
    nw = d // 2 // LANES
    first = tb_ref[t]
    last = tb_ref[t + 1]
    s0 = s0_ref[...]
    s1 = s1_ref[...]

    def window_start(lo):
        return jnp.minimum(lo, npairs - win)

    def window_copy(lo, slot):
        rows = pl.ds(pl.multiple_of(window_start(lo) * nw, nw), win * nw)
        return pltpu.make_async_copy(g_ref.at[rows, :], gbuf.at[slot], sem.at[slot])

    def window_sum(lo, slot):
        pos = lax.broadcasted_iota(I32, (1, win), 1) + window_start(lo)
        own = (pos >= s0) & (pos < s1) & (pos >= lo)
        sel = jnp.where(own, 1.0, 0.0).astype(BF16)
        rows = _unpack_rows(lambda c: gbuf[slot, pl.ds(c, win, stride=nw), :], nw)
        return _dot(sel, rows)

    slot = t % 2

    @pl.when(t == 0)
    def _():
        window_copy(first, slot).start()

    window_copy(first, slot).wait()

    @pl.when(t + 1 < pl.num_programs(0))
    def _():
        window_copy(last, 1 - slot).start()

    def extra_window(wi, acc):
        lo = first + wi * win
        cp = window_copy(lo, 2)
        cp.start()
        cp.wait()
        return acc + window_sum(lo, 2)

    nwin = (last - first + win - 1) // win
    acc = lax.fori_loop(1, nwin, extra_window, window_sum(first, slot))
    o_ref[...] = x_ref[...] + gate_ref[...] * acc


def combine(x1, seg0, seg1, tile_bounds, gate2, g, seq):
    t, d = x1.shape
    tm = _pick_tile(seq, 256)
    nw = d // 2 // LANES
    npairs = g.shape[0] // nw
    win = min(COMBINE_WINDOW, npairs)
    row = lambda i, tb: (i, 0)
    grid_spec = pltpu.PrefetchScalarGridSpec(
        num_scalar_prefetch=1,
        grid=(t // tm,),
        in_specs=[pl.BlockSpec((tm, d), row), pl.BlockSpec((tm, 1), row), pl.BlockSpec((tm, 1), row),
                  pl.BlockSpec((None, 1, d), _mod_index(gate2.shape[0], seq // tm)),
                  pl.BlockSpec(memory_space=pl.ANY)],
        out_specs=pl.BlockSpec((tm, d), row),
        scratch_shapes=[pltpu.VMEM((3, win * nw, LANES), U32), pltpu.SemaphoreType.DMA((3,))],
    )
    return pl.pallas_call(
        functools.partial(_combine_body, win=win, d=d, npairs=npairs),
        grid_spec=grid_spec,
        out_shape=jax.ShapeDtypeStruct((t, d), F32),
        compiler_params=_cparams(1),
    )(tile_bounds, x1, seg0, seg1, gate2, g)


def _dft_channel_table():
    k = np.arange(FOURIER_CH)
    ang = 2.0 * np.pi * ((k[:, None] * k[None, :]) % FOURIER_CH) / FOURIER_CH
    return jnp.asarray(np.concatenate([np.cos(ang), np.sin(ang)], axis=1), BF16)


def _dft_position_tables(seq):
    r = 1 << ((seq.bit_length() - 1) // 2)
    n = jnp.arange(seq, dtype=I32)[None, :]
    k1 = jnp.arange(seq // r, dtype=I32)[:, None] * r
    k2 = jnp.arange(r, dtype=I32)[:, None]
    ang = lambda k: ((k * n) % seq).astype(F32) * (2.0 * math.pi / seq)
    a, b = ang(k1), ang(k2)
    ca, sa = jnp.cos(a)[:, None, :], jnp.sin(a)[:, None, :]
    cb, sb = jnp.cos(b)[None, :, :], jnp.sin(b)[None, :, :]
    scale = 1.0 / math.sqrt(seq * FOURIER_CH)
    cos = ((ca * cb - sa * sb) * scale).astype(BF16).reshape(seq, seq)
    sin = ((sa * cb + ca * sb) * scale).astype(BF16).reshape(seq, seq)
    return cos, sin


def _rope_pattern(seq, width):
    nf = width // 4
    pos = np.arange(seq)
    inv = ROPE_BASE ** (-np.arange(nf, dtype=np.float64) / nf)
    ar = (pos // GRID_W)[:, None] * inv
    ac = (pos % GRID_W)[:, None] * inv
    cos = np.concatenate([np.cos(ar), np.cos(ar), np.cos(ac), np.cos(ac)], axis=1)
    sin = np.concatenate([-np.sin(ar), np.sin(ar), -np.sin(ac), np.sin(ac)], axis=1)
    return cos, sin


def _rope_tables(seq):
    cd, sd = _rope_pattern(seq, DIFF_HD)
    diff = (np.tile(cd, (1, 2 * DIFF_HEADS)), np.tile(sd, (1, 2 * DIFF_HEADS)))
    cm, sm = _rope_pattern(seq, MLA_ROPE)
    pad_r = MLA_HEAD_PAD - MLA_QK_HD
    cm = np.concatenate([np.ones((seq, MLA_NOPE)), cm, np.ones((seq, pad_r))], axis=1)
    sm = np.concatenate([np.zeros((seq, MLA_NOPE)), sm, np.zeros((seq, pad_r))], axis=1)
    mla = (cm, sm)
    as_f32 = lambda pair: tuple(jnp.asarray(a, F32) for a in pair)
    return as_f32(diff), as_f32(mla)


def _pad_heads(a, width):
    lead = a.shape[:-1]
    a = a.reshape(lead + (MLA_HEADS, width))
    a = jnp.pad(a, [(0, 0)] * len(lead) + [(0, 0), (0, MLA_HEAD_PAD - width)])
    return a.reshape(lead + (MLA_PAD_W,))


def _layer_weights(p, l):
    w_in = p["w_in"][l]
    c0 = FOURIER_W
    c1 = c0 + 3 * DIFF_W
    c2 = c1 + MLA_Q_LORA + MLA_KV_LORA
    c3 = c2 + MLA_ROPE
    kr_cols = jnp.pad(w_in[:, c2:c3], ((0, 0), (MLA_NOPE, LANES - MLA_QK_HD)))
    w_kvb = p["mla_w_kvb"][l].reshape(MLA_KV_LORA, MLA_HEADS, MLA_NOPE + MLA_V)
    router = jnp.pad(p["moe_w_router"][l], ((0, 0), (0, LANES - N_EXPERTS)))
    router_hi = router.astype(BF16)
    tile = lambda v, reps: jnp.tile(v, reps)[None, :].astype(F32)
    pad_gain = lambda v: jnp.pad(v, (0, MLA_HEAD_PAD - MLA_QK_HD))[None, :].astype(F32)
    w_qb = _pad_heads(p["mla_w_qb"][l], MLA_QK_HD)
    lane = np.arange(LANES)
    rot = (lane >= MLA_NOPE) & (lane < MLA_QK_HD)
    off = MLA_ROPE // 4
    partner = np.where(rot, np.where((lane & off) == 0, lane + off, lane - off), lane)
    w_qb_partner = ((w_qb.reshape(MLA_Q_LORA, MLA_HEADS, LANES) * pad_gain(p["mla_qnorm_w"][l]))[:, :, partner]
                    * jnp.asarray(rot, F32)).reshape(MLA_Q_LORA, MLA_PAD_W)
    return dict(
        fourier=w_in[:, :c0].astype(BF16),
        diff=w_in[:, c0:c1].astype(BF16),
        mla_in=jnp.concatenate([w_in[:, c1:c2], kr_cols], axis=1).astype(BF16),
        gates=w_in[:, c3:].astype(BF16),
        diff_qn=tile(p["diff_qnorm_w"][l], 2 * DIFF_HEADS),
        diff_kn=tile(p["diff_knorm_w"][l], 2 * DIFF_HEADS),
        subln=p["diff_subln_w"][l][None, :].astype(F32),
        lamv=jnp.pad(jnp.stack([p["diff_lambda_q1"][l], p["diff_lambda_k1"][l],
                                p["diff_lambda_q2"][l], p["diff_lambda_k2"][l]]).astype(F32),
                     ((0, SUBLANES - 4), (0, LANES - DIFF_HD))),
        mla=dict(
            qa_norm=p["mla_qa_norm_w"][l][None, :].astype(F32),
            w_qb=w_qb.astype(BF16),
            w_qb_rope=jnp.concatenate([w_qb, w_qb_partner], axis=1).astype(BF16),
            q_norm=pad_gain(p["mla_qnorm_w"][l]),
            kva_norm=p["mla_kva_norm_w"][l][None, :].astype(F32),
            w_k=_pad_heads(w_kvb[:, :, :MLA_NOPE].reshape(MLA_KV_LORA, -1), MLA_NOPE).astype(BF16),
            w_v=_pad_heads(w_kvb[:, :, MLA_NOPE:].reshape(MLA_KV_LORA, -1), MLA_V).astype(BF16),
            k_norm=pad_gain(p["mla_knorm_w"][l]),
        ),
        merge=dict(
            br_f=p["w_br_fourier"][l].astype(BF16),
            br_d=p["w_br_diff"][l].astype(BF16),
            br_m=p["w_br_mla"][l].astype(BF16),
            out=p["w_out"][l].astype(BF16),
            router_hi=router_hi,
            router_hi_lo=jnp.concatenate([router_hi, (router - router_hi.astype(F32)).astype(BF16)], axis=1),
        ),
        layer=l,
        norm1=p["norm1_w"][l][None, :].astype(F32),
        norm2=p["norm2_w"][l][None, :].astype(F32),
    )


def _trunk_layer(x, mods, w, lam_init, nbatch, seq, tabs, ctx):
    t, d = x.shape
    sh1, sc1, g1, sh2, sc2, g2 = mods
    h, gates = norm_gates_proj(x, w["norm1"], sc1, sh1, w["gates"], seq)
    ab = fourier_channel(h, w["fourier"], tabs["dft_ch"])
    y_f = fourier_position(ab, tabs["dft_cos"], tabs["dft_sin"], seq)
    new_ctx = None
    if ctx is None:
        q_d, k_d, v_d, k_d32, v_d32 = diff_qkv(h, w["diff"], w["diff_qn"], w["diff_kn"], seq, None)
        q_m, k_m, v_m, ckv32, krb32 = mla_proj(h, w["mla_in"], w["mla"], seq, None)
        new_ctx = (k_d32, v_d32, ckv32, krb32[:, MLA_NOPE:MLA_QK_HD])
        ctx_d = ctx_m = None
    else:
        q_d, k_d, v_d = diff_qkv(h, w["diff"], w["diff_qn"], w["diff_kn"], seq, tabs["rope_diff"])
        q_m, k_m, v_m = mla_proj(h, w["mla_in"], w["mla"], seq, tabs["rope_mla"])
        kd_c, vd_c, ckv_c, kr_c = ctx
        ctx_d = (kd_c.astype(BF16), vd_c.astype(BF16))
        krb_c = jnp.pad(kr_c, ((0, 0), (MLA_NOPE, LANES - MLA_QK_HD)))
        ctx_m = mla_ctx_keys(ckv_c, krb_c, w["mla"])
    o_d = attention(q_d, k_d, v_d, ctx_d, nbatch=nbatch, seq=seq, heads=DIFF_HEADS, nmaps=2,
                    lamv=w["lamv"], subln=w["subln"], lam_init=lam_init)
    o_m = attention(q_m, k_m, v_m, ctx_m, nbatch=nbatch, seq=seq, heads=MLA_HEADS, nmaps=1, compact=True)
    x1, h2p, aff_t = merge_out(x, y_f, o_d, o_m, gates, w["merge"], g1, w["norm2"], sc2, sh2, seq)
    idx, dst, gv, seg = route(aff_t, nbatch, seq)
    cap = idx.shape[1]
    rt = _pick_tile(nbatch * cap // 2, 512)
    by_expert = lambda a: jnp.transpose(a[:, :, :N_EXPERTS], (2, 0, 1))
    idx_e = by_expert(idx).reshape(-1, 1, rt)
    dst_e = by_expert(dst).reshape(-1, 1, rt)
    gv_e = by_expert(gv).reshape(-1, 1)
    g = moe_experts(idx_e, dst_e, gv_e, h2p, *w["moe"], w["layer"], d)
    seg0 = seg[:, 0, :].reshape(t, 1)
    seg1 = seg[:, 1, :].reshape(t, 1)
    tm = _pick_tile(seq, 256)
    npairs = nbatch * N_EXPERTS * cap
    tile_bounds = jnp.concatenate([seg0[::tm, 0], jnp.full((1,), npairs, I32)])
    x2 = combine(x1, seg0, seg1, tile_bounds, g2, g, seq)
    return x2, new_ctx


def kernel(x_prompt, x_sample, cache_diff_k, cache_diff_v, cache_mla_ckv, cache_mla_krope, c, c_ctx, w_ada, b_ada, norm1_w, norm2_w, w_in, diff_qnorm_w, diff_knorm_w, diff_lambda_q1, diff_lambda_k1, diff_lambda_q2, diff_lambda_k2, diff_subln_w, mla_qa_norm_w, mla_w_qb, mla_kva_norm_w, mla_w_kvb, mla_qnorm_w, mla_knorm_w, w_br_fourier, w_br_diff, w_br_mla, w_out, moe_w_router, moe_w_gate, moe_w_up, moe_w_down):
    params = dict(w_in=w_in, norm1_w=norm1_w, norm2_w=norm2_w,
                  diff_qnorm_w=diff_qnorm_w, diff_knorm_w=diff_knorm_w,
                  diff_lambda_q1=diff_lambda_q1, diff_lambda_k1=diff_lambda_k1,
                  diff_lambda_q2=diff_lambda_q2, diff_lambda_k2=diff_lambda_k2, diff_subln_w=diff_subln_w,
                  mla_qa_norm_w=mla_qa_norm_w, mla_w_qb=mla_w_qb, mla_kva_norm_w=mla_kva_norm_w,
                  mla_w_kvb=mla_w_kvb, mla_qnorm_w=mla_qnorm_w, mla_knorm_w=mla_knorm_w,
                  w_br_fourier=w_br_fourier, w_br_diff=w_br_diff, w_br_mla=w_br_mla, w_out=w_out,
                  moe_w_router=moe_w_router, moe_w_gate=moe_w_gate, moe_w_up=moe_w_up, moe_w_down=moe_w_down)
    bp, lp, d = x_prompt.shape
    bs, ls, _ = x_sample.shape
    depth = w_in.shape[0]
    past = cache_diff_k.shape[2]

    cond = jnp.concatenate([c, c_ctx[None, :], jnp.zeros((COND_ROWS - bs - 1, d), F32)], axis=0)
    mods = adaln(cond, w_ada, b_ada)

    dft_ch = _dft_channel_table()
    rope_diff, rope_mla = _rope_tables(ls)
    tabs_p = dict(dft_ch=dft_ch)
    tabs_p["dft_cos"], tabs_p["dft_sin"] = _dft_position_tables(lp)
    tabs_s = dict(dft_ch=dft_ch, rope_diff=rope_diff, rope_mla=rope_mla)
    tabs_s["dft_cos"], tabs_s["dft_sin"] = _dft_position_tables(ls)

    y_p = x_prompt.reshape(bp * lp, d)
    y_s = x_sample.reshape(bs * ls, d)
    new_ctx = []
    moe_weights = (moe_w_gate.astype(BF16), moe_w_up.astype(BF16), moe_w_down.astype(BF16))
    for l in range(depth):
        w = _layer_weights(params, l)
        w["moe"] = moe_weights
        lam_init = 0.8 - 0.6 * math.exp(-0.3 * l)
        m = mods[l].reshape(COND_ROWS, N_ADA, d)
        mods_s = [m[:bs, j][:, None, :] for j in range(N_ADA)]
        mods_p = [m[bs:bs + 1, j][:, None, :] for j in range(N_ADA)]
        y_p, ctx_l = _trunk_layer(y_p, mods_p, w, lam_init, bp, lp, tabs_p, None)
        new_ctx.append(ctx_l)
        ctx = (cache_diff_k[:, l].reshape(bs * past, DIFF_W), cache_diff_v[:, l].reshape(bs * past, DIFF_W),
               cache_mla_ckv[:, l].reshape(bs * past, MLA_KV_LORA), cache_mla_krope[:, l].reshape(bs * past, MLA_ROPE))
        y_s, _ = _trunk_layer(y_s, mods_s, w, lam_init, bs, ls, tabs_s, ctx)

    stack = lambda j, shape: jnp.stack([n[j].reshape((bp, lp) + shape) for n in new_ctx], axis=1)
    return (y_p.reshape(bp, lp, d), y_s.reshape(bs, ls, d),
            stack(0, (DIFF_HEADS, 2, DIFF_HD)), stack(1, (DIFF_HEADS, 2 * DIFF_HD)),
            stack(2, (MLA_KV_LORA,)), stack(3, (MLA_ROPE,)))
```

```python
import functools
import math

import jax
import jax.numpy as jnp
import numpy as np
from jax import lax
from jax.experimental import pallas as pl
from jax.experimental.pallas import tpu as pltpu

F32, BF16, I32, U32 = jnp.float32, jnp.bfloat16, jnp.int32, jnp.uint32

GRID_W = 64
ROPE_BASE = 10000.0
EPS = 1e-6
N_ADA = 6
FOURIER_GROUPS = 4
FOURIER_CH = 128
FOURIER_W = FOURIER_GROUPS * FOURIER_CH
DIFF_HEADS = 4
DIFF_HD = 64
DIFF_W = DIFF_HEADS * 2 * DIFF_HD
MLA_HEADS = 8
MLA_NOPE = 64
MLA_ROPE = 32
MLA_QK_HD = MLA_NOPE + MLA_ROPE
MLA_V = 64
MLA_Q_LORA = 384
MLA_KV_LORA = 256
N_EXPERTS = 16
EC_FACTOR = 2

LANES = 128
SUBLANES = 8
VMEM_LIMIT_BYTES = 56 * 1024 * 1024
LOG2E = math.log2(math.e)
MLA_HEAD_PAD = LANES
MLA_PAD_W = MLA_HEADS * MLA_HEAD_PAD
COND_ROWS = 16


def _cparams(n_axes, **kw):
    return pltpu.CompilerParams(dimension_semantics=("arbitrary",) * n_axes,
                                vmem_limit_bytes=VMEM_LIMIT_BYTES, **kw)


def _dot(a, b):
    return jnp.dot(a, b, preferred_element_type=F32)


def _dot_nt(a, b):
    return lax.dot_general(a, b, (((1,), (1,)), ((), ())), preferred_element_type=F32)


def _pick_tile(n, target):
    t = min(n, target)
    while n % t:
        t //= 2
    return t


def _pack_bf16_pairs(x):
    half = x.shape[1] // 2
    bits = pltpu.bitcast(x.astype(jnp.bfloat16).astype(F32), U32)
    return (bits[:, :half] >> 16) | bits[:, half:]


def _store_token_tiles(ref, packed):
    rows, w = packed.shape
    n = w // LANES
    for c in range(n):
        ref[pl.ds(c, rows, stride=n), :] = packed[:, c * LANES:(c + 1) * LANES]


def _unpack_rows(load_chunk, nchunk):
    lo, hi = [], []
    for c in range(nchunk):
        wds = load_chunk(c)
        lo.append(pltpu.bitcast(wds << 16, F32).astype(BF16))
        hi.append(pltpu.bitcast(wds & jnp.uint32(0xFFFF0000), F32).astype(BF16))
    return jnp.concatenate(lo + hi, axis=1)


def _adaln_body(c_ref, w_ref, b_ref, o_ref):
    c = c_ref[...]
    a = (c * jax.nn.sigmoid(c)).astype(BF16)
    o_ref[...] = _dot(a, w_ref[...].astype(BF16)) + b_ref[...]


def adaln(cond, w_ada, b_ada):
    depth, d, n = w_ada.shape
    tn = _pick_tile(n, 2048)
    return pl.pallas_call(
        _adaln_body,
        grid=(depth, n // tn),
        in_specs=[pl.BlockSpec((COND_ROWS, d), lambda l, j: (0, 0)),
                  pl.BlockSpec((None, d, tn), lambda l, j: (l, 0, j)),
                  pl.BlockSpec((None, 1, tn), lambda l, j: (l, 0, j))],
        out_specs=pl.BlockSpec((None, COND_ROWS, tn), lambda l, j: (l, 0, j)),
        out_shape=jax.ShapeDtypeStruct((depth, COND_ROWS, n), F32),
        compiler_params=_cparams(2),
    )(cond, w_ada, b_ada.reshape(depth, 1, n))


def _norm_mod(x, nw, sc, sh):
    r = lax.rsqrt(jnp.mean(x * x, axis=-1, keepdims=True) + EPS)
    return (x * r) * nw * (1.0 + sc) + sh


def _mod_index(nb, tiles_per_batch):
    if nb == 1:
        return lambda i, *_: (0, 0, 0)
    return lambda i, *_: (i // tiles_per_batch, 0, 0)


def _gates_body(x_ref, nw_ref, sc_ref, sh_ref, w_ref, h_ref, o_ref):
    @pl.when(pl.program_id(1) == 0)
    def _():
        h_ref[...] = _norm_mod(x_ref[...], nw_ref[...], sc_ref[...], sh_ref[...]).astype(BF16)

    o_ref[...] = jax.nn.sigmoid(_dot(h_ref[...], w_ref[...])).astype(BF16)


def norm_gates_proj(x, nw, sc, sh, w, seq):
    t, d = x.shape
    n = w.shape[1]
    nb = sc.shape[0]
    tm, tn = _pick_tile(seq if nb > 1 else t, 1024), _pick_tile(n, 1024)
    mod_spec = pl.BlockSpec((None, 1, d), _mod_index(nb, seq // tm))
    return pl.pallas_call(
        _gates_body,
        grid=(t // tm, n // tn),
        in_specs=[pl.BlockSpec((tm, d), lambda i, j: (i, 0)),
                  pl.BlockSpec((1, d), lambda i, j: (0, 0)),
                  mod_spec, mod_spec,
                  pl.BlockSpec((d, tn), lambda i, j: (0, j))],
        out_specs=[pl.BlockSpec((tm, d), lambda i, j: (i, 0)),
                   pl.BlockSpec((tm, tn), lambda i, j: (i, j))],
        out_shape=[jax.ShapeDtypeStruct((t, d), BF16), jax.ShapeDtypeStruct((t, n), BF16)],
        compiler_params=_cparams(2),
    )(x, nw, sc, sh, w)


def _fourier_ch_body(h_ref, w_ref, cs_ref, o_ref):
    u = _dot(h_ref[...], w_ref[...]).astype(BF16)
    for g in range(FOURIER_GROUPS):
        ab = _dot(u[:, g * FOURIER_CH:(g + 1) * FOURIER_CH], cs_ref[...])
        o_ref[:, g * FOURIER_CH:(g + 1) * FOURIER_CH] = ab[:, :FOURIER_CH].astype(BF16)
        o_ref[:, FOURIER_W + g * FOURIER_CH:FOURIER_W + (g + 1) * FOURIER_CH] = ab[:, FOURIER_CH:].astype(BF16)


def fourier_channel(h, w, cs):
    t, d = h.shape
    tm = _pick_tile(t, 1024)
    return pl.pallas_call(
        _fourier_ch_body,
        grid=(t // tm,),
        in_specs=[pl.BlockSpec((tm, d), lambda i: (i, 0)),
                  pl.BlockSpec((d, FOURIER_W), lambda i: (0, 0)),
                  pl.BlockSpec((FOURIER_CH, 2 * FOURIER_CH), lambda i: (0, 0))],
        out_specs=pl.BlockSpec((tm, 2 * FOURIER_W), lambda i: (i, 0)),
        out_shape=jax.ShapeDtypeStruct((t, 2 * FOURIER_W), BF16),
        compiler_params=_cparams(1),
    )(h, w, cs)


def _fourier_pos_body(c_ref, s_ref, ab_ref, o_ref):
    y = _dot(c_ref[...], ab_ref[:, :FOURIER_W]) - _dot(s_ref[...], ab_ref[:, FOURIER_W:])
    o_ref[...] = y.astype(BF16)


def fourier_position(ab, cpos, spos, seq):
    t = ab.shape[0]
    tr = _pick_tile(seq, 512)
    nr = seq // tr
    return pl.pallas_call(
        _fourier_pos_body,
        grid=(t // seq, nr),
        in_specs=[pl.BlockSpec((tr, seq), lambda b, r: (r, 0)),
                  pl.BlockSpec((tr, seq), lambda b, r: (r, 0)),
                  pl.BlockSpec((seq, 2 * FOURIER_W), lambda b, r: (b, 0))],
        out_specs=pl.BlockSpec((tr, FOURIER_W), lambda b, r: (b * nr + r, 0)),
        out_shape=jax.ShapeDtypeStruct((t, FOURIER_W), BF16),
        compiler_params=_cparams(2),
    )(cpos, spos, ab)


def _rope(x, cos, sin, off):
    w = x.shape[1]
    lane = lax.broadcasted_iota(I32, (1, w), 1)
    first = (lane & off) == 0
    partner = jnp.where(first, pltpu.roll(x, w - off, 1), pltpu.roll(x, off, 1))
    return x * cos + partner * sin


def _half_tile_norm(x):
    outs = []
    lane = lax.broadcasted_iota(I32, (1, LANES), 1)
    low = lane < DIFF_HD
    for j in range(x.shape[1] // LANES):
        seg = x[:, j * LANES:(j + 1) * LANES]
        sq = seg * seg
        s_lo = jnp.sum(jnp.where(low, sq, 0.0), axis=-1, keepdims=True)
        s_hi = jnp.sum(jnp.where(low, 0.0, sq), axis=-1, keepdims=True)
        ms = jnp.where(low, s_lo, s_hi) * (1.0 / DIFF_HD)
        outs.append(seg * lax.rsqrt(ms + EPS))
    return jnp.concatenate(outs, axis=1)


def _diff_qkv_body(*refs, rope):
    if rope:
        h_ref, w_ref, qn_ref, kn_ref, cos_ref, sin_ref, q_out, k_out, v_out = refs
    else:
        h_ref, w_ref, qn_ref, kn_ref, q_out, k_out, v_out, k32_out, v32_out = refs
    z = _dot(h_ref[...], w_ref[...])
    q = _half_tile_norm(z[:, :DIFF_W]) * qn_ref[...]
    k = _half_tile_norm(z[:, DIFF_W:2 * DIFF_W]) * kn_ref[...]
    v = z[:, 2 * DIFF_W:]
    if rope:
        q = _rope(q, cos_ref[...], sin_ref[...], DIFF_HD // 4)
        k = _rope(k, cos_ref[...], sin_ref[...], DIFF_HD // 4)
    else:
        k32_out[...] = k
        v32_out[...] = v
    q_out[...] = (q * (DIFF_HD ** -0.5 * LOG2E)).astype(BF16)
    k_out[...] = k.astype(BF16)
    v_out[...] = v.astype(BF16)


def diff_qkv(h, w, qn, kn, seq, rope_tabs):
    t, d = h.shape
    rope = rope_tabs is not None
    tm = _pick_tile(seq, 512)
    row = lambda i: (i, 0)
    const = lambda i: (0, 0)
    in_specs = [pl.BlockSpec((tm, d), row), pl.BlockSpec((d, 3 * DIFF_W), const),
                pl.BlockSpec((1, DIFF_W), const), pl.BlockSpec((1, DIFF_W), const)]
    args = [h, w, qn, kn]
    out_shape = [jax.ShapeDtypeStruct((t, DIFF_W), BF16)] * 3
    if rope:
        nt = seq // tm
        tab = pl.BlockSpec((tm, DIFF_W), lambda i: (i % nt, 0))
        in_specs += [tab, tab]
        args += list(rope_tabs)
    else:
        out_shape = out_shape + [jax.ShapeDtypeStruct((t, DIFF_W), F32)] * 2
    return pl.pallas_call(
        functools.partial(_diff_qkv_body, rope=rope),
        grid=(t // tm,),
        in_specs=in_specs,
        out_specs=[pl.BlockSpec((tm, DIFF_W), row)] * len(out_shape),
        out_shape=out_shape,
        compiler_params=_cparams(1),
    )(*args)


def _mla_keys(ckv_bf, krb, wk_ref, wv_ref, kn_ref, rope_tab):
    kn = _dot(ckv_bf, wk_ref[...])
    gain = kn_ref[...]
    kr = krb * gain
    if rope_tab is not None:
        kr = _rope(kr, rope_tab[0], rope_tab[1], MLA_ROPE // 4)
    kr_sq = jnp.sum(krb * krb, axis=-1, keepdims=True)
    outs = []
    for j in range(MLA_HEADS):
        seg = kn[:, j * LANES:(j + 1) * LANES]
        ms = (jnp.sum(seg * seg, axis=-1, keepdims=True) + kr_sq) * (1.0 / MLA_QK_HD)
        outs.append((seg * gain + kr) * lax.rsqrt(ms + EPS))
    return jnp.concatenate(outs, axis=1), _dot(ckv_bf, wv_ref[...])


def _mla_proj_body(*refs, rope):
    (h_ref, w_ref, qan_ref, wqb_ref, qn_ref, kvan_ref, wk_ref, wv_ref, kn_ref), refs = refs[:9], refs[9:]
    if rope:
        cos_ref, sin_ref, q_out, k_out, v_out = refs
        rope_tab = (cos_ref[...], sin_ref[...])
    else:
        q_out, k_out, v_out, ckv32_out, kr32_out = refs
        rope_tab = None
    z = _dot(h_ref[...], w_ref[...])
    q_a = z[:, :MLA_Q_LORA]
    kv_a = z[:, MLA_Q_LORA:MLA_Q_LORA + MLA_KV_LORA]
    krb = z[:, MLA_Q_LORA + MLA_KV_LORA:]
    c_q = q_a * lax.rsqrt(jnp.mean(q_a * q_a, axis=-1, keepdims=True) + EPS) * qan_ref[...]
    zq = _dot(c_q.astype(BF16), wqb_ref[...])
    gain = qn_ref[...]
    outs = []
    for j in range(MLA_HEADS):
        seg = zq[:, j * LANES:(j + 1) * LANES]
        ms = jnp.sum(seg * seg, axis=-1, keepdims=True) * (1.0 / MLA_QK_HD)
        val = seg * gain
        if rope:
            val = val * rope_tab[0] + zq[:, MLA_PAD_W + j * LANES:MLA_PAD_W + (j + 1) * LANES] * rope_tab[1]
        outs.append(val * (lax.rsqrt(ms + EPS) * (MLA_QK_HD ** -0.5 * LOG2E)))
    q_out[...] = jnp.concatenate(outs, axis=1).astype(BF16)
    c_kv = kv_a * lax.rsqrt(jnp.mean(kv_a * kv_a, axis=-1, keepdims=True) + EPS) * kvan_ref[...]
    k, v = _mla_keys(c_kv.astype(BF16), krb, wk_ref, wv_ref, kn_ref, rope_tab)
    if not rope:
        ckv32_out[...] = c_kv
        kr32_out[...] = krb
    k_out[...] = k.astype(BF16)
    v_out[...] = v.astype(BF16)


def mla_proj(h, w, p, seq, rope_tabs):
    t, d = h.shape
    rope = rope_tabs is not None
    tm = _pick_tile(seq, 512)
    row = lambda i: (i, 0)
    const = lambda i: (0, 0)
    wcols = MLA_Q_LORA + MLA_KV_LORA + LANES
    w_qb = p["w_qb_rope"] if rope else p["w_qb"]
    in_specs = [pl.BlockSpec((tm, d), row), pl.BlockSpec((d, wcols), const),
                pl.BlockSpec((1, MLA_Q_LORA), const), pl.BlockSpec(w_qb.shape, const),
                pl.BlockSpec((1, LANES), const), pl.BlockSpec((1, MLA_KV_LORA), const),
                pl.BlockSpec((MLA_KV_LORA, MLA_PAD_W), const), pl.BlockSpec((MLA_KV_LORA, MLA_PAD_W), const),
                pl.BlockSpec((1, LANES), const)]
    args = [h, w, p["qa_norm"], w_qb, p["q_norm"], p["kva_norm"], p["w_k"], p["w_v"], p["k_norm"]]
    out_shape = [jax.ShapeDtypeStruct((t, MLA_PAD_W), BF16)] * 3
    out_specs = [pl.BlockSpec((tm, MLA_PAD_W), row)] * 3
    if rope:
        nt = seq // tm
        tab = pl.BlockSpec((tm, LANES), lambda i: (i % nt, 0))
        in_specs += [tab, tab]
        args += list(rope_tabs)
    else:
        out_shape += [jax.ShapeDtypeStruct((t, MLA_KV_LORA), F32), jax.ShapeDtypeStruct((t, LANES), F32)]
        out_specs += [pl.BlockSpec((tm, MLA_KV_LORA), row), pl.BlockSpec((tm, LANES), row)]
    return pl.pallas_call(
        functools.partial(_mla_proj_body, rope=rope),
        grid=(t // tm,),
        in_specs=in_specs,
        out_specs=out_specs,
        out_shape=out_shape,
        compiler_params=_cparams(1),
    )(*args)


def _mla_ctx_body(ckv_ref, krb_ref, wk_ref, wv_ref, kn_ref, k_out, v_out):
    k, v = _mla_keys(ckv_ref[...].astype(BF16), krb_ref[...], wk_ref, wv_ref, kn_ref, None)
    k_out[...] = k.astype(BF16)
    v_out[...] = v.astype(BF16)


def mla_ctx_keys(ckv, krb, p):
    t = ckv.shape[0]
    tm = _pick_tile(t, 512)
    row = lambda i: (i, 0)
    const = lambda i: (0, 0)
    return pl.pallas_call(
        _mla_ctx_body,
        grid=(t // tm,),
        in_specs=[pl.BlockSpec((tm, MLA_KV_LORA), row), pl.BlockSpec((tm, LANES), row),
                  pl.BlockSpec((MLA_KV_LORA, MLA_PAD_W), const), pl.BlockSpec((MLA_KV_LORA, MLA_PAD_W), const),
                  pl.BlockSpec((1, LANES), const)],
        out_specs=[pl.BlockSpec((tm, MLA_PAD_W), row)] * 2,
        out_shape=[jax.ShapeDtypeStruct((t, MLA_PAD_W), BF16)] * 2,
        compiler_params=_cparams(1),
    )(ckv, krb, p["w_k"], p["w_v"], p["k_norm"])


SOFTMAX_SUM_FLOOR = 2.0 ** -100


def _attn_body(*refs, nmaps, hp, has_ctx, subln, compact, tk, lam_init):
    refs = list(refs)
    kmax_scr = refs.pop()
    lamv_ref = refs.pop(0) if nmaps == 2 else None
    q_ref, k_ref, v_ref = refs[:3]
    refs = refs[3:]
    segments = []
    if has_ctx:
        segments.append((refs[0], refs[1]))
        refs = refs[2:]
    segments.append((k_ref, v_ref))
    subln_ref = refs.pop(0) if subln else None
    o_ref = refs[0]
    tq = q_ref.shape[0]
    lane = lax.broadcasted_iota(I32, (1, LANES), 1)
    masks = [lane < DIFF_HD, lane >= DIFF_HD] if nmaps == 2 else [None]

    def cols(hh):
        return slice(hh * LANES, (hh + 1) * LANES)

    @pl.when(pl.program_id(2) == 0)
    def _():
        for hh in range(hp):
            for i, mk in enumerate(masks):
                best = jnp.zeros((1, 1), F32)
                for kr, _ in segments:
                    kf = kr[:, cols(hh)].astype(F32)
                    sq = kf * kf if mk is None else jnp.where(mk, kf * kf, 0.0)
                    best = jnp.maximum(best, jnp.max(jnp.sum(sq, axis=-1, keepdims=True), axis=0, keepdims=True))
                row = hh * nmaps + i
                kmax_scr[row:row + 1, :] = jnp.broadcast_to(best, (1, LANES))

    def queries(hh):
        q = q_ref[:, cols(hh)]
        return [q if mk is None else jnp.where(mk, q, jnp.zeros_like(q)) for mk in masks]

    def key_chunks(hh):
        for kr, vr in segments:
            n = kr.shape[0]
            for c0 in range(0, n, tk):
                c1 = min(n, c0 + tk)
                yield kr[c0:c1, cols(hh)], vr[c0:c1, cols(hh)]

    def shifted(hh):
        acc, l = [], []
        for i, qi in enumerate(queries(hh)):
            qf = qi.astype(F32)
            row = hh * nmaps + i
            shift = jnp.sqrt(jnp.sum(qf * qf, axis=-1, keepdims=True) * kmax_scr[row:row + 1, 0:1])
            a = jnp.zeros((tq, LANES), F32)
            li = jnp.zeros((tq, 1), F32)
            for kc, vc in key_chunks(hh):
                p = jnp.exp2(_dot_nt(qi, kc) - shift)
                li = li + jnp.sum(p, axis=-1, keepdims=True)
                a = a + _dot(p.astype(BF16), vc)
            acc.append(a)
            l.append(li)
        return acc, l

    def running_max(hh):
        qs = queries(hh)
        m = [jnp.full((tq, 1), -jnp.inf, F32) for _ in qs]
        l = [jnp.zeros((tq, 1), F32) for _ in qs]
        acc = [jnp.zeros((tq, LANES), F32) for _ in qs]
        for kc, vc in key_chunks(hh):
            for i, qi in enumerate(qs):
                s = _dot_nt(qi, kc)
                mn = jnp.maximum(m[i], jnp.max(s, axis=-1, keepdims=True))
                alpha = jnp.exp2(m[i] - mn)
                p = jnp.exp2(s - mn)
                l[i] = alpha * l[i] + jnp.sum(p, axis=-1, keepdims=True)
                acc[i] = alpha * acc[i] + _dot(p.astype(BF16), vc)
                m[i] = mn
        return acc, l

    def head_out(acc, l):
        o = acc[0] / l[0]
        if nmaps == 2:
            lv = lamv_ref[...]
            lam = (jnp.exp(jnp.sum(lv[0:1] * lv[1:2], axis=-1, keepdims=True))
                   - jnp.exp(jnp.sum(lv[2:3] * lv[3:4], axis=-1, keepdims=True)) + lam_init)
            o = o - lam * (acc[1] / l[1])
        if subln:
            o = o * lax.rsqrt(jnp.mean(o * o, axis=-1, keepdims=True) + EPS) * subln_ref[...] * (1.0 - lam_init)
        return o

    def write(outs):
        if compact:
            for j in range(hp // 2):
                pair = jnp.where(lane < LANES // 2, outs[2 * j], pltpu.roll(outs[2 * j + 1], LANES // 2, 1))
                o_ref[:, cols(j)] = pair.astype(BF16)
        else:
            for hh in range(hp):
                o_ref[:, cols(hh)] = outs[hh].astype(BF16)

    results = [shifted(hh) for hh in range(hp)]
    write([head_out(acc, l) for acc, l in results])

    lmin = functools.reduce(jnp.minimum, [jnp.min(li) for _, l in results for li in l])

    @pl.when(lmin < SOFTMAX_SUM_FLOOR)
    def _():
        write([head_out(*running_max(hh)) for hh in range(hp)])


def attention(q, k, v, ctx, *, nbatch, seq, heads, nmaps, compact=False, lamv=None, subln=None, lam_init=0.0):
    t, w = q.shape
    tq = _pick_tile(seq, 1024)
    nq = seq // tq
    hp = heads if seq <= 512 else (2 if compact else 1)
    assert hp * nmaps <= SUBLANES and heads % hp == 0
    wo = hp * LANES // 2 if compact else hp * LANES
    qspec = pl.BlockSpec((tq, hp * LANES), lambda b, h, i: (b * nq + i, h))
    kvspec = pl.BlockSpec((seq, hp * LANES), lambda b, h, i: (b, h))
    in_specs, args = [], []
    if nmaps == 2:
        in_specs.append(pl.BlockSpec((SUBLANES, LANES), lambda b, h, i: (0, 0)))
        args.append(lamv)
    in_specs += [qspec, kvspec, kvspec]
    args += [q, k, v]
    if ctx is not None:
        past = ctx[0].shape[0] // nbatch
        cspec = pl.BlockSpec((past, hp * LANES), lambda b, h, i: (b, h))
        in_specs += [cspec, cspec]
        args += list(ctx)
    if subln is not None:
        in_specs.append(pl.BlockSpec((1, LANES), lambda b, h, i: (0, 0)))
        args.append(subln)
    return pl.pallas_call(
        functools.partial(_attn_body, nmaps=nmaps, hp=hp, has_ctx=ctx is not None, subln=subln is not None,
                          compact=compact, tk=512, lam_init=lam_init),
        grid=(nbatch, heads // hp, nq),
        in_specs=in_specs,
        out_specs=pl.BlockSpec((tq, wo), lambda b, h, i: (b * nq + i, h)),
        out_shape=jax.ShapeDtypeStruct((t, w // 2 if compact else w), BF16),
        scratch_shapes=[pltpu.VMEM((SUBLANES, LANES), F32)],
        compiler_params=_cparams(3),
    )(*args)


def _merge_body(x_ref, f_ref, od_ref, om_ref, g0_ref, g1_ref, g2_ref, wf_ref, wd_ref, wm_ref, wo_ref,
                gate1_ref, nw_ref, sc_ref, sh_ref, wrh_ref, wrl_ref,
                x1_ref, h2p_ref, aff_ref):
    merged = g0_ref[...].astype(F32) * _dot(f_ref[...], wf_ref[...])
    merged = merged + g1_ref[...].astype(F32) * _dot(od_ref[...], wd_ref[...])
    merged = merged + g2_ref[...].astype(F32) * _dot(om_ref[...], wm_ref[...])
    x1 = x_ref[...] + gate1_ref[...] * _dot(merged.astype(BF16), wo_ref[...])
    x1_ref[...] = x1
    h2 = _norm_mod(x1, nw_ref[...], sc_ref[...], sh_ref[...])
    _store_token_tiles(h2p_ref, _pack_bf16_pairs(h2))
    h_hi = h2.astype(jnp.bfloat16)
    h_lo = (h2 - h_hi.astype(F32)).astype(BF16)
    hi_terms = _dot(h_hi, wrl_ref[...])
    logits = hi_terms[:, :LANES] + hi_terms[:, LANES:] + _dot(h_lo, wrh_ref[...])
    lane = lax.broadcasted_iota(I32, (1, LANES), 1)
    logits = jnp.where(lane < N_EXPERTS, logits, -1e30)
    e = jnp.exp(logits - jnp.max(logits, axis=-1, keepdims=True))
    aff = e / jnp.sum(e, axis=-1, keepdims=True)
    aff_ref[...] = aff.T[:N_EXPERTS, :]


def merge_out(x, f, od, om, gates, w, gate1, nw, sc, sh, seq):
    t, d = x.shape
    tm = _pick_tile(seq, 512)
    nchunk = d // 2 // LANES
    row = lambda i: (i, 0)
    const = lambda i: (0, 0)
    once = pl.Buffered(1)
    mod_spec = pl.BlockSpec((None, 1, d), _mod_index(sc.shape[0], seq // tm))
    wspec = lambda a: pl.BlockSpec(a.shape, const, pipeline_mode=once)
    in_specs = [pl.BlockSpec((tm, d), row),
                pl.BlockSpec((tm, f.shape[1]), row), pl.BlockSpec((tm, od.shape[1]), row),
                pl.BlockSpec((tm, om.shape[1]), row),
                pl.BlockSpec((tm, d), lambda i: (i, 0)), pl.BlockSpec((tm, d), lambda i: (i, 1)),
                pl.BlockSpec((tm, d), lambda i: (i, 2)),
                wspec(w["br_f"]), wspec(w["br_d"]), wspec(w["br_m"]), wspec(w["out"]),
                mod_spec, pl.BlockSpec((1, d), const), mod_spec, mod_spec,
                wspec(w["router_hi"]), wspec(w["router_hi_lo"])]
    return pl.pallas_call(
        _merge_body,
        grid=(t // tm,),
        in_specs=in_specs,
        out_specs=[pl.BlockSpec((tm, d), row),
                   pl.BlockSpec((tm * nchunk, LANES), row),
                   pl.BlockSpec((N_EXPERTS, tm), lambda i: (0, i))],
        out_shape=[jax.ShapeDtypeStruct((t, d), F32),
                   jax.ShapeDtypeStruct((t * nchunk, LANES), U32),
                   jax.ShapeDtypeStruct((N_EXPERTS, t), F32)],
        compiler_params=_cparams(1),
    )(x, f, od, om, gates, gates, gates, w["br_f"], w["br_d"], w["br_m"], w["out"],
      gate1, nw, sc, sh, w["router_hi"], w["router_hi_lo"])


def _cumsum_lanes(x):
    n = x.shape[1]
    lane = lax.broadcasted_iota(I32, (1, n), 1)
    s = 1
    while s < n:
        x = x + jnp.where(lane >= s, pltpu.roll(x, s, 1), 0)
        s *= 2
    return x


ROUTE_MATMUL_MIN_TOKENS = 1024


def _route_body(aff_ref, idx_ref, dst_ref, gv_ref, seg_ref, key_scr, dst_scr, *, cap, row_chunk):
    b = pl.program_id(0)
    aff = aff_ref[...]
    ne, n = aff.shape
    bits = pltpu.bitcast(aff, I32)

    def search(i, thr):
        cand = thr | (1 << (30 - i))
        cnt = jnp.sum(jnp.where(bits >= cand, 1.0, 0.0), axis=-1, keepdims=True)
        return jnp.where(cnt >= cap, cand, thr)

    thr = lax.fori_loop(0, 31, search, jnp.zeros((ne, 1), I32))
    gt = bits > thr
    eq = (bits == thr).astype(I32)
    need = cap - jnp.sum(jnp.where(gt, 1.0, 0.0), axis=-1, keepdims=True).astype(I32)
    eq_before = _cumsum_lanes(eq) - eq
    sel = jnp.where(gt | ((eq > 0) & (eq_before < need)), 1, 0)
    cum = _cumsum_lanes(sel)
    key_scr[...] = sel * cum

    before = jnp.zeros((1, n), I32)
    for e in range(ne):
        dst_scr[e:e + 1, :] = before
        before = before + sel[e:e + 1, :]
    k_tok = before
    start = _cumsum_lanes(k_tok) - k_tok + b * (ne * cap)
    dst_scr[...] = dst_scr[...] + start
    seg_ref[0:1, :] = start
    seg_ref[1:2, :] = start + k_tok
    seg_ref[2:SUBLANES, :] = jnp.zeros((SUBLANES - 2, n), I32)

    tok = lax.broadcasted_iota(I32, (1, n), 1) + b * n
    lane = lax.broadcasted_iota(I32, (1, LANES), 1)
    idx_ref[...] = jnp.zeros(idx_ref.shape, I32)
    dst_ref[...] = jnp.zeros(dst_ref.shape, I32)
    gv_ref[...] = jnp.zeros(gv_ref.shape, F32)
    byte = lambda v, shift: ((v >> shift) & 255).astype(F32)

    def per_expert(e, carry):
        key = key_scr[pl.ds(e, 1), :]
        dst = dst_scr[pl.ds(e, 1), :]
        af = aff_ref[pl.ds(e, 1), :]
        if n >= ROUTE_MATMUL_MIN_TOKENS:
            a_hi = af.astype(jnp.bfloat16).astype(F32)
            a_mid = (af - a_hi).astype(jnp.bfloat16).astype(F32)
            a_lo = af - a_hi - a_mid
            vals = jnp.concatenate([byte(tok, 16), byte(tok, 8), byte(tok, 0),
                                    byte(dst, 16), byte(dst, 8), byte(dst, 0),
                                    a_hi, a_mid, a_lo] + [jnp.zeros((1, n), F32)] * 7, axis=0).astype(BF16)
        for c0 in range(0, cap, row_chunk):
            slot = lax.broadcasted_iota(I32, (row_chunk, 1), 0) + (c0 + 1)
            hit = key == slot
            if n >= ROUTE_MATMUL_MIN_TOKENS:
                got = _dot_nt(jnp.where(hit, 1.0, 0.0).astype(BF16), vals)
                word = lambda j: (got[:, j:j + 1].astype(I32) * 65536 + got[:, j + 1:j + 2].astype(I32) * 256
                                  + got[:, j + 2:j + 3].astype(I32))
                i_col, d_col = word(0), word(3)
                g_col = got[:, 6:7] + got[:, 7:8] + got[:, 8:9]
            else:
                pick = lambda v: jnp.sum(jnp.where(hit, v, 0.0), axis=-1, keepdims=True)
                i_col, d_col = pick(tok.astype(F32)).astype(I32), pick(dst.astype(F32)).astype(I32)
                g_col = pick(af)
            rows = pl.ds(c0, row_chunk)
            idx_ref[rows, :] = jnp.where(lane == e, i_col, idx_ref[rows, :])
            dst_ref[rows, :] = jnp.where(lane == e, d_col, dst_ref[rows, :])
            gv_ref[rows, :] = jnp.where(lane == e, g_col, gv_ref[rows, :])
        return carry

    lax.fori_loop(0, ne, per_expert, 0)


def route(aff_t, nbatch, seq):
    ne = aff_t.shape[0]
    cap = EC_FACTOR * seq // ne
    row_chunk = min(cap, 256)
    tab = lambda dt: jax.ShapeDtypeStruct((nbatch, cap, LANES), dt)
    tspec = pl.BlockSpec((None, cap, LANES), lambda b: (b, 0, 0))
    return pl.pallas_call(
        functools.partial(_route_body, cap=cap, row_chunk=row_chunk),
        grid=(nbatch,),
        in_specs=[pl.BlockSpec((ne, seq), lambda b: (0, b))],
        out_specs=[tspec, tspec, tspec, pl.BlockSpec((None, SUBLANES, seq), lambda b: (b, 0, 0))],
        out_shape=[tab(I32), tab(I32), tab(F32), jax.ShapeDtypeStruct((nbatch, SUBLANES, seq), I32)],
        scratch_shapes=[pltpu.VMEM((ne, seq), I32), pltpu.VMEM((ne, seq), I32)],
        compiler_params=_cparams(1),
    )(aff_t)


DMA_UNROLL = 8


def _moe_body(idx_a, idx_b, idx_a_next, dst_a, dst_b, dst_b_prev, gv_a, gv_b, h2p_ref, wg_ref, wu_ref, wd_ref, g_ref,
              xa, xb, ya, yb, sem, *, rt, d):
    nw = d // 2 // LANES
    step = pl.program_id(0) * pl.num_programs(1) + pl.program_id(1)
    last = pl.num_programs(0) * pl.num_programs(1) - 1
    in_a, in_b, out_a, out_b = (sem.at[i] for i in range(4))

    def row_tile(r):
        return pl.ds(r * nw if isinstance(r, int) else pl.multiple_of(r * nw, nw), nw)

    def looped(fn):
        def body(i, carry):
            for u in range(DMA_UNROLL):
                fn(i * DMA_UNROLL + u)
            return carry
        lax.fori_loop(0, rt // DMA_UNROLL, body, 0)

    def inline(fn):
        for r in range(rt):
            fn(r)

    def gather(idx_ref, r, xbuf, s):
        src = 0 if idx_ref is None else idx_ref[0, r]
        return pltpu.make_async_copy(h2p_ref.at[row_tile(src), :], xbuf.at[row_tile(r), :], s)

    def scatter(ybuf, r, dst_ref, s):
        dst = 0 if dst_ref is None else dst_ref[0, r]
        return pltpu.make_async_copy(ybuf.at[row_tile(r), :], g_ref.at[row_tile(dst), :], s)

    def drain_gather(xbuf, s):
        looped(lambda r: gather(None, r, xbuf, s).wait())

    def drain_scatter(ybuf, s):
        looped(lambda r: scatter(ybuf, r, None, s).wait())

    def ffn(xbuf, gv_ref):
        x = _unpack_rows(lambda c: xbuf[pl.ds(c, rt, stride=nw), :], nw)
        a = _dot(x, wg_ref[...])
        u = _dot(x, wu_ref[...])
        mid = (a * jax.nn.sigmoid(a) * u).astype(BF16)
        return _pack_bf16_pairs(_dot(mid, wd_ref[...]) * gv_ref[...])

    @pl.when(step == 0)
    def _():
        looped(lambda r: gather(idx_a, r, xa, in_a).start())
        yb[...] = jnp.zeros(yb.shape, U32)

    drain_gather(xa, in_a)
    inline(lambda r: gather(idx_b, r, xb, in_b).start())
    inline(lambda r: scatter(yb, r, dst_b_prev, out_b).start())
    y = ffn(xa, gv_a)

    @pl.when(step > 0)
    def _():
        drain_scatter(ya, out_a)

    _store_token_tiles(ya, y)

    drain_gather(xb, in_b)
    inline(lambda r: gather(idx_a_next, r, xa, in_a).start())
    inline(lambda r: scatter(ya, r, dst_a, out_a).start())
    y = ffn(xb, gv_b)
    drain_scatter(yb, out_b)
    _store_token_tiles(yb, y)

    @pl.when(step == last)
    def _():
        looped(lambda r: scatter(yb, r, dst_b, out_b).start())
        drain_gather(xa, in_a)
        drain_scatter(ya, out_a)
        drain_scatter(yb, out_b)


def moe_experts(idx, dst, gv, h2p, wg, wu, wd, layer, d):
    _, ne, _, ff = wg.shape
    rt = idx.shape[2]
    ntiles = idx.shape[0]
    pairs = ntiles // ne // 2
    nw = d // 2 // LANES
    tile_a = lambda e, j: 2 * (e * pairs + j)
    smem = lambda tile: pl.BlockSpec((None, 1, rt), lambda e, j: (tile(e, j), 0, 0), memory_space=pltpu.SMEM)
    vmem_col = lambda tile: pl.BlockSpec((rt, 1), lambda e, j: (tile(e, j), 0))
    tile_b = lambda e, j: tile_a(e, j) + 1
    tile_a_next = lambda e, j: jnp.minimum(tile_a(e, j) + 2, ntiles - 2)
    tile_b_prev = lambda e, j: jnp.maximum(tile_a(e, j) - 1, 1)
    wspec = lambda a: pl.BlockSpec((None, None) + a.shape[2:], lambda e, j: (layer, e, 0, 0))
    row_buf = pltpu.VMEM((rt * nw, LANES), U32)
    return pl.pallas_call(
        functools.partial(_moe_body, rt=rt, d=d),
        grid=(ne, pairs),
        in_specs=[smem(tile_a), smem(tile_b), smem(tile_a_next), smem(tile_a), smem(tile_b), smem(tile_b_prev),
                  vmem_col(tile_a), vmem_col(tile_b),
                  pl.BlockSpec(memory_space=pl.ANY), wspec(wg), wspec(wu), wspec(wd)],
        out_specs=pl.BlockSpec(memory_space=pl.ANY),
        out_shape=jax.ShapeDtypeStruct((ntiles * rt * nw, LANES), U32),
        scratch_shapes=[row_buf, row_buf, row_buf, row_buf, pltpu.SemaphoreType.DMA((4,))],
        compiler_params=_cparams(2, has_side_effects=True),
    )(idx, idx, idx, dst, dst, dst, gv, gv, h2p, wg, wu, wd)


COMBINE_WINDOW = 640


def _combine_body(tb_ref, x_ref, s0_ref, s1_ref, gate_ref, g_ref, o_ref, gbuf, sem, *, win, d, npairs):
    t = pl.program_id(0)
    nw = d // 2 // LANES
    first = tb_ref[t]
    last = tb_ref[t + 1]
    s0 = s0_ref[...]
    s1 = s1_ref[...]

    def window_start(lo):
        return jnp.minimum(lo, npairs - win)

    def window_copy(lo, slot):
        rows = pl.ds(pl.multiple_of(window_start(lo) * nw, nw), win * nw)
        return pltpu.make_async_copy(g_ref.at[rows, :], gbuf.at[slot], sem.at[slot])

    def window_sum(lo, slot):
        pos = lax.broadcasted_iota(I32, (1, win), 1) + window_start(lo)
        own = (pos >= s0) & (pos < s1) & (pos >= lo)
        sel = jnp.where(own, 1.0, 0.0).astype(BF16)
        rows = _unpack_rows(lambda c: gbuf[slot, pl.ds(c, win, stride=nw), :], nw)
        return _dot(sel, rows)

    slot = t % 2

    @pl.when(t == 0)
    def _():
        window_copy(first, slot).start()

    window_copy(first, slot).wait()

    @pl.when(t + 1 < pl.num_programs(0))
    def _():
        window_copy(last, 1 - slot).start()

    def extra_window(wi, acc):
        lo = first + wi * win
        cp = window_copy(lo, 2)
        cp.start()
        cp.wait()
        return acc + window_sum(lo, 2)

    nwin = (last - first + win - 1) // win
    acc = lax.fori_loop(1, nwin, extra_window, window_sum(first, slot))
    o_ref[...] = x_ref[...] + gate_ref[...] * acc


def combine(x1, seg0, seg1, tile_bounds, gate2, g, seq):
    t, d = x1.shape
    tm = _pick_tile(seq, 256)
    nw = d // 2 // LANES
    npairs = g.shape[0] // nw
    win = min(COMBINE_WINDOW, npairs)
    row = lambda i, tb: (i, 0)
    grid_spec = pltpu.PrefetchScalarGridSpec(
        num_scalar_prefetch=1,
        grid=(t // tm,),
        in_specs=[pl.BlockSpec((tm, d), row), pl.BlockSpec((tm, 1), row), pl.BlockSpec((tm, 1), row),
                  pl.BlockSpec((None, 1, d), _mod_index(gate2.shape[0], seq // tm)),
                  pl.BlockSpec(memory_space=pl.ANY)],
        out_specs=pl.BlockSpec((tm, d), row),
        scratch_shapes=[pltpu.VMEM((3, win * nw, LANES), U32), pltpu.SemaphoreType.DMA((3,))],
    )
    return pl.pallas_call(
        functools.partial(_combine_body, win=win, d=d, npairs=npairs),
        grid_spec=grid_spec,
        out_shape=jax.ShapeDtypeStruct((t, d), F32),
        compiler_params=_cparams(1),
    )(tile_bounds, x1, seg0, seg1, gate2, g)


def _dft_channel_table():
    k = np.arange(FOURIER_CH)
    ang = 2.0 * np.pi * ((k[:, None] * k[None, :]) % FOURIER_CH) / FOURIER_CH
    return jnp.asarray(np.concatenate([np.cos(ang), np.sin(ang)], axis=1), BF16)


def _dft_position_tables(seq):
    r = 1 << ((seq.bit_length() - 1) // 2)
    n = jnp.arange(seq, dtype=I32)[None, :]
    k1 = jnp.arange(seq // r, dtype=I32)[:, None] * r
    k2 = jnp.arange(r, dtype=I32)[:, None]
    ang = lambda k: ((k * n) % seq).astype(F32) * (2.0 * math.pi / seq)
    a, b = ang(k1), ang(k2)
    ca, sa = jnp.cos(a)[:, None, :], jnp.sin(a)[:, None, :]
    cb, sb = jnp.cos(b)[None, :, :], jnp.sin(b)[None, :, :]
    scale = 1.0 / math.sqrt(seq * FOURIER_CH)
    cos = ((ca * cb - sa * sb) * scale).astype(BF16).reshape(seq, seq)
    sin = ((sa * cb + ca * sb) * scale).astype(BF16).reshape(seq, seq)
    return cos, sin


def _rope_pattern(seq, width):
    nf = width // 4
    pos = np.arange(seq)
    inv = ROPE_BASE ** (-np.arange(nf, dtype=np.float64) / nf)
    ar = (pos // GRID_W)[:, None] * inv
    ac = (pos % GRID_W)[:, None] * inv
    cos = np.concatenate([np.cos(ar), np.cos(ar), np.cos(ac), np.cos(ac)], axis=1)
    sin = np.concatenate([-np.sin(ar), np.sin(ar), -np.sin(ac), np.sin(ac)], axis=1)
    return cos, sin


def _rope_tables(seq):
    cd, sd = _rope_pattern(seq, DIFF_HD)
    diff = (np.tile(cd, (1, 2 * DIFF_HEADS)), np.tile(sd, (1, 2 * DIFF_HEADS)))
    cm, sm = _rope_pattern(seq, MLA_ROPE)
    pad_r = MLA_HEAD_PAD - MLA_QK_HD
    cm = np.concatenate([np.ones((seq, MLA_NOPE)), cm, np.ones((seq, pad_r))], axis=1)
    sm = np.concatenate([np.zeros((seq, MLA_NOPE)), sm, np.zeros((seq, pad_r))], axis=1)
    mla = (cm, sm)
    as_f32 = lambda pair: tuple(jnp.asarray(a, F32) for a in pair)
    return as_f32(diff), as_f32(mla)


def _pad_heads(a, width):
    lead = a.shape[:-1]
    a = a.reshape(lead + (MLA_HEADS, width))
    a = jnp.pad(a, [(0, 0)] * len(lead) + [(0, 0), (0, MLA_HEAD_PAD - width)])
    return a.reshape(lead + (MLA_PAD_W,))


def _layer_weights(p, l):
    w_in = p["w_in"][l]
    c0 = FOURIER_W
    c1 = c0 + 3 * DIFF_W
    c2 = c1 + MLA_Q_LORA + MLA_KV_LORA
    c3 = c2 + MLA_ROPE
    kr_cols = jnp.pad(w_in[:, c2:c3], ((0, 0), (MLA_NOPE, LANES - MLA_QK_HD)))
    w_kvb = p["mla_w_kvb"][l].reshape(MLA_KV_LORA, MLA_HEADS, MLA_NOPE + MLA_V)
    router = jnp.pad(p["moe_w_router"][l], ((0, 0), (0, LANES - N_EXPERTS)))
    router_hi = router.astype(BF16)
    tile = lambda v, reps: jnp.tile(v, reps)[None, :].astype(F32)
    pad_gain = lambda v: jnp.pad(v, (0, MLA_HEAD_PAD - MLA_QK_HD))[None, :].astype(F32)
    w_qb = _pad_heads(p["mla_w_qb"][l], MLA_QK_HD)
    lane = np.arange(LANES)
    rot = (lane >= MLA_NOPE) & (lane < MLA_QK_HD)
    off = MLA_ROPE // 4
    partner = np.where(rot, np.where((lane & off) == 0, lane + off, lane - off), lane)
    w_qb_partner = ((w_qb.reshape(MLA_Q_LORA, MLA_HEADS, LANES) * pad_gain(p["mla_qnorm_w"][l]))[:, :, partner]
                    * jnp.asarray(rot, F32)).reshape(MLA_Q_LORA, MLA_PAD_W)
    return dict(
        fourier=w_in[:, :c0].astype(BF16),
        diff=w_in[:, c0:c1].astype(BF16),
        mla_in=jnp.concatenate([w_in[:, c1:c2], kr_cols], axis=1).astype(BF16),
        gates=w_in[:, c3:].astype(BF16),
        diff_qn=tile(p["diff_qnorm_w"][l], 2 * DIFF_HEADS),
        diff_kn=tile(p["diff_knorm_w"][l], 2 * DIFF_HEADS),
        subln=p["diff_subln_w"][l][None, :].astype(F32),
        lamv=jnp.pad(jnp.stack([p["diff_lambda_q1"][l], p["diff_lambda_k1"][l],
                                p["diff_lambda_q2"][l], p["diff_lambda_k2"][l]]).astype(F32),
                     ((0, SUBLANES - 4), (0, LANES - DIFF_HD))),
        mla=dict(
            qa_norm=p["mla_qa_norm_w"][l][None, :].astype(F32),
            w_qb=w_qb.astype(BF16),
            w_qb_rope=jnp.concatenate([w_qb, w_qb_partner], axis=1).astype(BF16),
            q_norm=pad_gain(p["mla_qnorm_w"][l]),
            kva_norm=p["mla_kva_norm_w"][l][None, :].astype(F32),
            w_k=_pad_heads(w_kvb[:, :, :MLA_NOPE].reshape(MLA_KV_LORA, -1), MLA_NOPE).astype(BF16),
            w_v=_pad_heads(w_kvb[:, :, MLA_NOPE:].reshape(MLA_KV_LORA, -1), MLA_V).astype(BF16),
            k_norm=pad_gain(p["mla_knorm_w"][l]),
        ),
        merge=dict(
            br_f=p["w_br_fourier"][l].astype(BF16),
            br_d=p["w_br_diff"][l].astype(BF16),
            br_m=p["w_br_mla"][l].astype(BF16),
            out=p["w_out"][l].astype(BF16),
            router_hi=router_hi,
            router_hi_lo=jnp.concatenate([router_hi, (router - router_hi.astype(F32)).astype(BF16)], axis=1),
        ),
        layer=l,
        norm1=p["norm1_w"][l][None, :].astype(F32),
        norm2=p["norm2_w"][l][None, :].astype(F32),
    )


def _trunk_layer(x, mods, w, lam_init, nbatch, seq, tabs, ctx):
    t, d = x.shape
    sh1, sc1, g1, sh2, sc2, g2 = mods
    h, gates = norm_gates_proj(x, w["norm1"], sc1, sh1, w["gates"], seq)
    ab = fourier_channel(h, w["fourier"], tabs["dft_ch"])
    y_f = fourier_position(ab, tabs["dft_cos"], tabs["dft_sin"], seq)
    new_ctx = None
    if ctx is None:
        q_d, k_d, v_d, k_d32, v_d32 = diff_qkv(h, w["diff"], w["diff_qn"], w["diff_kn"], seq, None)
        q_m, k_m, v_m, ckv32, krb32 = mla_proj(h, w["mla_in"], w["mla"], seq, None)
        new_ctx = (k_d32, v_d32, ckv32, krb32[:, MLA_NOPE:MLA_QK_HD])
        ctx_d = ctx_m = None
    else:
        q_d, k_d, v_d = diff_qkv(h, w["diff"], w["diff_qn"], w["diff_kn"], seq, tabs["rope_diff"])
        q_m, k_m, v_m = mla_proj(h, w["mla_in"], w["mla"], seq, tabs["rope_mla"])
        kd_c, vd_c, ckv_c, kr_c = ctx
        ctx_d = (kd_c.astype(BF16), vd_c.astype(BF16))
        krb_c = jnp.pad(kr_c, ((0, 0), (MLA_NOPE, LANES - MLA_QK_HD)))
        ctx_m = mla_ctx_keys(ckv_c, krb_c, w["mla"])
    o_d = attention(q_d, k_d, v_d, ctx_d, nbatch=nbatch, seq=seq, heads=DIFF_HEADS, nmaps=2,
                    lamv=w["lamv"], subln=w["subln"], lam_init=lam_init)
    o_m = attention(q_m, k_m, v_m, ctx_m, nbatch=nbatch, seq=seq, heads=MLA_HEADS, nmaps=1, compact=True)
    x1, h2p, aff_t = merge_out(x, y_f, o_d, o_m, gates, w["merge"], g1, w["norm2"], sc2, sh2, seq)
    idx, dst, gv, seg = route(aff_t, nbatch, seq)
    cap = idx.shape[1]
    rt = _pick_tile(nbatch * cap // 2, 512)
    by_expert = lambda a: jnp.transpose(a[:, :, :N_EXPERTS], (2, 0, 1))
    idx_e = by_expert(idx).reshape(-1, 1, rt)
    dst_e = by_expert(dst).reshape(-1, 1, rt)
    gv_e = by_expert(gv).reshape(-1, 1)
    g = moe_experts(idx_e, dst_e, gv_e, h2p, *w["moe"], w["layer"], d)
    seg0 = seg[:, 0, :].reshape(t, 1)
    seg1 = seg[:, 1, :].reshape(t, 1)
    tm = _pick_tile(seq, 256)
    npairs = nbatch * N_EXPERTS * cap
    tile_bounds = jnp.concatenate([seg0[::tm, 0], jnp.full((1,), npairs, I32)])
    x2 = combine(x1, seg0, seg1, tile_bounds, g2, g, seq)
    return x2, new_ctx


def kernel(x_prompt, x_sample, cache_diff_k, cache_diff_v, cache_mla_ckv, cache_mla_krope, c, c_ctx, w_ada, b_ada, norm1_w, norm2_w, w_in, diff_qnorm_w, diff_knorm_w, diff_lambda_q1, diff_lambda_k1, diff_lambda_q2, diff_lambda_k2, diff_subln_w, mla_qa_norm_w, mla_w_qb, mla_kva_norm_w, mla_w_kvb, mla_qnorm_w, mla_knorm_w, w_br_fourier, w_br_diff, w_br_mla, w_out, moe_w_router, moe_w_gate, moe_w_up, moe_w_down):
    params = dict(w_in=w_in, norm1_w=norm1_w, norm2_w=norm2_w,
                  diff_qnorm_w=diff_qnorm_w, diff_knorm_w=diff_knorm_w,
                  diff_lambda_q1=diff_lambda_q1, diff_lambda_k1=diff_lambda_k1,
                  diff_lambda_q2=diff_lambda_q2, diff_lambda_k2=diff_lambda_k2, diff_subln_w=diff_subln_w,
                  mla_qa_norm_w=mla_qa_norm_w, mla_w_qb=mla_w_qb, mla_kva_norm_w=mla_kva_norm_w,
                  mla_w_kvb=mla_w_kvb, mla_qnorm_w=mla_qnorm_w, mla_knorm_w=mla_knorm_w,
                  w_br_fourier=w_br_fourier, w_br_diff=w_br_diff, w_br_mla=w_br_mla, w_out=w_out,
                  moe_w_router=moe_w_router, moe_w_gate=moe_w_gate, moe_w_up=moe_w_up, moe_w_down=moe_w_down)
    bp, lp, d = x_prompt.shape
    bs, ls, _ = x_sample.shape
    depth = w_in.shape[0]
    past = cache_diff_k.shape[2]

    cond = jnp.concatenate([c, c_ctx[None, :], jnp.zeros((COND_ROWS - bs - 1, d), F32)], axis=0)
    mods = adaln(cond, w_ada, b_ada)

    dft_ch = _dft_channel_table()
    rope_diff, rope_mla = _rope_tables(ls)
    tabs_p = dict(dft_ch=dft_ch)
    tabs_p["dft_cos"], tabs_p["dft_sin"] = _dft_position_tables(lp)
    tabs_s = dict(dft_ch=dft_ch, rope_diff=rope_diff, rope_mla=rope_mla)
    tabs_s["dft_cos"], tabs_s["dft_sin"] = _dft_position_tables(ls)

    y_p = x_prompt.reshape(bp * lp, d)
    y_s = x_sample.reshape(bs * ls, d)
    new_ctx = []
    moe_weights = (moe_w_gate.astype(BF16), moe_w_up.astype(BF16), moe_w_down.astype(BF16))
    for l in range(depth):
        w = _layer_weights(params, l)
        w["moe"] = moe_weights
        lam_init = 0.8 - 0.6 * math.exp(-0.3 * l)
        m = mods[l].reshape(COND_ROWS, N_ADA, d)
        mods_s = [m[:bs, j][:, None, :] for j in range(N_ADA)]
        mods_p = [m[bs:bs + 1, j][:, None, :] for j in range(N_ADA)]
        y_p, ctx_l = _trunk_layer(y_p, mods_p, w, lam_init, bp, lp, tabs_p, None)
        new_ctx.append(ctx_l)
        ctx = (cache_diff_k[:, l].reshape(bs * past, DIFF_W), cache_diff_v[:, l].reshape(bs * past, DIFF_W),
               cache_mla_ckv[:, l].reshape(bs * past, MLA_KV_LORA), cache_mla_krope[:, l].reshape(bs * past, MLA_ROPE))
        y_s, _ = _trunk_layer(y_s, mods_s, w, lam_init, bs, ls, tabs_s, ctx)

    stack = lambda j, shape: jnp.stack([n[j].reshape((bp, lp) + shape) for n in new_ctx], axis=1)
    return (y_p.reshape(bp, lp, d), y_s.reshape(bs, ls, d),
            stack(0, (DIFF_HEADS, 2, DIFF_HD)), stack(1, (DIFF_HEADS, 2 * DIFF_HD)),
            stack(2, (MLA_KV_LORA,)), stack(3, (MLA_ROPE,)))
```

```python
import functools
import math

import jax
import jax.numpy as jnp
import numpy as np
from jax import lax
from jax.experimental import pallas as pl
from jax.experimental.pallas import tpu as pltpu

F32, BF16, I32, U32 = jnp.float32, jnp.bfloat16, jnp.int32, jnp.uint32

GRID_W = 64
ROPE_BASE = 10000.0
EPS = 1e-6
N_ADA = 6
FOURIER_GROUPS = 4
FOURIER_CH = 128
FOURIER_W = FOURIER_GROUPS * FOURIER_CH
DIFF_HEADS = 4
DIFF_HD = 64
DIFF_W = DIFF_HEADS * 2 * DIFF_HD
MLA_HEADS = 8
MLA_NOPE = 64
MLA_ROPE = 32
MLA_QK_HD = MLA_NOPE + MLA_ROPE
MLA_V = 64
MLA_Q_LORA = 384
MLA_KV_LORA = 256
N_EXPERTS = 16
EC_FACTOR = 2

LANES = 128
SUBLANES = 8
VMEM_LIMIT_BYTES = 56 * 1024 * 1024
LOG2E = math.log2(math.e)
MLA_HEAD_PAD = LANES
MLA_PAD_W = MLA_HEADS * MLA_HEAD_PAD
COND_ROWS = 16


def _cparams(n_axes, **kw):
    return pltpu.CompilerParams(dimension_semantics=("arbitrary",) * n_axes,
                                vmem_limit_bytes=VMEM_LIMIT_BYTES, **kw)


def _dot(a, b):
    return jnp.dot(a, b, preferred_element_type=F32)


def _dot_nt(a, b):
    return lax.dot_general(a, b, (((1,), (1,)), ((), ())), preferred_element_type=F32)


def _pick_tile(n, target):
    t = min(n, target)
    while n % t:
        t //= 2
    return t


def _pack_bf16_pairs(x):
    half = x.shape[1] // 2
    bits = pltpu.bitcast(x.astype(jnp.bfloat16).astype(F32), U32)
    return (bits[:, :half] >> 16) | bits[:, half:]


def _store_token_tiles(ref, packed):
    rows, w = packed.shape
    n = w // LANES
    for c in range(n):
        ref[pl.ds(c, rows, stride=n), :] = packed[:, c * LANES:(c + 1) * LANES]


def _unpack_rows(load_chunk, nchunk):
    lo, hi = [], []
    for c in range(nchunk):
        wds = load_chunk(c)
        lo.append(pltpu.bitcast(wds << 16, F32).astype(BF16))
        hi.append(pltpu.bitcast(wds & jnp.uint32(0xFFFF0000), F32).astype(BF16))
    return jnp.concatenate(lo + hi, axis=1)


def _adaln_body(c_ref, w_ref, b_ref, o_ref):
    c = c_ref[...]
    a = (c * jax.nn.sigmoid(c)).astype(BF16)
    o_ref[...] = _dot(a, w_ref[...].astype(BF16)) + b_ref[...]


def adaln(cond, w_ada, b_ada):
    depth, d, n = w_ada.shape
    tn = _pick_tile(n, 2048)
    return pl.pallas_call(
        _adaln_body,
        grid=(depth, n // tn),
        in_specs=[pl.BlockSpec((COND_ROWS, d), lambda l, j: (0, 0)),
                  pl.BlockSpec((None, d, tn), lambda l, j: (l, 0, j)),
                  pl.BlockSpec((None, 1, tn), lambda l, j: (l, 0, j))],
        out_specs=pl.BlockSpec((None, COND_ROWS, tn), lambda l, j: (l, 0, j)),
        out_shape=jax.ShapeDtypeStruct((depth, COND_ROWS, n), F32),
        compiler_params=_cparams(2),
    )(cond, w_ada, b_ada.reshape(depth, 1, n))


def _norm_mod(x, nw, sc, sh):
    r = lax.rsqrt(jnp.mean(x * x, axis=-1, keepdims=True) + EPS)
    return (x * r) * nw * (1.0 + sc) + sh


def _mod_index(nb, tiles_per_batch):
    if nb == 1:
        return lambda i, *_: (0, 0, 0)
    return lambda i, *_: (i // tiles_per_batch, 0, 0)


def _gates_body(x_ref, nw_ref, sc_ref, sh_ref, w_ref, h_ref, o_ref):
    @pl.when(pl.program_id(1) == 0)
    def _():
        h_ref[...] = _norm_mod(x_ref[...], nw_ref[...], sc_ref[...], sh_ref[...]).astype(BF16)

    o_ref[...] = jax.nn.sigmoid(_dot(h_ref[...], w_ref[...])).astype(BF16)


def norm_gates_proj(x, nw, sc, sh, w, seq):
    t, d = x.shape
    n = w.shape[1]
    nb = sc.shape[0]
    tm, tn = _pick_tile(seq if nb > 1 else t, 1024), _pick_tile(n, 1024)
    mod_spec = pl.BlockSpec((None, 1, d), _mod_index(nb, seq // tm))
    return pl.pallas_call(
        _gates_body,
        grid=(t // tm, n // tn),
        in_specs=[pl.BlockSpec((tm, d), lambda i, j: (i, 0)),
                  pl.BlockSpec((1, d), lambda i, j: (0, 0)),
                  mod_spec, mod_spec,
                  pl.BlockSpec((d, tn), lambda i, j: (0, j))],
        out_specs=[pl.BlockSpec((tm, d), lambda i, j: (i, 0)),
                   pl.BlockSpec((tm, tn), lambda i, j: (i, j))],
        out_shape=[jax.ShapeDtypeStruct((t, d), BF16), jax.ShapeDtypeStruct((t, n), BF16)],
        compiler_params=_cparams(2),
    )(x, nw, sc, sh, w)


def _fourier_ch_body(h_ref, w_ref, cs_ref, o_ref):
    u = _dot(h_ref[...], w_ref[...]).astype(BF16)
    for g in range(FOURIER_GROUPS):
        ab = _dot(u[:, g * FOURIER_CH:(g + 1) * FOURIER_CH], cs_ref[...])
        o_ref[:, g * FOURIER_CH:(g + 1) * FOURIER_CH] = ab[:, :FOURIER_CH].astype(BF16)
        o_ref[:, FOURIER_W + g * FOURIER_CH:FOURIER_W + (g + 1) * FOURIER_CH] = ab[:, FOURIER_CH:].astype(BF16)


def fourier_channel(h, w, cs):
    t, d = h.shape
    tm = _pick_tile(t, 1024)
    return pl.pallas_call(
        _fourier_ch_body,
        grid=(t // tm,),
        in_specs=[pl.BlockSpec((tm, d), lambda i: (i, 0)),
                  pl.BlockSpec((d, FOURIER_W), lambda i: (0, 0)),
                  pl.BlockSpec((FOURIER_CH, 2 * FOURIER_CH), lambda i: (0, 0))],
        out_specs=pl.BlockSpec((tm, 2 * FOURIER_W), lambda i: (i, 0)),
        out_shape=jax.ShapeDtypeStruct((t, 2 * FOURIER_W), BF16),
        compiler_params=_cparams(1),
    )(h, w, cs)


def _fourier_pos_body(c_ref, s_ref, ab_ref, o_ref):
    y = _dot(c_ref[...], ab_ref[:, :FOURIER_W]) - _dot(s_ref[...], ab_ref[:, FOURIER_W:])
    o_ref[...] = y.astype(BF16)


def fourier_position(ab, cpos, spos, seq):
    t = ab.shape[0]
    tr = _pick_tile(seq, 512)
    nr = seq // tr
    return pl.pallas_call(
        _fourier_pos_body,
        grid=(t // seq, nr),
        in_specs=[pl.BlockSpec((tr, seq), lambda b, r: (r, 0)),
                  pl.BlockSpec((tr, seq), lambda b, r: (r, 0)),
                  pl.BlockSpec((seq, 2 * FOURIER_W), lambda b, r: (b, 0))],
        out_specs=pl.BlockSpec((tr, FOURIER_W), lambda b, r: (b * nr + r, 0)),
        out_shape=jax.ShapeDtypeStruct((t, FOURIER_W), BF16),
        compiler_params=_cparams(2),
    )(cpos, spos, ab)


def _rope(x, cos, sin, off):
    w = x.shape[1]
    lane = lax.broadcasted_iota(I32, (1, w), 1)
    first = (lane & off) == 0
    partner = jnp.where(first, pltpu.roll(x, w - off, 1), pltpu.roll(x, off, 1))
    return x * cos + partner * sin


def _half_tile_norm(x):
    outs = []
    lane = lax.broadcasted_iota(I32, (1, LANES), 1)
    low = lane < DIFF_HD
    for j in range(x.shape[1] // LANES):
        seg = x[:, j * LANES:(j + 1) * LANES]
        sq = seg * seg
        s_lo = jnp.sum(jnp.where(low, sq, 0.0), axis=-1, keepdims=True)
        s_hi = jnp.sum(jnp.where(low, 0.0, sq), axis=-1, keepdims=True)
        ms = jnp.where(low, s_lo, s_hi) * (1.0 / DIFF_HD)
        outs.append(seg * lax.rsqrt(ms + EPS))
    return jnp.concatenate(outs, axis=1)


def _diff_qkv_body(*refs, rope):
    if rope:
        h_ref, w_ref, qn_ref, kn_ref, cos_ref, sin_ref, q_out, k_out, v_out = refs
    else:
        h_ref, w_ref, qn_ref, kn_ref, q_out, k_out, v_out, k32_out, v32_out = refs
    z = _dot(h_ref[...], w_ref[...])
    q = _half_tile_norm(z[:, :DIFF_W]) * qn_ref[...]
    k = _half_tile_norm(z[:, DIFF_W:2 * DIFF_W]) * kn_ref[...]
    v = z[:, 2 * DIFF_W:]
    if rope:
        q = _rope(q, cos_ref[...], sin_ref[...], DIFF_HD // 4)
        k = _rope(k, cos_ref[...], sin_ref[...], DIFF_HD // 4)
    else:
        k32_out[...] = k
        v32_out[...] = v
    q_out[...] = (q * (DIFF_HD ** -0.5 * LOG2E)).astype(BF16)
    k_out[...] = k.astype(BF16)
    v_out[...] = v.astype(BF16)


def diff_qkv(h, w, qn, kn, seq, rope_tabs):
    t, d = h.shape
    rope = rope_tabs is not None
    tm = _pick_tile(seq, 512)
    row = lambda i: (i, 0)
    const = lambda i: (0, 0)
    in_specs = [pl.BlockSpec((tm, d), row), pl.BlockSpec((d, 3 * DIFF_W), const),
                pl.BlockSpec((1, DIFF_W), const), pl.BlockSpec((1, DIFF_W), const)]
    args = [h, w, qn, kn]
    out_shape = [jax.ShapeDtypeStruct((t, DIFF_W), BF16)] * 3
    if rope:
        nt = seq // tm
        tab = pl.BlockSpec((tm, DIFF_W), lambda i: (i % nt, 0))
        in_specs += [tab, tab]
        args += list(rope_tabs)
    else:
        out_shape = out_shape + [jax.ShapeDtypeStruct((t, DIFF_W), F32)] * 2
    return pl.pallas_call(
        functools.partial(_diff_qkv_body, rope=rope),
        grid=(t // tm,),
        in_specs=in_specs,
        out_specs=[pl.BlockSpec((tm, DIFF_W), row)] * len(out_shape),
        out_shape=out_shape,
        compiler_params=_cparams(1),
    )(*args)


def _mla_keys(ckv_bf, krb, wk_ref, wv_ref, kn_ref, rope_tab):
    kn = _dot(ckv_bf, wk_ref[...])
    gain = kn_ref[...]
    kr = krb * gain
    if rope_tab is not None:
        kr = _rope(kr, rope_tab[0], rope_tab[1], MLA_ROPE // 4)
    kr_sq = jnp.sum(krb * krb, axis=-1, keepdims=True)
    outs = []
    for j in range(MLA_HEADS):
        seg = kn[:, j * LANES:(j + 1) * LANES]
        ms = (jnp.sum(seg * seg, axis=-1, keepdims=True) + kr_sq) * (1.0 / MLA_QK_HD)
        outs.append((seg * gain + kr) * lax.rsqrt(ms + EPS))
    return jnp.concatenate(outs, axis=1), _dot(ckv_bf, wv_ref[...])


def _mla_proj_body(*refs, rope):
    (h_ref, w_ref, qan_ref, wqb_ref, qn_ref, kvan_ref, wk_ref, wv_ref, kn_ref), refs = refs[:9], refs[9:]
    if rope:
        cos_ref, sin_ref, q_out, k_out, v_out = refs
        rope_tab = (cos_ref[...], sin_ref[...])
    else:
        q_out, k_out, v_out, ckv32_out, kr32_out = refs
        rope_tab = None
    z = _dot(h_ref[...], w_ref[...])
    q_a = z[:, :MLA_Q_LORA]
    kv_a = z[:, MLA_Q_LORA:MLA_Q_LORA + MLA_KV_LORA]
    krb = z[:, MLA_Q_LORA + MLA_KV_LORA:]
    c_q = q_a * lax.rsqrt(jnp.mean(q_a * q_a, axis=-1, keepdims=True) + EPS) * qan_ref[...]
    zq = _dot(c_q.astype(BF16), wqb_ref[...])
    gain = qn_ref[...]
    outs = []
    for j in range(MLA_HEADS):
        seg = zq[:, j * LANES:(j + 1) * LANES]
        ms = jnp.sum(seg * seg, axis=-1, keepdims=True) * (1.0 / MLA_QK_HD)
        val = seg * gain
        if rope:
            val = val * rope_tab[0] + zq[:, MLA_PAD_W + j * LANES:MLA_PAD_W + (j + 1) * LANES] * rope_tab[1]
        outs.append(val * (lax.rsqrt(ms + EPS) * (MLA_QK_HD ** -0.5 * LOG2E)))
    q_out[...] = jnp.concatenate(outs, axis=1).astype(BF16)
    c_kv = kv_a * lax.rsqrt(jnp.mean(kv_a * kv_a, axis=-1, keepdims=True) + EPS) * kvan_ref[...]
    k, v = _mla_keys(c_kv.astype(BF16), krb, wk_ref, wv_ref, kn_ref, rope_tab)
    if not rope:
        ckv32_out[...] = c_kv
        kr32_out[...] = krb
    k_out[...] = k.astype(BF16)
    v_out[...] = v.astype(BF16)


def mla_proj(h, w, p, seq, rope_tabs):
    t, d = h.shape
    rope = rope_tabs is not None
    tm = _pick_tile(seq, 512)
    row = lambda i: (i, 0)
    const = lambda i: (0, 0)
    wcols = MLA_Q_LORA + MLA_KV_LORA + LANES
    w_qb = p["w_qb_rope"] if rope else p["w_qb"]
    in_specs = [pl.BlockSpec((tm, d), row), pl.BlockSpec((d, wcols), const),
                pl.BlockSpec((1, MLA_Q_LORA), const), pl.BlockSpec(w_qb.shape, const),
                pl.BlockSpec((1, LANES), const), pl.BlockSpec((1, MLA_KV_LORA), const),
                pl.BlockSpec((MLA_KV_LORA, MLA_PAD_W), const), pl.BlockSpec((MLA_KV_LORA, MLA_PAD_W), const),
                pl.BlockSpec((1, LANES), const)]
    args = [h, w, p["qa_norm"], w_qb, p["q_norm"], p["kva_norm"], p["w_k"], p["w_v"], p["k_norm"]]
    out_shape = [jax.ShapeDtypeStruct((t, MLA_PAD_W), BF16)] * 3
    out_specs = [pl.BlockSpec((tm, MLA_PAD_W), row)] * 3
    if rope:
        nt = seq // tm
        tab = pl.BlockSpec((tm, LANES), lambda i: (i % nt, 0))
        in_specs += [tab, tab]
        args += list(rope_tabs)
    else:
        out_shape += [jax.ShapeDtypeStruct((t, MLA_KV_LORA), F32), jax.ShapeDtypeStruct((t, LANES), F32)]
        out_specs += [pl.BlockSpec((tm, MLA_KV_LORA), row), pl.BlockSpec((tm, LANES), row)]
    return pl.pallas_call(
        functools.partial(_mla_proj_body, rope=rope),
        grid=(t // tm,),
        in_specs=in_specs,
        out_specs=out_specs,
        out_shape=out_shape,
        compiler_params=_cparams(1),
    )(*args)


def _mla_ctx_body(ckv_ref, krb_ref, wk_ref, wv_ref, kn_ref, k_out, v_out):
    k, v = _mla_keys(ckv_ref[...].astype(BF16), krb_ref[...], wk_ref, wv_ref, kn_ref, None)
    k_out[...] = k.astype(BF16)
    v_out[...] = v.astype(BF16)


def mla_ctx_keys(ckv, krb, p):
    t = ckv.shape[0]
    tm = _pick_tile(t, 512)
    row = lambda i: (i, 0)
    const = lambda i: (0, 0)
    return pl.pallas_call(
        _mla_ctx_body,
        grid=(t // tm,),
        in_specs=[pl.BlockSpec((tm, MLA_KV_LORA), row), pl.BlockSpec((tm, LANES), row),
                  pl.BlockSpec((MLA_KV_LORA, MLA_PAD_W), const), pl.BlockSpec((MLA_KV_LORA, MLA_PAD_W), const),
                  pl.BlockSpec((1, LANES), const)],
        out_specs=[pl.BlockSpec((tm, MLA_PAD_W), row)] * 2,
        out_shape=[jax.ShapeDtypeStruct((t, MLA_PAD_W), BF16)] * 2,
        compiler_params=_cparams(1),
    )(ckv, krb, p["w_k"], p["w_v"], p["k_norm"])


SOFTMAX_SUM_FLOOR = 2.0 ** -100


def _attn_body(*refs, nmaps, hp, has_ctx, subln, compact, tk, lam_init):
    refs = list(refs)
    kmax_scr = refs.pop()
    lamv_ref = refs.pop(0) if nmaps == 2 else None
    q_ref, k_ref, v_ref = refs[:3]
    refs = refs[3:]
    segments = []
    if has_ctx:
        segments.append((refs[0], refs[1]))
        refs = refs[2:]
    segments.append((k_ref, v_ref))
    subln_ref = refs.pop(0) if subln else None
    o_ref = refs[0]
    tq = q_ref.shape[0]
    lane = lax.broadcasted_iota(I32, (1, LANES), 1)
    masks = [lane < DIFF_HD, lane >= DIFF_HD] if nmaps == 2 else [None]

    def cols(hh):
        return slice(hh * LANES, (hh + 1) * LANES)

    @pl.when(pl.program_id(2) == 0)
    def _():
        for hh in range(hp):
            for i, mk in enumerate(masks):
                best = jnp.zeros((1, 1), F32)
                for kr, _ in segments:
                    kf = kr[:, cols(hh)].astype(F32)
                    sq = kf * kf if mk is None else jnp.where(mk, kf * kf, 0.0)
                    best = jnp.maximum(best, jnp.max(jnp.sum(sq, axis=-1, keepdims=True), axis=0, keepdims=True))
                row = hh * nmaps + i
                kmax_scr[row:row + 1, :] = jnp.broadcast_to(best, (1, LANES))

    def queries(hh):
        q = q_ref[:, cols(hh)]
        return [q if mk is None else jnp.where(mk, q, jnp.zeros_like(q)) for mk in masks]

    def key_chunks(hh):
        for kr, vr in segments:
            n = kr.shape[0]
            for c0 in range(0, n, tk):
                c1 = min(n, c0 + tk)
                yield kr[c0:c1, cols(hh)], vr[c0:c1, cols(hh)]

    def shifted(hh):
        acc, l = [], []
        for i, qi in enumerate(queries(hh)):
            qf = qi.astype(F32)
            row = hh * nmaps + i
            shift = jnp.sqrt(jnp.sum(qf * qf, axis=-1, keepdims=True) * kmax_scr[row:row + 1, 0:1])
            a = jnp.zeros((tq, LANES), F32)
            li = jnp.zeros((tq, 1), F32)
            for kc, vc in key_chunks(hh):
                p = jnp.exp2(_dot_nt(qi, kc) - shift)
                li = li + jnp.sum(p, axis=-1, keepdims=True)
                a = a + _dot(p.astype(BF16), vc)
            acc.append(a)
            l.append(li)
        return acc, l

    def running_max(hh):
        qs = queries(hh)
        m = [jnp.full((tq, 1), -jnp.inf, F32) for _ in qs]
        l = [jnp.zeros((tq, 1), F32) for _ in qs]
        acc = [jnp.zeros((tq, LANES), F32) for _ in qs]
        for kc, vc in key_chunks(hh):
            for i, qi in enumerate(qs):
                s = _dot_nt(qi, kc)
                mn = jnp.maximum(m[i], jnp.max(s, axis=-1, keepdims=True))
                alpha = jnp.exp2(m[i] - mn)
                p = jnp.exp2(s - mn)
                l[i] = alpha * l[i] + jnp.sum(p, axis=-1, keepdims=True)
                acc[i] = alpha * acc[i] + _dot(p.astype(BF16), vc)
                m[i] = mn
        return acc, l

    def head_out(acc, l):
        o = acc[0] / l[0]
        if nmaps == 2:
            lv = lamv_ref[...]
            lam = (jnp.exp(jnp.sum(lv[0:1] * lv[1:2], axis=-1, keepdims=True))
                   - jnp.exp(jnp.sum(lv[2:3] * lv[3:4], axis=-1, keepdims=True)) + lam_init)
            o = o - lam * (acc[1] / l[1])
        if subln:
            o = o * lax.rsqrt(jnp.mean(o * o, axis=-1, keepdims=True) + EPS) * subln_ref[...] * (1.0 - lam_init)
        return o

    def write(outs):
        if compact:
            for j in range(hp // 2):
                pair = jnp.where(lane < LANES // 2, outs[2 * j], pltpu.roll(outs[2 * j + 1], LANES // 2, 1))
                o_ref[:, cols(j)] = pair.astype(BF16)
        else:
            for hh in range(hp):
                o_ref[:, cols(hh)] = outs[hh].astype(BF16)

    results = [shifted(hh) for hh in range(hp)]
    write([head_out(acc, l) for acc, l in results])

    lmin = functools.reduce(jnp.minimum, [jnp.min(li) for _, l in results for li in l])

    @pl.when(lmin < SOFTMAX_SUM_FLOOR)
    def _():
        write([head_out(*running_max(hh)) for hh in range(hp)])


def attention(q, k, v, ctx, *, nbatch, seq, heads, nmaps, compact=False, lamv=None, subln=None, lam_init=0.0):
    t, w = q.shape
    tq = _pick_tile(seq, 1024)
    nq = seq // tq
    hp = heads if seq <= 512 else (2 if compact else 1)
    assert hp * nmaps <= SUBLANES and heads % hp == 0
    wo = hp * LANES // 2 if compact else hp * LANES
    qspec = pl.BlockSpec((tq, hp * LANES), lambda b, h, i: (b * nq + i, h))
    kvspec = pl.BlockSpec((seq, hp * LANES), lambda b, h, i: (b, h))
    in_specs, args = [], []
    if nmaps == 2:
        in_specs.append(pl.BlockSpec((SUBLANES, LANES), lambda b, h, i: (0, 0)))
        args.append(lamv)
    in_specs += [qspec, kvspec, kvspec]
    args += [q, k, v]
    if ctx is not None:
        past = ctx[0].shape[0] // nbatch
        cspec = pl.BlockSpec((past, hp * LANES), lambda b, h, i: (b, h))
        in_specs += [cspec, cspec]
        args += list(ctx)
    if subln is not None:
        in_specs.append(pl.BlockSpec((1, LANES), lambda b, h, i: (0, 0)))
        args.append(subln)
    return pl.pallas_call(
        functools.partial(_attn_body, nmaps=nmaps, hp=hp, has_ctx=ctx is not None, subln=subln is not None,
                          compact=compact, tk=1024, lam_init=lam_init),
        grid=(nbatch, heads // hp, nq),
        in_specs=in_specs,
        out_specs=pl.BlockSpec((tq, wo), lambda b, h, i: (b * nq + i, h)),
        out_shape=jax.ShapeDtypeStruct((t, w // 2 if compact else w), BF16),
        scratch_shapes=[pltpu.VMEM((SUBLANES, LANES), F32)],
        compiler_params=_cparams(3),
    )(*args)


def _merge_body(x_ref, f_ref, od_ref, om_ref, g0_ref, g1_ref, g2_ref, wf_ref, wd_ref, wm_ref, wo_ref,
                gate1_ref, nw_ref, sc_ref, sh_ref, wrh_ref, wrl_ref,
                x1_ref, h2p_ref, aff_ref):
    merged = g0_ref[...].astype(F32) * _dot(f_ref[...], wf_ref[...])
    merged = merged + g1_ref[...].astype(F32) * _dot(od_ref[...], wd_ref[...])
    merged = merged + g2_ref[...].astype(F32) * _dot(om_ref[...], wm_ref[...])
    x1 = x_ref[...] + gate1_ref[...] * _dot(merged.astype(BF16), wo_ref[...])
    x1_ref[...] = x1
    h2 = _norm_mod(x1, nw_ref[...], sc_ref[...], sh_ref[...])
    _store_token_tiles(h2p_ref, _pack_bf16_pairs(h2))
    h_hi = h2.astype(jnp.bfloat16)
    h_lo = (h2 - h_hi.astype(F32)).astype(BF16)
    hi_terms = _dot(h_hi, wrl_ref[...])
    logits = hi_terms[:, :LANES] + hi_terms[:, LANES:] + _dot(h_lo, wrh_ref[...])
    lane = lax.broadcasted_iota(I32, (1, LANES), 1)
    logits = jnp.where(lane < N_EXPERTS, logits, -1e30)
    e = jnp.exp(logits - jnp.max(logits, axis=-1, keepdims=True))
    aff = e / jnp.sum(e, axis=-1, keepdims=True)
    aff_ref[...] = aff.T[:N_EXPERTS, :]


def merge_out(x, f, od, om, gates, w, gate1, nw, sc, sh, seq):
    t, d = x.shape
    tm = _pick_tile(seq, 512)
    nchunk = d // 2 // LANES
    row = lambda i: (i, 0)
    const = lambda i: (0, 0)
    once = pl.Buffered(1)
    mod_spec = pl.BlockSpec((None, 1, d), _mod_index(sc.shape[0], seq // tm))
    wspec = lambda a: pl.BlockSpec(a.shape, const, pipeline_mode=once)
    in_specs = [pl.BlockSpec((tm, d), row),
                pl.BlockSpec((tm, f.shape[1]), row), pl.BlockSpec((tm, od.shape[1]), row),
                pl.BlockSpec((tm, om.shape[1]), row),
                pl.BlockSpec((tm, d), lambda i: (i, 0)), pl.BlockSpec((tm, d), lambda i: (i, 1)),
                pl.BlockSpec((tm, d), lambda i: (i, 2)),
                wspec(w["br_f"]), wspec(w["br_d"]), wspec(w["br_m"]), wspec(w["out"]),
                mod_spec, pl.BlockSpec((1, d), const), mod_spec, mod_spec,
                wspec(w["router_hi"]), wspec(w["router_hi_lo"])]
    return pl.pallas_call(
        _merge_body,
        grid=(t // tm,),
        in_specs=in_specs,
        out_specs=[pl.BlockSpec((tm, d), row),
                   pl.BlockSpec((tm * nchunk, LANES), row),
                   pl.BlockSpec((N_EXPERTS, tm), lambda i: (0, i))],
        out_shape=[jax.ShapeDtypeStruct((t, d), F32),
                   jax.ShapeDtypeStruct((t * nchunk, LANES), U32),
                   jax.ShapeDtypeStruct((N_EXPERTS, t), F32)],
        compiler_params=_cparams(1),
    )(x, f, od, om, gates, gates, gates, w["br_f"], w["br_d"], w["br_m"], w["out"],
      gate1, nw, sc, sh, w["router_hi"], w["router_hi_lo"])


def _cumsum_lanes(x):
    n = x.shape[1]
    lane = lax.broadcasted_iota(I32, (1, n), 1)
    s = 1
    while s < n:
        x = x + jnp.where(lane >= s, pltpu.roll(x, s, 1), 0)
        s *= 2
    return x


ROUTE_MATMUL_MIN_TOKENS = 1024


def _route_body(aff_ref, idx_ref, dst_ref, gv_ref, seg_ref, key_scr, dst_scr, *, cap, row_chunk):
    b = pl.program_id(0)
    aff = aff_ref[...]
    ne, n = aff.shape
    bits = pltpu.bitcast(aff, I32)

    def search(i, thr):
        cand = thr | (1 << (30 - i))
        cnt = jnp.sum(jnp.where(bits >= cand, 1.0, 0.0), axis=-1, keepdims=True)
        return jnp.where(cnt >= cap, cand, thr)

    thr = lax.fori_loop(0, 31, search, jnp.zeros((ne, 1), I32))
    gt = bits > thr
    eq = (bits == thr).astype(I32)
    need = cap - jnp.sum(jnp.where(gt, 1.0, 0.0), axis=-1, keepdims=True).astype(I32)
    eq_before = _cumsum_lanes(eq) - eq
    sel = jnp.where(gt | ((eq > 0) & (eq_before < need)), 1, 0)
    cum = _cumsum_lanes(sel)
    key_scr[...] = sel * cum

    before = jnp.zeros((1, n), I32)
    for e in range(ne):
        dst_scr[e:e + 1, :] = before
        before = before + sel[e:e + 1, :]
    k_tok = before
    start = _cumsum_lanes(k_tok) - k_tok + b * (ne * cap)
    dst_scr[...] = dst_scr[...] + start
    seg_ref[0:1, :] = start
    seg_ref[1:2, :] = start + k_tok
    seg_ref[2:SUBLANES, :] = jnp.zeros((SUBLANES - 2, n), I32)

    tok = lax.broadcasted_iota(I32, (1, n), 1) + b * n
    lane = lax.broadcasted_iota(I32, (1, LANES), 1)
    idx_ref[...] = jnp.zeros(idx_ref.shape, I32)
    dst_ref[...] = jnp.zeros(dst_ref.shape, I32)
    gv_ref[...] = jnp.zeros(gv_ref.shape, F32)
    byte = lambda v, shift: ((v >> shift) & 255).astype(F32)

    def per_expert(e, carry):
        key = key_scr[pl.ds(e, 1), :]
        dst = dst_scr[pl.ds(e, 1), :]
        af = aff_ref[pl.ds(e, 1), :]
        if n >= ROUTE_MATMUL_MIN_TOKENS:
            a_hi = af.astype(jnp.bfloat16).astype(F32)
            a_mid = (af - a_hi).astype(jnp.bfloat16).astype(F32)
            a_lo = af - a_hi - a_mid
            vals = jnp.concatenate([byte(tok, 16), byte(tok, 8), byte(tok, 0),
                                    byte(dst, 16), byte(dst, 8), byte(dst, 0),
                                    a_hi, a_mid, a_lo] + [jnp.zeros((1, n), F32)] * 7, axis=0).astype(BF16)
        for c0 in range(0, cap, row_chunk):
            slot = lax.broadcasted_iota(I32, (row_chunk, 1), 0) + (c0 + 1)
            hit = key == slot
            if n >= ROUTE_MATMUL_MIN_TOKENS:
                got = _dot_nt(jnp.where(hit, 1.0, 0.0).astype(BF16), vals)
                word = lambda j: (got[:, j:j + 1].astype(I32) * 65536 + got[:, j + 1:j + 2].astype(I32) * 256
                                  + got[:, j + 2:j + 3].astype(I32))
                i_col, d_col = word(0), word(3)
                g_col = got[:, 6:7] + got[:, 7:8] + got[:, 8:9]
            else:
                pick = lambda v: jnp.sum(jnp.where(hit, v, 0.0), axis=-1, keepdims=True)
                i_col, d_col = pick(tok.astype(F32)).astype(I32), pick(dst.astype(F32)).astype(I32)
                g_col = pick(af)
            rows = pl.ds(c0, row_chunk)
            idx_ref[rows, :] = jnp.where(lane == e, i_col, idx_ref[rows, :])
            dst_ref[rows, :] = jnp.where(lane == e, d_col, dst_ref[rows, :])
            gv_ref[rows, :] = jnp.where(lane == e, g_col, gv_ref[rows, :])
        return carry

    lax.fori_loop(0, ne, per_expert, 0)


def route(aff_t, nbatch, seq):
    ne = aff_t.shape[0]
    cap = EC_FACTOR * seq // ne
    row_chunk = min(cap, 256)
    tab = lambda dt: jax.ShapeDtypeStruct((nbatch, cap, LANES), dt)
    tspec = pl.BlockSpec((None, cap, LANES), lambda b: (b, 0, 0))
    return pl.pallas_call(
        functools.partial(_route_body, cap=cap, row_chunk=row_chunk),
        grid=(nbatch,),
        in_specs=[pl.BlockSpec((ne, seq), lambda b: (0, b))],
        out_specs=[tspec, tspec, tspec, pl.BlockSpec((None, SUBLANES, seq), lambda b: (b, 0, 0))],
        out_shape=[tab(I32), tab(I32), tab(F32), jax.ShapeDtypeStruct((nbatch, SUBLANES, seq), I32)],
        scratch_shapes=[pltpu.VMEM((ne, seq), I32), pltpu.VMEM((ne, seq), I32)],
        compiler_params=_cparams(1),
    )(aff_t)


DMA_UNROLL = 8


def _moe_body(idx_a, idx_b, idx_a_next, dst_a, dst_b, dst_b_prev, gv_a, gv_b, h2p_ref, wg_ref, wu_ref, wd_ref, g_ref,
              xa, xb, ya, yb, sem, *, rt, d):
    nw = d // 2 // LANES
    step = pl.program_id(0) * pl.num_programs(1) + pl.program_id(1)
    last = pl.num_programs(0) * pl.num_programs(1) - 1
    in_a, in_b, out_a, out_b = (sem.at[i] for i in range(4))

    def row_tile(r):
        return pl.ds(r * nw if isinstance(r, int) else pl.multiple_of(r * nw, nw), nw)

    def looped(fn):
        def body(i, carry):
            for u in range(DMA_UNROLL):
                fn(i * DMA_UNROLL + u)
            return carry
        lax.fori_loop(0, rt // DMA_UNROLL, body, 0)

    def inline(fn):
        for r in range(rt):
            fn(r)

    def gather(idx_ref, r, xbuf, s):
        src = 0 if idx_ref is None else idx_ref[0, r]
        return pltpu.make_async_copy(h2p_ref.at[row_tile(src), :], xbuf.at[row_tile(r), :], s)

    def scatter(ybuf, r, dst_ref, s):
        dst = 0 if dst_ref is None else dst_ref[0, r]
        return pltpu.make_async_copy(ybuf.at[row_tile(r), :], g_ref.at[row_tile(dst), :], s)

    def drain_gather(xbuf, s):
        looped(lambda r: gather(None, r, xbuf, s).wait())

    def drain_scatter(ybuf, s):
        looped(lambda r: scatter(ybuf, r, None, s).wait())

    def ffn(xbuf, gv_ref):
        x = _unpack_rows(lambda c: xbuf[pl.ds(c, rt, stride=nw), :], nw)
        a = _dot(x, wg_ref[...])
        u = _dot(x, wu_ref[...])
        mid = (a * jax.nn.sigmoid(a) * u).astype(BF16)
        return _pack_bf16_pairs(_dot(mid, wd_ref[...]) * gv_ref[...])

    @pl.when(step == 0)
    def _():
        looped(lambda r: gather(idx_a, r, xa, in_a).start())
        yb[...] = jnp.zeros(yb.shape, U32)

    drain_gather(xa, in_a)
    inline(lambda r: gather(idx_b, r, xb, in_b).start())
    inline(lambda r: scatter(yb, r, dst_b_prev, out_b).start())
    y = ffn(xa, gv_a)

    @pl.when(step > 0)
    def _():
        drain_scatter(ya, out_a)

    _store_token_tiles(ya, y)

    drain_gather(xb, in_b)
    inline(lambda r: gather(idx_a_next, r, xa, in_a).start())
    inline(lambda r: scatter(ya, r, dst_a, out_a).start())
    y = ffn(xb, gv_b)
    drain_scatter(yb, out_b)
    _store_token_tiles(yb, y)

    @pl.when(step == last)
    def _():
        looped(lambda r: scatter(yb, r, dst_b, out_b).start())
        drain_gather(xa, in_a)
        drain_scatter(ya, out_a)
        drain_scatter(yb, out_b)


def moe_experts(idx, dst, gv, h2p, wg, wu, wd, layer, d):
    _, ne, _, ff = wg.shape
    rt = idx.shape[2]
    ntiles = idx.shape[0]
    pairs = ntiles // ne // 2
    nw = d // 2 // LANES
    tile_a = lambda e, j: 2 * (e * pairs + j)
    smem = lambda tile: pl.BlockSpec((None, 1, rt), lambda e, j: (tile(e, j), 0, 0), memory_space=pltpu.SMEM)
    vmem_col = lambda tile: pl.BlockSpec((rt, 1), lambda e, j: (tile(e, j), 0))
    tile_b = lambda e, j: tile_a(e, j) + 1
    tile_a_next = lambda e, j: jnp.minimum(tile_a(e, j) + 2, ntiles - 2)
    tile_b_prev = lambda e, j: jnp.maximum(tile_a(e, j) - 1, 1)
    wspec = lambda a: pl.BlockSpec((None, None) + a.shape[2:], lambda e, j: (layer, e, 0, 0))
    row_buf = pltpu.VMEM((rt * nw, LANES), U32)
    return pl.pallas_call(
        functools.partial(_moe_body, rt=rt, d=d),
        grid=(ne, pairs),
        in_specs=[smem(tile_a), smem(tile_b), smem(tile_a_next), smem(tile_a), smem(tile_b), smem(tile_b_prev),
                  vmem_col(tile_a), vmem_col(tile_b),
                  pl.BlockSpec(memory_space=pl.ANY), wspec(wg), wspec(wu), wspec(wd)],
        out_specs=pl.BlockSpec(memory_space=pl.ANY),
        out_shape=jax.ShapeDtypeStruct((ntiles * rt * nw, LANES), U32),
        scratch_shapes=[row_buf, row_buf, row_buf, row_buf, pltpu.SemaphoreType.DMA((4,))],
        compiler_params=_cparams(2, has_side_effects=True),
    )(idx, idx, idx, dst, dst, dst, gv, gv, h2p, wg, wu, wd)


COMBINE_WINDOW = 640


def _combine_body(tb_ref, x_ref, s0_ref, s1_ref, gate_ref, g_ref, o_ref, gbuf, sem, *, win, d, npairs):
    t = pl.program_id(0)
    nw = d // 2 // LANES
    first = tb_ref[t]
    last = tb_ref[t + 1]
    s0 = s0_ref[...]
    s1 = s1_ref[...]

    def window_start(lo):
        return jnp.minimum(lo, npairs - win)

    def window_copy(lo, slot):
        rows = pl.ds(pl.multiple_of(window_start(lo) * nw, nw), win * nw)
        return pltpu.make_async_copy(g_ref.at[rows, :], gbuf.at[slot], sem.at[slot])

    def window_sum(lo, slot):
        pos = lax.broadcasted_iota(I32, (1, win), 1) + window_start(lo)
        own = (pos >= s0) & (pos < s1) & (pos >= lo)
        sel = jnp.where(own, 1.0, 0.0).astype(BF16)
        rows = _unpack_rows(lambda c: gbuf[slot, pl.ds(c, win, stride=nw), :], nw)
        return _dot(sel, rows)

    slot = t % 2

    @pl.when(t == 0)
    def _():
        window_copy(first, slot).start()

    window_copy(first, slot).wait()

    @pl.when(t + 1 < pl.num_programs(0))
    def _():
        window_copy(last, 1 - slot).start()

    def extra_window(wi, acc):
        lo = first + wi * win
        cp = window_copy(lo, 2)
        cp.start()
        cp.wait()
        return acc + window_sum(lo, 2)

    nwin = (last - first + win - 1) // win
    acc = lax.fori_loop(1, nwin, extra_window, window_sum(first, slot))
    o_ref[...] = x_ref[...] + gate_ref[...] * acc


def combine(x1, seg0, seg1, tile_bounds, gate2, g, seq):
    t, d = x1.shape
    tm = _pick_tile(seq, 256)
    nw = d // 2 // LANES
    npairs = g.shape[0] // nw
    win = min(COMBINE_WINDOW, npairs)
    row = lambda i, tb: (i, 0)
    grid_spec = pltpu.PrefetchScalarGridSpec(
        num_scalar_prefetch=1,
        grid=(t // tm,),
        in_specs=[pl.BlockSpec((tm, d), row), pl.BlockSpec((tm, 1), row), pl.BlockSpec((tm, 1), row),
                  pl.BlockSpec((None, 1, d), _mod_index(gate2.shape[0], seq // tm)),
                  pl.BlockSpec(memory_space=pl.ANY)],
        out_specs=pl.BlockSpec((tm, d), row),
        scratch_shapes=[pltpu.VMEM((3, win * nw, LANES), U32), pltpu.SemaphoreType.DMA((3,))],
    )
    return pl.pallas_call(
        functools.partial(_combine_body, win=win, d=d, npairs=npairs),
        grid_spec=grid_spec,
        out_shape=jax.ShapeDtypeStruct((t, d), F32),
        compiler_params=_cparams(1),
    )(tile_bounds, x1, seg0, seg1, gate2, g)


def _dft_channel_table():
    k = np.arange(FOURIER_CH)
    ang = 2.0 * np.pi * ((k[:, None] * k[None, :]) % FOURIER_CH) / FOURIER_CH
    return jnp.asarray(np.concatenate([np.cos(ang), np.sin(ang)], axis=1), BF16)


def _dft_position_tables(seq):
    r = 1 << ((seq.bit_length() - 1) // 2)
    n = jnp.arange(seq, dtype=I32)[None, :]
    k1 = jnp.arange(seq // r, dtype=I32)[:, None] * r
    k2 = jnp.arange(r, dtype=I32)[:, None]
    ang = lambda k: ((k * n) % seq).astype(F32) * (2.0 * math.pi / seq)
    a, b = ang(k1), ang(k2)
    ca, sa = jnp.cos(a)[:, None, :], jnp.sin(a)[:, None, :]
    cb, sb = jnp.cos(b)[None, :, :], jnp.sin(b)[None, :, :]
    scale = 1.0 / math.sqrt(seq * FOURIER_CH)
    cos = ((ca * cb - sa * sb) * scale).astype(BF16).reshape(seq, seq)
    sin = ((sa * cb + ca * sb) * scale).astype(BF16).reshape(seq, seq)
    return cos, sin


def _rope_pattern(seq, width):
    nf = width // 4
    pos = np.arange(seq)
    inv = ROPE_BASE ** (-np.arange(nf, dtype=np.float64) / nf)
    ar = (pos // GRID_W)[:, None] * inv
    ac = (pos % GRID_W)[:, None] * inv
    cos = np.concatenate([np.cos(ar), np.cos(ar), np.cos(ac), np.cos(ac)], axis=1)
    sin = np.concatenate([-np.sin(ar), np.sin(ar), -np.sin(ac), np.sin(ac)], axis=1)
    return cos, sin


def _rope_tables(seq):
    cd, sd = _rope_pattern(seq, DIFF_HD)
    diff = (np.tile(cd, (1, 2 * DIFF_HEADS)), np.tile(sd, (1, 2 * DIFF_HEADS)))
    cm, sm = _rope_pattern(seq, MLA_ROPE)
    pad_r = MLA_HEAD_PAD - MLA_QK_HD
    cm = np.concatenate([np.ones((seq, MLA_NOPE)), cm, np.ones((seq, pad_r))], axis=1)
    sm = np.concatenate([np.zeros((seq, MLA_NOPE)), sm, np.zeros((seq, pad_r))], axis=1)
    mla = (cm, sm)
    as_f32 = lambda pair: tuple(jnp.asarray(a, F32) for a in pair)
    return as_f32(diff), as_f32(mla)


def _pad_heads(a, width):
    lead = a.shape[:-1]
    a = a.reshape(lead + (MLA_HEADS, width))
    a = jnp.pad(a, [(0, 0)] * len(lead) + [(0, 0), (0, MLA_HEAD_PAD - width)])
    return a.reshape(lead + (MLA_PAD_W,))


def _layer_weights(p, l):
    w_in = p["w_in"][l]
    c0 = FOURIER_W
    c1 = c0 + 3 * DIFF_W
    c2 = c1 + MLA_Q_LORA + MLA_KV_LORA
    c3 = c2 + MLA_ROPE
    kr_cols = jnp.pad(w_in[:, c2:c3], ((0, 0), (MLA_NOPE, LANES - MLA_QK_HD)))
    w_kvb = p["mla_w_kvb"][l].reshape(MLA_KV_LORA, MLA_HEADS, MLA_NOPE + MLA_V)
    router = jnp.pad(p["moe_w_router"][l], ((0, 0), (0, LANES - N_EXPERTS)))
    router_hi = router.astype(BF16)
    tile = lambda v, reps: jnp.tile(v, reps)[None, :].astype(F32)
    pad_gain = lambda v: jnp.pad(v, (0, MLA_HEAD_PAD - MLA_QK_HD))[None, :].astype(F32)
    w_qb = _pad_heads(p["mla_w_qb"][l], MLA_QK_HD)
    lane = np.arange(LANES)
    rot = (lane >= MLA_NOPE) & (lane < MLA_QK_HD)
    off = MLA_ROPE // 4
    partner = np.where(rot, np.where((lane & off) == 0, lane + off, lane - off), lane)
    w_qb_partner = ((w_qb.reshape(MLA_Q_LORA, MLA_HEADS, LANES) * pad_gain(p["mla_qnorm_w"][l]))[:, :, partner]
                    * jnp.asarray(rot, F32)).reshape(MLA_Q_LORA, MLA_PAD_W)
    return dict(
        fourier=w_in[:, :c0].astype(BF16),
        diff=w_in[:, c0:c1].astype(BF16),
        mla_in=jnp.concatenate([w_in[:, c1:c2], kr_cols], axis=1).astype(BF16),
        gates=w_in[:, c3:].astype(BF16),
        diff_qn=tile(p["diff_qnorm_w"][l], 2 * DIFF_HEADS),
        diff_kn=tile(p["diff_knorm_w"][l], 2 * DIFF_HEADS),
        subln=p["diff_subln_w"][l][None, :].astype(F32),
        lamv=jnp.pad(jnp.stack([p["diff_lambda_q1"][l], p["diff_lambda_k1"][l],
                                p["diff_lambda_q2"][l], p["diff_lambda_k2"][l]]).astype(F32),
                     ((0, SUBLANES - 4), (0, LANES - DIFF_HD))),
        mla=dict(
            qa_norm=p["mla_qa_norm_w"][l][None, :].astype(F32),
            w_qb=w_qb.astype(BF16),
            w_qb_rope=jnp.concatenate([w_qb, w_qb_partner], axis=1).astype(BF16),
            q_norm=pad_gain(p["mla_qnorm_w"][l]),
            kva_norm=p["mla_kva_norm_w"][l][None, :].astype(F32),
            w_k=_pad_heads(w_kvb[:, :, :MLA_NOPE].reshape(MLA_KV_LORA, -1), MLA_NOPE).astype(BF16),
            w_v=_pad_heads(w_kvb[:, :, MLA_NOPE:].reshape(MLA_KV_LORA, -1), MLA_V).astype(BF16),
            k_norm=pad_gain(p["mla_knorm_w"][l]),
        ),
        merge=dict(
            br_f=p["w_br_fourier"][l].astype(BF16),
            br_d=p["w_br_diff"][l].astype(BF16),
            br_m=p["w_br_mla"][l].astype(BF16),
            out=p["w_out"][l].astype(BF16),
            router_hi=router_hi,
            router_hi_lo=jnp.concatenate([router_hi, (router - router_hi.astype(F32)).astype(BF16)], axis=1),
        ),
        layer=l,
        norm1=p["norm1_w"][l][None, :].astype(F32),
        norm2=p["norm2_w"][l][None, :].astype(F32),
    )


def _trunk_layer(x, mods, w, lam_init, nbatch, seq, tabs, ctx):
    t, d = x.shape
    sh1, sc1, g1, sh2, sc2, g2 = mods
    h, gates = norm_gates_proj(x, w["norm1"], sc1, sh1, w["gates"], seq)
    ab = fourier_channel(h, w["fourier"], tabs["dft_ch"])
    y_f = fourier_position(ab, tabs["dft_cos"], tabs["dft_sin"], seq)
    new_ctx = None
    if ctx is None:
        q_d, k_d, v_d, k_d32, v_d32 = diff_qkv(h, w["diff"], w["diff_qn"], w["diff_kn"], seq, None)
        q_m, k_m, v_m, ckv32, krb32 = mla_proj(h, w["mla_in"], w["mla"], seq, None)
        new_ctx = (k_d32, v_d32, ckv32, krb32[:, MLA_NOPE:MLA_QK_HD])
        ctx_d = ctx_m = None
    else:
        q_d, k_d, v_d = diff_qkv(h, w["diff"], w["diff_qn"], w["diff_kn"], seq, tabs["rope_diff"])
        q_m, k_m, v_m = mla_proj(h, w["mla_in"], w["mla"], seq, tabs["rope_mla"])
        kd_c, vd_c, ckv_c, kr_c = ctx
        ctx_d = (kd_c.astype(BF16), vd_c.astype(BF16))
        krb_c = jnp.pad(kr_c, ((0, 0), (MLA_NOPE, LANES - MLA_QK_HD)))
        ctx_m = mla_ctx_keys(ckv_c, krb_c, w["mla"])
    o_d = attention(q_d, k_d, v_d, ctx_d, nbatch=nbatch, seq=seq, heads=DIFF_HEADS, nmaps=2,
                    lamv=w["lamv"], subln=w["subln"], lam_init=lam_init)
    o_m = attention(q_m, k_m, v_m, ctx_m, nbatch=nbatch, seq=seq, heads=MLA_HEADS, nmaps=1, compact=True)
    x1, h2p, aff_t = merge_out(x, y_f, o_d, o_m, gates, w["merge"], g1, w["norm2"], sc2, sh2, seq)
    idx, dst, gv, seg = route(aff_t, nbatch, seq)
    cap = idx.shape[1]
    rt = _pick_tile(nbatch * cap // 2, 512)
    by_expert = lambda a: jnp.transpose(a[:, :, :N_EXPERTS], (2, 0, 1))
    idx_e = by_expert(idx).reshape(-1, 1, rt)
    dst_e = by_expert(dst).reshape(-1, 1, rt)
    gv_e = by_expert(gv).reshape(-1, 1)
    g = moe_experts(idx_e, dst_e, gv_e, h2p, *w["moe"], w["layer"], d)
    seg0 = seg[:, 0, :].reshape(t, 1)
    seg1 = seg[:, 1, :].reshape(t, 1)
    tm = _pick_tile(seq, 256)
    npairs = nbatch * N_EXPERTS * cap
    tile_bounds = jnp.concatenate([seg0[::tm, 0], jnp.full((1,), npairs, I32)])
    x2 = combine(x1, seg0, seg1, tile_bounds, g2, g, seq)
    return x2, new_ctx


def kernel(x_prompt, x_sample, cache_diff_k, cache_diff_v, cache_mla_ckv, cache_mla_krope, c, c_ctx, w_ada, b_ada, norm1_w, norm2_w, w_in, diff_qnorm_w, diff_knorm_w, diff_lambda_q1, diff_lambda_k1, diff_lambda_q2, diff_lambda_k2, diff_subln_w, mla_qa_norm_w, mla_w_qb, mla_kva_norm_w, mla_w_kvb, mla_qnorm_w, mla_knorm_w, w_br_fourier, w_br_diff, w_br_mla, w_out, moe_w_router, moe_w_gate, moe_w_up, moe_w_down):
    params = dict(w_in=w_in, norm1_w=norm1_w, norm2_w=norm2_w,
                  diff_qnorm_w=diff_qnorm_w, diff_knorm_w=diff_knorm_w,
                  diff_lambda_q1=diff_lambda_q1, diff_lambda_k1=diff_lambda_k1,
                  diff_lambda_q2=diff_lambda_q2, diff_lambda_k2=diff_lambda_k2, diff_subln_w=diff_subln_w,
                  mla_qa_norm_w=mla_qa_norm_w, mla_w_qb=mla_w_qb, mla_kva_norm_w=mla_kva_norm_w,
                  mla_w_kvb=mla_w_kvb, mla_qnorm_w=mla_qnorm_w, mla_knorm_w=mla_knorm_w,
                  w_br_fourier=w_br_fourier, w_br_diff=w_br_diff, w_br_mla=w_br_mla, w_out=w_out,
                  moe_w_router=moe_w_router, moe_w_gate=moe_w_gate, moe_w_up=moe_w_up, moe_w_down=moe_w_down)
    bp, lp, d = x_prompt.shape
    bs, ls, _ = x_sample.shape
    depth = w_in.shape[0]
    past = cache_diff_k.shape[2]

    cond = jnp.concatenate([c, c_ctx[None, :], jnp.zeros((COND_ROWS - bs - 1, d), F32)], axis=0)
    mods = adaln(cond, w_ada, b_ada)

    dft_ch = _dft_channel_table()
    rope_diff, rope_mla = _rope_tables(ls)
    tabs_p = dict(dft_ch=dft_ch)
    tabs_p["dft_cos"], tabs_p["dft_sin"] = _dft_position_tables(lp)
    tabs_s = dict(dft_ch=dft_ch, rope_diff=rope_diff, rope_mla=rope_mla)
    tabs_s["dft_cos"], tabs_s["dft_sin"] = _dft_position_tables(ls)

    y_p = x_prompt.reshape(bp * lp, d)
    y_s = x_sample.reshape(bs * ls, d)
    new_ctx = []
    moe_weights = (moe_w_gate.astype(BF16), moe_w_up.astype(BF16), moe_w_down.astype(BF16))
    for l in range(depth):
        w = _layer_weights(params, l)
        w["moe"] = moe_weights
        lam_init = 0.8 - 0.6 * math.exp(-0.3 * l)
        m = mods[l].reshape(COND_ROWS, N_ADA, d)
        mods_s = [m[:bs, j][:, None, :] for j in range(N_ADA)]
        mods_p = [m[bs:bs + 1, j][:, None, :] for j in range(N_ADA)]
        y_p, ctx_l = _trunk_layer(y_p, mods_p, w, lam_init, bp, lp, tabs_p, None)
        new_ctx.append(ctx_l)
        ctx = (cache_diff_k[:, l].reshape(bs * past, DIFF_W), cache_diff_v[:, l].reshape(bs * past, DIFF_W),
               cache_mla_ckv[:, l].reshape(bs * past, MLA_KV_LORA), cache_mla_krope[:, l].reshape(bs * past, MLA_ROPE))
        y_s, _ = _trunk_layer(y_s, mods_s, w, lam_init, bs, ls, tabs_s, ctx)

    stack = lambda j, shape: jnp.stack([n[j].reshape((bp, lp) + shape) for n in new_ctx], axis=1)
    return (y_p.reshape(bp, lp, d), y_s.reshape(bs, ls, d),
            stack(0, (DIFF_HEADS, 2, DIFF_HD)), stack(1, (DIFF_HEADS, 2 * DIFF_HD)),
            stack(2, (MLA_KV_LORA,)), stack(3, (MLA_ROPE,)))
```
